```python
import math
import jax, jax.numpy as jnp
from jax import lax
import numpy as np

D_MODEL = 2048
BATCH = 32
SEQ = 256
DEPTH = 4
DEC_BATCH = 8
DEC_SEQ = 1024
PAST_LEN = 256

GRID_W = 64
POS_BASE = 10000.0
N_MIXERS = 2
N_S5_LAYERS = (DEPTH + N_MIXERS - 1) // N_MIXERS
N_ML_LAYERS = DEPTH // N_MIXERS
S5_GROUP_CH = 16
S5_GROUPS = D_MODEL // S5_GROUP_CH
S5_STATE = 64
ML_HEADS = 8
ML_DV = D_MODEL // ML_HEADS
ML_DK = ML_DV // 2
ML_QK = ML_HEADS * ML_DK
ML_PROJ = 2 * ML_QK + 2 * D_MODEL + 4 * ML_HEADS
ML_CHUNK = 64
MOE_GROUPS = 4
MOE_EPG = 8
MOE_EXPERTS = MOE_GROUPS * MOE_EPG
MOE_TOPK = 2
D_EXPERT = D_MODEL // 2
MOE_BLOCK = 128
RMS_EPS = 1e-6
NEG_BIG = -1e30

kernel_name = 'hybrid_s5_mlstm_hmoe_diffusion_step'


def rms_norm(x, g):
    x32 = x.astype(jnp.float32)
    y = x32 * lax.rsqrt(jnp.mean(x32 * x32, axis=-1, keepdims=True) + RMS_EPS)
    return (y * g.astype(jnp.float32)).astype(x.dtype)


def ada_norm(x, g, shift, scale):
    return rms_norm(x, g) * (1 + scale) + shift


def modulation(cond, w, b):
    m = jax.nn.silu(cond) @ w + b
    return jnp.split(m[:, None, :], 6, axis=-1)


def grid_pos_embed(n_tokens):
    rows = n_tokens // GRID_W
    r, col = jnp.meshgrid(jnp.arange(rows, dtype=jnp.float32), jnp.arange(GRID_W, dtype=jnp.float32), indexing='ij')
    quarter = D_MODEL // 4
    omega = 1.0 / (POS_BASE ** (jnp.arange(quarter, dtype=jnp.float32) / quarter))
    def axis_embed(pos):
        ang = pos.reshape(-1, 1) * omega[None, :]
        return jnp.concatenate([jnp.sin(ang), jnp.cos(ang)], axis=-1)
    return jnp.concatenate([axis_embed(r), axis_embed(col)], axis=-1)


def s5_discretize(lam_re, lam_im, log_step, b_re, b_im):
    dt = jnp.exp(log_step)[:, None]
    mag = jnp.exp(lam_re * dt)
    ab_re = mag * jnp.cos(lam_im * dt)
    ab_im = mag * jnp.sin(lam_im * dt)
    den = lam_re * lam_re + lam_im * lam_im
    z_re = ((ab_re - 1) * lam_re + ab_im * lam_im) / den
    z_im = (ab_im * lam_re - (ab_re - 1) * lam_im) / den
    bb_re = z_re[..., None] * b_re - z_im[..., None] * b_im
    bb_im = z_re[..., None] * b_im + z_im[..., None] * b_re
    return ab_re, ab_im, bb_re, bb_im


def s5_combine(e1, e2):
    a1r, a1i, b1r, b1i = e1
    a2r, a2i, b2r, b2i = e2
    return (a1r * a2r - a1i * a2i, a1r * a2i + a1i * a2r,
            a2r * b1r - a2i * b1i + b2r, a2r * b1i + a2i * b1r + b2i)


def s5_direction(u, lam_re, lam_im, log_step, b_re, b_im, c_re, c_im, s0_re, s0_im):
    L = u.shape[1]
    ab_re, ab_im, bb_re, bb_im = s5_discretize(lam_re, lam_im, log_step, b_re, b_im)
    bu_re = jnp.einsum('blgh,gph->blgp', u, bb_re)
    bu_im = jnp.einsum('blgh,gph->blgp', u, bb_im)
    a_shape = (1, L) + ab_re.shape
    acc_re, acc_im, h_re, h_im = lax.associative_scan(
        s5_combine, (jnp.broadcast_to(ab_re, a_shape), jnp.broadcast_to(ab_im, a_shape), bu_re, bu_im), axis=1)
    x_re = h_re + acc_re * s0_re[:, None] - acc_im * s0_im[:, None]
    x_im = h_im + acc_re * s0_im[:, None] + acc_im * s0_re[:, None]
    y = jnp.einsum('blgp,ghp->blgh', x_re, c_re) - jnp.einsum('blgp,ghp->blgh', x_im, c_im)
    return y, x_re[:, -1], x_im[:, -1]


def s5_mixer(h, lam_re, lam_im, log_step, b_re, b_im, c_re, c_im, d, w_a, b_a, w_b, b_b, s0_re, s0_im):
    f32 = jnp.float32
    Bn, L, Dm = h.shape
    u = h.astype(f32).reshape(Bn, L, S5_GROUPS, S5_GROUP_CH)
    def prm(i):
        return tuple(a[i].astype(f32) for a in (lam_re, lam_im, log_step, b_re, b_im, c_re, c_im))
    y_f, f_re, f_im = s5_direction(u, *prm(0), s0_re[:, 0].astype(f32), s0_im[:, 0].astype(f32))
    y_b, r_re, r_im = s5_direction(u[:, ::-1], *prm(1), s0_re[:, 1].astype(f32), s0_im[:, 1].astype(f32))
    y = (y_f + y_b[:, ::-1]).reshape(Bn, L, Dm) + d.astype(f32) * h.astype(f32)
    z = jax.nn.gelu(y).astype(h.dtype)
    out = (z @ w_a + b_a) * jax.nn.sigmoid(z @ w_b + b_b)
    return out, jnp.stack([f_re, r_re], axis=1), jnp.stack([f_im, r_im], axis=1)


def mlstm_chunked(q, k, v, ig, lf, C0, n0, m0):
    Bn, H, L, _ = q.shape
    nc = L // ML_CHUNK
    def chunks(a):
        return jnp.moveaxis(a.reshape((Bn, H, nc, ML_CHUNK) + a.shape[3:]), 2, 0)
    causal = jnp.tril(jnp.ones((ML_CHUNK, ML_CHUNK), dtype=bool))
    def step(carry, xs):
        C, n, m = carry
        qc, kc, vc, ic, fc = xs
        g = jnp.cumsum(fc, axis=-1)
        a = g + m[..., None]
        dmat = jnp.where(causal, g[..., :, None] - g[..., None, :] + ic[..., None, :], -jnp.inf)
        mt = jnp.maximum(a, jnp.max(dmat, axis=-1))
        s = jnp.einsum('bhtd,bhsd->bhts', qc, kc) * jnp.exp(dmat - mt[..., None])
        inter = jnp.exp(a - mt)
        num = jnp.einsum('bhts,bhsv->bhtv', s, vc) + inter[..., None] * jnp.einsum('bhtd,bhdv->bhtv', qc, C)
        den = jnp.sum(s, axis=-1) + inter * jnp.einsum('bhtd,bhd->bht', qc, n)
        hc = num / jnp.maximum(jnp.abs(den), jnp.exp(-mt))[..., None]
        g_end = g[..., -1]
        w = g_end[..., None] - g + ic
        m_new = jnp.maximum(g_end + m, jnp.max(w, axis=-1))
        decay = jnp.exp(g_end + m - m_new)
        ws = jnp.exp(w - m_new[..., None])
        C_new = decay[..., None, None] * C + jnp.einsum('bhs,bhsd,bhsv->bhdv', ws, kc, vc)
        n_new = decay[..., None] * n + jnp.einsum('bhs,bhsd->bhd', ws, kc)
        return (C_new, n_new, m_new), hc
    (C_f, n_f, m_f), hs = lax.scan(step, (C0, n0, m0), (chunks(q), chunks(k), chunks(v), chunks(ig), chunks(lf)))
    h = jnp.moveaxis(hs, 0, 2).reshape(Bn, H, L, -1)
    return h, C_f, n_f, m_f


def mlstm_mixer(h, w_in, b_gates, head_g, w_out, C0, n0, m0):
    f32 = jnp.float32
    Bn, L, Dm = h.shape
    q, k, v, o, gates = jnp.split(h @ w_in, [ML_QK, 2 * ML_QK, 2 * ML_QK + Dm, 2 * ML_QK + 2 * Dm], axis=-1)
    def heads(a):
        return a.reshape(Bn, L, ML_HEADS, -1).transpose(0, 2, 1, 3).astype(f32)
    q, k, v = heads(q), heads(k) * (ML_DK ** -0.5), heads(v)
    gates = (gates + b_gates).astype(f32).reshape(Bn, L, 4, ML_HEADS).transpose(2, 0, 3, 1)
    i_f, f_f, i_b, f_b = gates[0], gates[1], gates[2], gates[3]
    C0, n0, m0 = C0.astype(f32), n0.astype(f32), m0.astype(f32)
    h_f, C_f, n_f, m_f = mlstm_chunked(q, k, v, i_f, jax.nn.log_sigmoid(f_f), C0[:, 0], n0[:, 0], m0[:, 0])
    def rev(a):
        return jnp.flip(a, axis=2)
    h_b, C_b, n_b, m_b = mlstm_chunked(rev(q), rev(k), rev(v), rev(i_b), rev(jax.nn.log_sigmoid(f_b)),
                                       C0[:, 1], n0[:, 1], m0[:, 1])
    hsum = (h_f + rev(h_b)).transpose(0, 2, 1, 3)
    hn = hsum * lax.rsqrt(jnp.mean(hsum * hsum, axis=-1, keepdims=True) + RMS_EPS)
    hn = hn.reshape(Bn, L, Dm) * head_g.astype(f32) * jax.nn.sigmoid(o.astype(f32))
    out = hn.astype(h.dtype) @ w_out
    return out, jnp.stack([C_f, C_b], axis=1), jnp.stack([n_f, n_b], axis=1), jnp.stack([m_f, m_b], axis=1)


def hier_moe(h, w_group, b_group, w_expert, b_expert, w_gate, w_up, w_down):
    f32 = jnp.float32
    Bn, L, Dm = h.shape
    t = h.reshape(Bn * L, Dm)
    T = t.shape[0]
    t32 = t.astype(f32)
    p_group = jax.nn.softmax(t32 @ w_group.astype(f32) + b_group.astype(f32), axis=-1)
    pg, grp = lax.top_k(p_group, 1)
    logits = (t32 @ w_expert.astype(f32) + b_expert.astype(f32)).reshape(T, MOE_GROUPS, MOE_EPG)
    in_group = jnp.take_along_axis(logits, grp[:, :, None], axis=1)[:, 0]
    pv, pi = lax.top_k(jax.nn.softmax(in_group, axis=-1), MOE_TOPK)
    gate_w = (pg * pv / jnp.sum(pv, axis=-1, keepdims=True)).reshape(-1)
    eid = (grp * MOE_EPG + pi).reshape(-1)
    tok = jnp.repeat(jnp.arange(T, dtype=jnp.int32), MOE_TOPK)
    order = jnp.argsort(eid)
    se, stok, sw = eid[order], tok[order], gate_w[order]
    counts = jnp.bincount(eid, length=MOE_EXPERTS)
    starts = jnp.cumsum(counts) - counts
    pcounts = (counts + MOE_BLOCK - 1) // MOE_BLOCK * MOE_BLOCK
    pends = jnp.cumsum(pcounts)
    pstarts = pends - pcounts
    dest = pstarts[se] + jnp.arange(se.shape[0], dtype=jnp.int32) - starts[se]
    n_pad = (-(-(T * MOE_TOPK) // MOE_BLOCK) + MOE_EXPERTS) * MOE_BLOCK
    n_blocks = n_pad // MOE_BLOCK
    slot_tok = jnp.zeros((n_pad,), jnp.int32).at[dest].set(stok)
    slot_w = jnp.zeros((n_pad,), f32).at[dest].set(sw)
    block_expert = jnp.minimum(
        jnp.searchsorted(pends, jnp.arange(n_blocks, dtype=jnp.int32) * MOE_BLOCK, side='right'), MOE_EXPERTS - 1)
    xb = t[slot_tok].reshape(n_blocks, MOE_BLOCK, Dm)
    def expert_block(args):
        xe, e = args
        return (jax.nn.silu(xe @ w_gate[e]) * (xe @ w_up[e])) @ w_down[e]
    yb = lax.map(expert_block, (xb, block_expert)).reshape(n_pad, Dm)
    out = jnp.zeros_like(t).at[slot_tok].add((yb * slot_w[:, None]).astype(t.dtype))
    return out.reshape(Bn, L, Dm)


def setup_inputs(seed: int = 0) -> dict:
    key = jax.random.key(seed)
    ks = iter(jax.random.split(key, 48))
    f32 = jnp.float32
    D = D_MODEL
    def nrm(shape, scale):
        return jax.random.normal(next(ks), shape, f32) * scale
    s5_shape = (N_S5_LAYERS, 2, S5_GROUPS, S5_STATE)
    f_lin = jnp.linspace(3.0, 6.0, ML_HEADS, dtype=f32)
    inp = {}
    inp['x_prompt'] = nrm((BATCH, SEQ, D), 1.0)
    inp['x_sample'] = nrm((DEC_BATCH, DEC_SEQ, D), 1.0)
    inp['state_s5_re'] = nrm((DEC_BATCH,) + s5_shape, 0.3)
    inp['state_s5_im'] = nrm((DEC_BATCH,) + s5_shape, 0.3)
    inp['state_mlstm_C'] = nrm((DEC_BATCH, N_ML_LAYERS, 2, ML_HEADS, ML_DK, ML_DV), ML_DK ** -0.5)
    inp['state_mlstm_n'] = nrm((DEC_BATCH, N_ML_LAYERS, 2, ML_HEADS, ML_DK), ML_DK ** -0.5)
    inp['state_mlstm_m'] = nrm((DEC_BATCH, N_ML_LAYERS, 2, ML_HEADS), 1.0)
    inp['c'] = nrm((DEC_BATCH, D), 1.0)
    inp['c_ctx'] = nrm((D,), 1.0)
    inp['w_ada'] = nrm((DEPTH, D, 6 * D), 0.5 * D ** -0.5)
    inp['b_ada'] = nrm((DEPTH, 6 * D), 0.02)
    inp['norm1_g'] = 1.0 + nrm((DEPTH, D), 0.01)
    inp['norm2_g'] = 1.0 + nrm((DEPTH, D), 0.01)
    inp['final_norm_g'] = 1.0 + nrm((D,), 0.01)
    inp['s5_lambda_re'] = -0.5 + nrm(s5_shape, 0.01)
    inp['s5_lambda_im'] = math.pi * jnp.arange(S5_STATE, dtype=f32) + nrm(s5_shape, 0.01)
    inp['s5_log_step'] = jax.random.uniform(next(ks), (N_S5_LAYERS, 2, S5_GROUPS), f32, math.log(1e-3), math.log(1e-1))
    inp['s5_b_re'] = nrm(s5_shape + (S5_GROUP_CH,), (2 * S5_GROUP_CH) ** -0.5)
    inp['s5_b_im'] = nrm(s5_shape + (S5_GROUP_CH,), (2 * S5_GROUP_CH) ** -0.5)
    inp['s5_c_re'] = nrm((N_S5_LAYERS, 2, S5_GROUPS, S5_GROUP_CH, S5_STATE), S5_STATE ** -0.5)
    inp['s5_c_im'] = nrm((N_S5_LAYERS, 2, S5_GROUPS, S5_GROUP_CH, S5_STATE), S5_STATE ** -0.5)
    inp['s5_d'] = nrm((N_S5_LAYERS, D), 0.5)
    inp['s5_w_glu_a'] = nrm((N_S5_LAYERS, D, D), D ** -0.5)
    inp['s5_b_glu_a'] = nrm((N_S5_LAYERS, D), 0.02)
    inp['s5_w_glu_b'] = nrm((N_S5_LAYERS, D, D), D ** -0.5)
    inp['s5_b_glu_b'] = nrm((N_S5_LAYERS, D), 0.02)
    inp['ml_w_in'] = nrm((N_ML_LAYERS, D, ML_PROJ), D ** -0.5)
    inp['ml_b_gates'] = jnp.concatenate([
        nrm((N_ML_LAYERS, ML_HEADS), 0.1), f_lin + nrm((N_ML_LAYERS, ML_HEADS), 0.1),
        nrm((N_ML_LAYERS, ML_HEADS), 0.1), f_lin + nrm((N_ML_LAYERS, ML_HEADS), 0.1)], axis=-1)
    inp['ml_head_norm_g'] = 1.0 + nrm((N_ML_LAYERS, D), 0.01)
    inp['ml_w_out'] = nrm((N_ML_LAYERS, D, D), D ** -0.5)
    inp['moe_w_group'] = nrm((DEPTH, D, MOE_GROUPS), D ** -0.5)
    inp['moe_b_group'] = nrm((DEPTH, MOE_GROUPS), 0.01)
    inp['moe_w_expert'] = nrm((DEPTH, D, MOE_EXPERTS), D ** -0.5)
    inp['moe_b_expert'] = nrm((DEPTH, MOE_EXPERTS), 0.01)
    inp['moe_w_gate'] = nrm((DEPTH, MOE_EXPERTS, D, D_EXPERT), D ** -0.5)
    inp['moe_w_up'] = nrm((DEPTH, MOE_EXPERTS, D, D_EXPERT), D ** -0.5)
    inp['moe_w_down'] = nrm((DEPTH, MOE_EXPERTS, D_EXPERT, D), D_EXPERT ** -0.5)
    return inp


def reference(x_prompt, x_sample, state_s5_re, state_s5_im, state_mlstm_C, state_mlstm_n, state_mlstm_m,
              c, c_ctx, w_ada, b_ada, norm1_g, norm2_g, final_norm_g,
              s5_lambda_re, s5_lambda_im, s5_log_step, s5_b_re, s5_b_im, s5_c_re, s5_c_im, s5_d,
              s5_w_glu_a, s5_b_glu_a, s5_w_glu_b, s5_b_glu_b,
              ml_w_in, ml_b_gates, ml_head_norm_g, ml_w_out,
              moe_w_group, moe_b_group, moe_w_expert, moe_b_expert, moe_w_gate, moe_w_up, moe_w_down):
    f32 = jnp.float32
    b_ctx = x_prompt.shape[0]
    xp = x_prompt
    xs = x_sample + grid_pos_embed(x_sample.shape[1]).astype(x_sample.dtype)
    new_s5_re, new_s5_im, new_C, new_n, new_m = [], [], [], [], []
    for l in range(DEPTH):
        j = l // N_MIXERS
        mp = modulation(c_ctx[None, :], w_ada[l], b_ada[l])
        ms = modulation(c, w_ada[l], b_ada[l])
        hp = ada_norm(xp, norm1_g[l], mp[0], mp[1])
        hs = ada_norm(xs, norm1_g[l], ms[0], ms[1])
        if l % N_MIXERS == 0:
            prm = (s5_lambda_re[j], s5_lambda_im[j], s5_log_step[j], s5_b_re[j], s5_b_im[j], s5_c_re[j],
                   s5_c_im[j], s5_d[j], s5_w_glu_a[j], s5_b_glu_a[j], s5_w_glu_b[j], s5_b_glu_b[j])
            zero = jnp.zeros((b_ctx, 2, S5_GROUPS, S5_STATE), f32)
            op, s_re, s_im = s5_mixer(hp, *prm, zero, zero)
            osm, _, _ = s5_mixer(hs, *prm, state_s5_re[:, j], state_s5_im[:, j])
            new_s5_re.append(s_re)
            new_s5_im.append(s_im)
        else:
            prm = (ml_w_in[j], ml_b_gates[j], ml_head_norm_g[j], ml_w_out[j])
            C0 = jnp.zeros((b_ctx, 2, ML_HEADS, ML_DK, ML_DV), f32)
            n0 = jnp.zeros((b_ctx, 2, ML_HEADS, ML_DK), f32)
            m0 = jnp.full((b_ctx, 2, ML_HEADS), NEG_BIG, f32)
            op, Cc, nc, mc = mlstm_mixer(hp, *prm, C0, n0, m0)
            osm, _, _, _ = mlstm_mixer(hs, *prm, state_mlstm_C[:, j], state_mlstm_n[:, j], state_mlstm_m[:, j])
            new_C.append(Cc)
            new_n.append(nc)
            new_m.append(mc)
        xp = xp + mp[2] * op
        xs = xs + ms[2] * osm
        moe_prm = (moe_w_group[l], moe_b_group[l], moe_w_expert[l], moe_b_expert[l],
                   moe_w_gate[l], moe_w_up[l], moe_w_down[l])
        xp = xp + mp[5] * hier_moe(ada_norm(xp, norm2_g[l], mp[3], mp[4]), *moe_prm)
        xs = xs + ms[5] * hier_moe(ada_norm(xs, norm2_g[l], ms[3], ms[4]), *moe_prm)
    y_prompt = rms_norm(xp, final_norm_g)
    y_sample = rms_norm(xs, final_norm_g)
    sdt = x_prompt.dtype
    new_s5_re_arr = jnp.stack(new_s5_re, axis=1).astype(sdt)
    new_s5_im_arr = jnp.stack(new_s5_im, axis=1).astype(sdt)
    new_C_arr = jnp.stack(new_C, axis=1).astype(sdt)
    new_n_arr = jnp.stack(new_n, axis=1).astype(sdt)
    new_m_arr = jnp.stack(new_m, axis=1).astype(sdt)
    return (y_prompt, y_sample, new_s5_re_arr, new_s5_im_arr, new_C_arr, new_n_arr, new_m_arr)
```

```python
import functools
import math

import jax
import jax.numpy as jnp
from jax import lax
from jax.experimental import pallas as pl
from jax.experimental.pallas import tpu as pltpu

F32 = jnp.float32
BF16 = jnp.bfloat16

S5_GROUP_CH = 16
ML_HEADS = 8
MOE_GROUPS = 4
MOE_EPG = 8
MOE_EXPERTS = MOE_GROUPS * MOE_EPG
GRID_W = 64
POS_BASE = 10000.0
RMS_EPS = 1e-6
NEG_BIG = -1e30
N_MIXERS = 2
N_MOD = 6

LANES = 128
SUBLANES = 8
VMEM_LIMIT_BYTES = 56 * 1024 * 1024

MOD_ROWS = 16
S5_LANE_GROUPS = LANES // S5_GROUP_CH
S5_TC = 64
ML_T = 128
ROUTE_LANES = LANES
MOE_TM = 256
ROW_TM = 256


def _cparams(sem, vmem=VMEM_LIMIT_BYTES):
    return pltpu.CompilerParams(dimension_semantics=sem, vmem_limit_bytes=vmem)


def _silu(x):
    return x * jax.nn.sigmoid(x)


def _gelu_tanh(x):
    c = math.sqrt(2.0 / math.pi)
    return 0.5 * x * (1.0 + jnp.tanh(c * (x + 0.044715 * (x * x * x))))


def _ada_norm_tile(x, g, shift, scale):
    r = lax.rsqrt(jnp.mean(x * x, axis=-1, keepdims=True) + RMS_EPS)
    return (x * r * g) * (1.0 + scale) + shift


def _mod_kernel(c_ref, w_ref, b_ref, o_ref):
    s = _silu(c_ref[...])
    o_ref[...] = jnp.dot(s, w_ref[...], preferred_element_type=F32,
                         precision=lax.Precision.HIGHEST) + b_ref[...]


def _modulation(cond, w_ada, b_ada, tn=1024):
    depth, d, n = w_ada.shape
    return pl.pallas_call(
        _mod_kernel,
        out_shape=jax.ShapeDtypeStruct((depth, MOD_ROWS, n), F32),
        grid=(depth, n // tn),
        in_specs=[
            pl.BlockSpec((MOD_ROWS, d), lambda l, j: (0, 0)),
            pl.BlockSpec((None, d, tn), lambda l, j: (l, 0, j)),
            pl.BlockSpec((None, 1, tn), lambda l, j: (l, 0, j)),
        ],
        out_specs=pl.BlockSpec((None, MOD_ROWS, tn), lambda l, j: (l, 0, j)),
        compiler_params=_cparams(("parallel", "parallel")),
        name="modulation",
    )(cond, w_ada, b_ada.reshape(depth, 1, n))


class _Mods:
    def __init__(self, mods, n_ctx, dec_seq):
        self.mods = mods
        self.n_ctx = n_ctx
        self.dec_seq = dec_seq
        self.d = mods.shape[-1]

    def spec(self, layer, k, tm, row_offset=0):
        base = layer * MOD_ROWS * N_MOD + k
        n_ctx, dec_seq = self.n_ctx, self.dec_seq

        def index(i, *_):
            row0 = i * tm + row_offset
            r = jnp.where(row0 < n_ctx, 0, 1 + (row0 - n_ctx) // dec_seq)
            return (base + r * N_MOD, 0, 0)

        return pl.BlockSpec((None, 1, self.d), index)


def _embed_kernel(x_ref, p_ref, o_ref):
    o_ref[...] = x_ref[...] + p_ref[...]


def _grid_pos_embed(n_tokens, d):
    rows = n_tokens // GRID_W
    r, col = jnp.meshgrid(jnp.arange(rows, dtype=F32), jnp.arange(GRID_W, dtype=F32), indexing="ij")
    quarter = d // 4
    omega = 1.0 / (POS_BASE ** (jnp.arange(quarter, dtype=F32) / quarter))

    def axis_embed(pos):
        ang = pos.reshape(-1, 1) * omega[None, :]
        return jnp.concatenate([jnp.sin(ang), jnp.cos(ang)], axis=-1)

    return jnp.concatenate([axis_embed(r), axis_embed(col)], axis=-1)


def _embed(x_sample):
    b, l, d = x_sample.shape
    pos = _grid_pos_embed(l, d)
    return pl.pallas_call(
        _embed_kernel,
        out_shape=jax.ShapeDtypeStruct((b, l, d), F32),
        grid=(b,),
        in_specs=[pl.BlockSpec((None, l, d), lambda i: (i, 0, 0)),
                  pl.BlockSpec((l, d), lambda i: (0, 0))],
        out_specs=pl.BlockSpec((None, l, d), lambda i: (i, 0, 0)),
        compiler_params=_cparams(("parallel",)),
        name="pos_embed",
    )(x_sample, pos)


def _adanorm_kernel(x_ref, g_ref, sh_ref, sc_ref, o_ref):
    o_ref[...] = _ada_norm_tile(x_ref[...], g_ref[...], sh_ref[...], sc_ref[...]).astype(o_ref.dtype)


def _adanorm(x, g, mods, layer, k_shift, k_scale, tm=512):
    n, d = x.shape
    return pl.pallas_call(
        _adanorm_kernel,
        out_shape=jax.ShapeDtypeStruct((n, d), F32),
        grid=(n // tm,),
        in_specs=[pl.BlockSpec((tm, d), lambda i: (i, 0)),
                  pl.BlockSpec((1, d), lambda i: (0, 0)),
                  mods.spec(layer, k_shift, tm),
                  mods.spec(layer, k_scale, tm)],
        out_specs=pl.BlockSpec((tm, d), lambda i: (i, 0)),
        compiler_params=_cparams(("parallel",)),
        name="adanorm",
    )(x, g.reshape(1, d), mods.mods, mods.mods)


def _final_norm_kernel(x_ref, g_ref, o_ref):
    x = x_ref[...]
    r = lax.rsqrt(jnp.mean(x * x, axis=-1, keepdims=True) + RMS_EPS)
    o_ref[...] = x * r * g_ref[...]


def _final_norm(x, g, row0, nrows, tm=512):
    d = x.shape[1]
    off = row0 // tm
    return pl.pallas_call(
        _final_norm_kernel,
        out_shape=jax.ShapeDtypeStruct((nrows, d), F32),
        grid=(nrows // tm,),
        in_specs=[pl.BlockSpec((tm, d), lambda i: (i + off, 0)),
                  pl.BlockSpec((1, d), lambda i: (0, 0))],
        out_specs=pl.BlockSpec((tm, d), lambda i: (i, 0)),
        compiler_params=_cparams(("parallel",)),
        name="final_norm",
    )(x, g.reshape(1, d))


def _s5_disc_kernel(lre_ref, lim_ref, ls_ref, bre_ref, bim_ref, are_ref, aim_ref, bbre_ref, bbim_ref):
    lr = lre_ref[...]
    li = lim_ref[...]
    dt = jnp.exp(ls_ref[...])
    mag = jnp.exp(lr * dt)
    ar = mag * jnp.cos(li * dt)
    ai = mag * jnp.sin(li * dt)
    den = lr * lr + li * li
    zr = ((ar - 1.0) * lr + ai * li) / den
    zi = (ai * lr - (ar - 1.0) * li) / den
    are_ref[...] = ar
    aim_ref[...] = ai
    br = bre_ref[...]
    bi = bim_ref[...]
    bbre_ref[...] = zr[None] * br - zi[None] * bi
    bbim_ref[...] = zr[None] * bi + zi[None] * br


def _s5_discretize(lam_re, lam_im, log_step, b_re, b_im):
    _, g, p = lam_re.shape
    h = b_re.shape[-1]
    bt_re = jnp.transpose(b_re, (0, 3, 1, 2))
    bt_im = jnp.transpose(b_im, (0, 3, 1, 2))
    vec = pl.BlockSpec((None, g, p), lambda i: (i, 0, 0))
    mat = pl.BlockSpec((None, h, g, p), lambda i: (i, 0, 0, 0))
    return pl.pallas_call(
        _s5_disc_kernel,
        out_shape=(jax.ShapeDtypeStruct((2, g, p), F32), jax.ShapeDtypeStruct((2, g, p), F32),
                   jax.ShapeDtypeStruct((2, h, g, p), F32), jax.ShapeDtypeStruct((2, h, g, p), F32)),
        grid=(2,),
        in_specs=[vec, vec, pl.BlockSpec((None, g, 1), lambda i: (i, 0, 0)), mat, mat],
        out_specs=(vec, vec, mat, mat),
        compiler_params=_cparams(("parallel",)),
        name="s5_discretize",
    )(lam_re, lam_im, log_step.reshape(2, g, 1), bt_re, bt_im)


def _s5_operands(ab_re, ab_im, bb_re, bb_im, c_re, c_im):
    _, g, p = ab_re.shape
    h = bb_re.shape[1]
    gl = S5_LANE_GROUPS
    nlc = g // gl
    eye = jnp.eye(gl, dtype=F32)

    def bblock(bb):
        t = jnp.transpose(bb, (0, 2, 1, 3)).reshape(2, nlc, gl, h, p)
        return (t[:, :, :, :, None, :] * eye[None, None, :, None, :, None]).reshape(2, nlc, gl * h, gl * p)

    def cblock(c):
        t = c.reshape(2, nlc, gl, h, p)
        t = jnp.transpose(t, (0, 1, 2, 4, 3))
        return (t[:, :, :, :, None, :] * eye[None, None, :, None, :, None]).reshape(2, nlc, gl * p, gl * h)

    bmat = jnp.concatenate([bblock(bb_re), bblock(bb_im)], axis=-1).astype(BF16)
    cmat = jnp.concatenate([cblock(c_re), cblock(-c_im)], axis=-2).astype(BF16)
    avec = jnp.concatenate([ab_re.reshape(2, nlc, 1, gl * p), ab_im.reshape(2, nlc, 1, gl * p)], axis=-1)
    return bmat, cmat, avec


def _s5_state_to_lanes(s_re, s_im):
    b, _, g, p = s_re.shape
    gl = S5_LANE_GROUPS
    nlc = g // gl

    def lay(s):
        return jnp.transpose(s.reshape(b, 2, nlc, gl * p), (1, 2, 0, 3))

    return jnp.concatenate([lay(s_re), lay(s_im)], axis=-1)


def _s5_state_from_lanes(s, g, p):
    _, nlc, b, s2 = s.shape
    half = s2 // 2

    def unlay(t):
        return jnp.transpose(t, (2, 0, 1, 3)).reshape(b, 2, g, p)

    return unlay(s[..., :half]), unlay(s[..., half:])


def _s5_scan_kernel(*refs, seq, tc, aliased):
    if aliased:
        h_ref, bm_ref, cm_ref, a_ref, d_ref, s0_ref, _, z_ref, sf_ref, u_scr, x_scr, y_scr = refs
    else:
        h_ref, bm_ref, cm_ref, a_ref, d_ref, s0_ref, z_ref, sf_ref, u_scr, x_scr, y_scr = refs
    nb = SUBLANES
    half = a_ref.shape[-1] // 2
    nc = seq // tc
    y_scr[...] = h_ref[...] * d_ref[...]
    a_f = a_ref[0]
    a_b = a_ref[1]
    ar_f = jnp.broadcast_to(a_f[:, :half], (nb, half))
    ai_f = jnp.broadcast_to(a_f[:, half:], (nb, half))
    ar_b = jnp.broadcast_to(a_b[:, :half], (nb, half))
    ai_b = jnp.broadcast_to(a_b[:, half:], (nb, half))

    def chunk(c, carry):
        t0s = (c * tc, (nc - 1 - c) * tc)
        for dr in range(2):
            for t in range(tc):
                u_scr[dr, t * nb:(t + 1) * nb, :] = h_ref[pl.ds(t0s[dr] + t, nb, stride=seq), :]
            x_scr[dr] = jnp.dot(u_scr[dr].astype(BF16), bm_ref[dr], preferred_element_type=F32)

        def step(t, st):
            xr_f, xi_f, xr_b, xi_b = st
            rf = pl.multiple_of(t * nb, nb)
            rb = pl.multiple_of((tc - 1 - t) * nb, nb)
            bu_f = x_scr[0, pl.ds(rf, nb), :]
            bu_b = x_scr[1, pl.ds(rb, nb), :]
            nr_f = ar_f * xr_f - ai_f * xi_f + bu_f[:, :half]
            ni_f = ar_f * xi_f + ai_f * xr_f + bu_f[:, half:]
            nr_b = ar_b * xr_b - ai_b * xi_b + bu_b[:, :half]
            ni_b = ar_b * xi_b + ai_b * xr_b + bu_b[:, half:]
            x_scr[0, pl.ds(rf, nb), :half] = nr_f
            x_scr[0, pl.ds(rf, nb), half:] = ni_f
            x_scr[1, pl.ds(rb, nb), :half] = nr_b
            x_scr[1, pl.ds(rb, nb), half:] = ni_b
            return nr_f, ni_f, nr_b, ni_b

        carry = lax.fori_loop(0, tc, step, carry, unroll=4)
        for dr in range(2):
            u_scr[dr] = jnp.dot(x_scr[dr].astype(BF16), cm_ref[dr], preferred_element_type=F32)
            for t in range(tc):
                rows = pl.ds(t0s[dr] + t, nb, stride=seq)
                y_scr[rows, :] = y_scr[rows, :] + u_scr[dr, t * nb:(t + 1) * nb, :]
        return carry

    s0f = s0_ref[0]
    s0b = s0_ref[1]
    init = (s0f[:, :half], s0f[:, half:], s0b[:, :half], s0b[:, half:])
    xr_f, xi_f, xr_b, xi_b = lax.fori_loop(0, nc, chunk, init)
    sf_ref[0, :, :half] = xr_f
    sf_ref[0, :, half:] = xi_f
    sf_ref[1, :, :half] = xr_b
    sf_ref[1, :, half:] = xi_b
    z_ref[...] = _gelu_tanh(y_scr[...]).astype(z_ref.dtype)


def _s5_scan(h, bmat, cmat, avec, dvec, s0, row0, nseq, seq, z_prev=None):
    n, d = h.shape
    nlc = d // LANES
    s2 = avec.shape[-1]
    nb = SUBLANES
    rows = nb * seq
    off = row0 // rows
    aliased = z_prev is not None
    in_specs = [
        pl.BlockSpec((rows, LANES), lambda i, j: (i + off, j)),
        pl.BlockSpec((2, None, LANES, s2), lambda i, j: (0, j, 0, 0)),
        pl.BlockSpec((2, None, s2, LANES), lambda i, j: (0, j, 0, 0)),
        pl.BlockSpec((2, None, 1, s2), lambda i, j: (0, j, 0, 0)),
        pl.BlockSpec((1, LANES), lambda i, j: (0, j)),
        pl.BlockSpec((2, None, nb, s2), lambda i, j: (0, j, i, 0)),
    ]
    args = [h, bmat, cmat, avec, dvec, s0]
    io_alias = {}
    if aliased:
        in_specs.append(pl.BlockSpec(memory_space=pl.ANY))
        args.append(z_prev)
        io_alias = {6: 0}
    return pl.pallas_call(
        functools.partial(_s5_scan_kernel, seq=seq, tc=S5_TC, aliased=aliased),
        out_shape=(jax.ShapeDtypeStruct((n, d), BF16), jax.ShapeDtypeStruct((2, nlc, nseq, s2), F32)),
        grid=(nseq // nb, nlc),
        in_specs=in_specs,
        out_specs=(pl.BlockSpec((rows, LANES), lambda i, j: (i + off, j)),
                   pl.BlockSpec((2, None, nb, s2), lambda i, j: (0, j, i, 0))),
        scratch_shapes=[pltpu.VMEM((2, S5_TC * nb, LANES), F32),
                        pltpu.VMEM((2, S5_TC * nb, s2), F32),
                        pltpu.VMEM((rows, LANES), F32)],
        input_output_aliases=io_alias,
        compiler_params=_cparams(("parallel", "parallel")),
        name="s5_scan",
    )(*args)


def _mm_res_kernel(*refs, n_w, has_bias):
    z_ref = refs[0]
    w_refs = refs[1:1 + n_w]
    pos = 1 + n_w
    b_refs = refs[pos:pos + n_w] if has_bias else ()
    pos += n_w if has_bias else 0
    x_ref, gate_ref, o_ref = refs[pos:pos + 3]
    z = z_ref[...]
    acc = [jnp.dot(z, w[...], preferred_element_type=F32) for w in w_refs]
    if has_bias:
        acc = [a + b[...] for a, b in zip(acc, b_refs)]
    y = acc[0] if n_w == 1 else acc[0] * jax.nn.sigmoid(acc[1])
    o_ref[...] = x_ref[...] + gate_ref[...] * y


def _mm_residual(z, ws, bs, x, mods, layer, k_gate, tm=1024, tn=512):
    n, k = z.shape
    n_out = ws[0].shape[1]
    n_w = len(ws)
    has_bias = bs is not None
    in_specs = [pl.BlockSpec((tm, k), lambda i, j: (i, 0))]
    in_specs += [pl.BlockSpec((k, tn), lambda i, j: (0, j)) for _ in ws]
    args = [z, *ws]
    if has_bias:
        in_specs += [pl.BlockSpec((1, tn), lambda i, j: (0, j)) for _ in bs]
        args += [b.reshape(1, n_out) for b in bs]
    gate_spec = mods.spec(layer, k_gate, tm)
    gate_spec = pl.BlockSpec((None, 1, tn), lambda i, j, f=gate_spec.index_map: (f(i)[0], 0, j))
    in_specs += [pl.BlockSpec((tm, tn), lambda i, j: (i, j)), gate_spec]
    args += [x, mods.mods]
    return pl.pallas_call(
        functools.partial(_mm_res_kernel, n_w=n_w, has_bias=has_bias),
        out_shape=jax.ShapeDtypeStruct((n, n_out), F32),
        grid=(n // tm, n_out // tn),
        in_specs=in_specs,
        out_specs=pl.BlockSpec((tm, tn), lambda i, j: (i, j)),
        compiler_params=_cparams(("parallel", "parallel")),
        name=f"proj_residual_{n_w}w",
    )(*args)


def _mm_norm_kernel(x_ref, g_ref, sh_ref, sc_ref, w_ref, o_ref, h_scr, *, precision):
    @pl.when(pl.program_id(1) == 0)
    def _():
        h_scr[...] = _ada_norm_tile(x_ref[...], g_ref[...], sh_ref[...], sc_ref[...]).astype(h_scr.dtype)

    o_ref[...] = jnp.dot(h_scr[...], w_ref[...], preferred_element_type=F32,
                         precision=precision).astype(o_ref.dtype)


def _mm_adanorm(x, g, mods, layer, k_shift, k_scale, w, out_dtype, tm=1024, tn=1024):
    n, d = x.shape
    n_out = w.shape[1]
    tn = min(tn, n_out)
    exact = w.dtype == F32
    return pl.pallas_call(
        functools.partial(_mm_norm_kernel, precision=lax.Precision.HIGHEST if exact else None),
        out_shape=jax.ShapeDtypeStruct((n, n_out), out_dtype),
        grid=(n // tm, n_out // tn),
        in_specs=[pl.BlockSpec((tm, d), lambda i, j: (i, 0)),
                  pl.BlockSpec((1, d), lambda i, j: (0, 0)),
                  mods.spec(layer, k_shift, tm),
                  mods.spec(layer, k_scale, tm),
                  pl.BlockSpec((d, tn), lambda i, j: (0, j))],
        out_specs=pl.BlockSpec((tm, tn), lambda i, j: (i, j)),
        scratch_shapes=[pltpu.VMEM((tm, d), F32 if exact else BF16)],
        compiler_params=_cparams(("parallel", "arbitrary")),
        name="adanorm_proj",
    )(x, g.reshape(1, d), mods.mods, mods.mods, w)


def _mlstm_kernel(q_ref, k_ref, v_ref, o_ref, gcol_ref, grow_ref, bcol_ref, brow_ref, hg_ref,
                  c0_ref, n0_ref, m0_ref, hn_ref, cf_ref, nf_ref, mf_ref, hacc, c_scr, *, seq, tch):
    nc = seq // tch
    dk = q_ref.shape[-1]
    tt = lax.broadcasted_iota(jnp.int32, (tch, tch), 0)
    ss = lax.broadcasted_iota(jnp.int32, (tch, tch), 1)
    hacc[...] = jnp.zeros_like(hacc)
    c_scr[...] = c0_ref[...]

    def chunk_dir(dr, r0, n_row, m):
        rows = pl.ds(pl.multiple_of(r0, tch), tch)
        q = q_ref[rows, :]
        k = k_ref[rows, :]
        v = v_ref[rows, :]
        gc = gcol_ref[rows, :] + bcol_ref[...]
        gr = grow_ref[:, rows] + brow_ref[...]
        ic_col = gc[:, 2 * dr:2 * dr + 1]
        fc_col = jax.nn.log_sigmoid(gc[:, 2 * dr + 1:2 * dr + 2])
        ic_row = gr[2 * dr:2 * dr + 1, :]
        fc_row = jax.nn.log_sigmoid(gr[2 * dr + 1:2 * dr + 2, :])
        mask = (ss <= tt) if dr == 0 else (ss >= tt)
        g_col = jnp.sum(jnp.where(mask, fc_row, 0.0), axis=1, keepdims=True)
        g_row = jnp.sum(jnp.where(mask, 0.0, fc_col) + jnp.where(ss == tt, fc_col, 0.0),
                        axis=0, keepdims=True)
        g_end = jnp.sum(fc_row, axis=1, keepdims=True)
        a_col = g_col + m
        dmat = jnp.where(mask, g_col - g_row + ic_row, -jnp.inf)
        mt = jnp.maximum(a_col, jnp.max(dmat, axis=1, keepdims=True))
        qk = lax.dot_general(q, k, (((1,), (1,)), ((), ())), preferred_element_type=F32)
        s = qk * jnp.exp(dmat - mt)
        inter = jnp.exp(a_col - mt)
        c_bf = c_scr[dr].astype(BF16)
        num = jnp.dot(s.astype(BF16), v, preferred_element_type=F32) \
            + inter * jnp.dot(q, c_bf, preferred_element_type=F32)
        qf = q.astype(F32)
        den = jnp.sum(s, axis=1, keepdims=True) + inter * jnp.sum(qf * n_row, axis=1, keepdims=True)
        hc = num / jnp.maximum(jnp.abs(den), jnp.exp(-mt))
        hacc[rows, :] = hacc[rows, :] + hc
        w_col = g_end - g_col + ic_col
        m_new = jnp.maximum(g_end + m, jnp.max(w_col, axis=0, keepdims=True))
        decay = jnp.exp(g_end + m - m_new)
        ws = jnp.exp(w_col - m_new)
        kw = ws * k.astype(F32)
        c_scr[dr] = decay * c_scr[dr] + lax.dot_general(
            kw.astype(BF16), v, (((0,), (0,)), ((), ())), preferred_element_type=F32)
        n_new = decay * n_row + jnp.sum(kw, axis=0, keepdims=True)
        return n_new, m_new

    def body(c, carry):
        n_f, m_f, n_b, m_b = carry
        n_f, m_f = chunk_dir(0, c * tch, n_f, m_f)
        n_b, m_b = chunk_dir(1, (nc - 1 - c) * tch, n_b, m_b)
        return n_f, m_f, n_b, m_b

    init = (n0_ref[0:1, :], m0_ref[0:1, 0:1], n0_ref[1:2, :], m0_ref[1:2, 0:1])
    n_f, m_f, n_b, m_b = lax.fori_loop(0, nc, body, init)
    cf_ref[...] = c_scr[...]
    nf_ref[0:1, :] = n_f
    nf_ref[1:2, :] = n_b
    mf_ref[0:1, :] = jnp.broadcast_to(m_f, (1, dk))
    mf_ref[1:2, :] = jnp.broadcast_to(m_b, (1, dk))
    hs = hacc[...]
    hn = hs * lax.rsqrt(jnp.mean(hs * hs, axis=-1, keepdims=True) + RMS_EPS)
    hn = hn * hg_ref[...] * jax.nn.sigmoid(o_ref[...].astype(F32))
    hn_ref[...] = hn.astype(hn_ref.dtype)


def _mlstm(proj, gcol, grow, bias, head_g, c0, n0, m0, row0, nseq, seq, hn_prev=None):
    n = proj.shape[0]
    h = ML_HEADS
    dk = n0.shape[-1]
    dv = c0.shape[-1]
    d = h * dv
    qk = h * dk
    off = row0 // seq
    aliased = hn_prev is not None
    in_specs = [
        pl.BlockSpec((seq, dk), lambda b, j: (b + off, j)),
        pl.BlockSpec((seq, dk), lambda b, j: (b + off, qk // dk + j)),
        pl.BlockSpec((seq, dv), lambda b, j: (b + off, 2 * qk // dv + j)),
        pl.BlockSpec((seq, dv), lambda b, j: (b + off, (2 * qk + d) // dv + j)),
        pl.BlockSpec((None, None, seq, 4), lambda b, j: (b, j, 0, 0)),
        pl.BlockSpec((None, None, 4, seq), lambda b, j: (b, j, 0, 0)),
        pl.BlockSpec((None, 1, 4), lambda b, j: (j, 0, 0)),
        pl.BlockSpec((None, 4, 1), lambda b, j: (j, 0, 0)),
        pl.BlockSpec((1, dv), lambda b, j: (0, j)),
        pl.BlockSpec((None, 2, None, dk, dv), lambda b, j: (b, 0, j, 0, 0)),
        pl.BlockSpec((None, None, 2, dk), lambda b, j: (b, j, 0, 0)),
        pl.BlockSpec((None, None, 2, dk), lambda b, j: (b, j, 0, 0)),
    ]
    args = [proj, proj, proj, proj, gcol, grow, bias, jnp.transpose(bias, (0, 2, 1)),
            head_g.reshape(1, d), c0, n0, m0]
    io_alias = {}
    if aliased:
        in_specs.append(pl.BlockSpec(memory_space=pl.ANY))
        args.append(hn_prev)
        io_alias = {len(args) - 1: 0}

    def kern(*refs):
        if aliased:
            refs = refs[:12] + refs[13:]
        _mlstm_kernel(*refs, seq=seq, tch=min(ML_T, seq))

    return pl.pallas_call(
        kern,
        out_shape=(jax.ShapeDtypeStruct((n, d), BF16),
                   jax.ShapeDtypeStruct((nseq, 2, h, dk, dv), F32),
                   jax.ShapeDtypeStruct((nseq, h, 2, dk), F32),
                   jax.ShapeDtypeStruct((nseq, h, 2, dk), F32)),
        grid=(nseq, h),
        in_specs=in_specs,
        out_specs=(pl.BlockSpec((seq, dv), lambda b, j: (b + off, j)),
                   pl.BlockSpec((None, 2, None, dk, dv), lambda b, j: (b, 0, j, 0, 0)),
                   pl.BlockSpec((None, None, 2, dk), lambda b, j: (b, j, 0, 0)),
                   pl.BlockSpec((None, None, 2, dk), lambda b, j: (b, j, 0, 0))),
        scratch_shapes=[pltpu.VMEM((seq, dv), F32), pltpu.VMEM((2, dk, dv), F32)],
        input_output_aliases=io_alias,
        compiler_params=_cparams(("parallel", "parallel")),
        name="mlstm",
    )(*args)


_R_EID, _R_W, _R_RANK = 0, 2, 4
_R_LOGIT0 = MOE_GROUPS


def _router_kernel(x_ref, g_ref, sh_ref, sc_ref, w_ref, b_ref, h_ref, route_ref, cnt_ref, carry):
    i = pl.program_id(0)

    @pl.when(i == 0)
    def _():
        carry[...] = jnp.zeros_like(carry)

    h = _ada_norm_tile(x_ref[...], g_ref[...], sh_ref[...], sc_ref[...])
    h_ref[...] = h
    tm = h.shape[0]
    logits = jnp.dot(h, w_ref[...], preferred_element_type=F32,
                     precision=lax.Precision.HIGHEST) + b_ref[...]
    lane = lax.broadcasted_iota(jnp.int32, logits.shape, 1)
    big = jnp.int32(ROUTE_LANES)

    def first_lane(cond):
        return jnp.min(jnp.where(cond, lane, big), axis=1, keepdims=True)

    glog = jnp.where(lane < MOE_GROUPS, logits, -jnp.inf)
    ge = jnp.exp(glog - jnp.max(glog, axis=1, keepdims=True))
    pgrp = ge / jnp.sum(ge, axis=1, keepdims=True)
    pg = jnp.max(pgrp, axis=1, keepdims=True)
    grp = first_lane(pgrp == pg)
    e_lane = lane - _R_LOGIT0
    in_grp = (e_lane >= 0) & (e_lane < MOE_EXPERTS) & ((e_lane // MOE_EPG) == grp)
    elog = jnp.where(in_grp, logits, -jnp.inf)
    ee = jnp.exp(elog - jnp.max(elog, axis=1, keepdims=True))
    pe = jnp.where(in_grp, ee / jnp.sum(ee, axis=1, keepdims=True), -1.0)
    p0 = jnp.max(pe, axis=1, keepdims=True)
    l0 = first_lane(pe == p0)
    pe1 = jnp.where(lane == l0, -1.0, pe)
    p1 = jnp.max(pe1, axis=1, keepdims=True)
    l1 = first_lane(pe1 == p1)
    psum = p0 + p1
    w0 = pg * p0 / psum
    w1 = pg * p1 / psum
    onehot = ((lane == l0) | (lane == l1)).astype(BF16)
    r_i = lax.broadcasted_iota(jnp.int32, (tm, tm), 0)
    c_i = lax.broadcasted_iota(jnp.int32, (tm, tm), 1)
    tri = (c_i < r_i).astype(BF16)
    before = jnp.dot(tri, onehot, preferred_element_type=F32) + carry[...]
    rank0 = jnp.sum(jnp.where(lane == l0, before, 0.0), axis=1, keepdims=True)
    rank1 = jnp.sum(jnp.where(lane == l1, before, 0.0), axis=1, keepdims=True)
    carry[...] = carry[...] + jnp.sum(onehot.astype(F32), axis=0, keepdims=True)
    cnt_ref[...] = carry[...]
    cols = [(l0 - _R_LOGIT0).astype(F32), (l1 - _R_LOGIT0).astype(F32), w0, w1, rank0, rank1]
    route = jnp.zeros(logits.shape, F32)
    for c, val in enumerate(cols):
        route = jnp.where(lane == c, val, route)
    route_ref[...] = route


def _router(x, g, mods, layer, w_route, b_route, tm=512):
    n, d = x.shape
    return pl.pallas_call(
        _router_kernel,
        out_shape=(jax.ShapeDtypeStruct((n, d), F32),
                   jax.ShapeDtypeStruct((n, ROUTE_LANES), F32),
                   jax.ShapeDtypeStruct((1, ROUTE_LANES), F32)),
        grid=(n // tm,),
        in_specs=[pl.BlockSpec((tm, d), lambda i: (i, 0)),
                  pl.BlockSpec((1, d), lambda i: (0, 0)),
                  mods.spec(layer, 3, tm),
                  mods.spec(layer, 4, tm),
                  pl.BlockSpec((d, ROUTE_LANES), lambda i: (0, 0)),
                  pl.BlockSpec((1, ROUTE_LANES), lambda i: (0, 0))],
        out_specs=(pl.BlockSpec((tm, d), lambda i: (i, 0)),
                   pl.BlockSpec((tm, ROUTE_LANES), lambda i: (i, 0)),
                   pl.BlockSpec((1, ROUTE_LANES), lambda i: (0, 0))),
        scratch_shapes=[pltpu.VMEM((1, ROUTE_LANES), F32)],
        compiler_params=_cparams(("arbitrary",)),
        name="moe_router",
    )(x, g.reshape(1, d), mods.mods, mods.mods, w_route, b_route)


def _dispatch_kernel(dest_ref, h_ref, xs_in, xs_ref, sem):
    del xs_in
    tm = h_ref.shape[0]

    def issue(r, _):
        for k in range(2):
            d = dest_ref[0, 0, 2 * r + k]
            pltpu.make_async_copy(h_ref.at[pl.ds(r, 1)], xs_ref.at[pl.ds(d, 1)], sem).start()
        return 0

    lax.fori_loop(0, tm, issue, 0)
    for _ in range(2):
        pltpu.make_async_copy(h_ref, xs_ref.at[pl.ds(0, tm)], sem).wait()


def _dispatch(h, dest, n_pad, tm=ROW_TM):
    n, d = h.shape
    nblk = n // tm
    zeros = jnp.zeros((n_pad, d), F32)
    return pl.pallas_call(
        _dispatch_kernel,
        out_shape=jax.ShapeDtypeStruct((n_pad, d), F32),
        grid=(nblk,),
        in_specs=[pl.BlockSpec((1, 1, 2 * tm), lambda i: (i, 0, 0), memory_space=pltpu.SMEM),
                  pl.BlockSpec((tm, d), lambda i: (i, 0)),
                  pl.BlockSpec(memory_space=pl.ANY)],
        out_specs=pl.BlockSpec(memory_space=pl.ANY),
        scratch_shapes=[pltpu.SemaphoreType.DMA(())],
        input_output_aliases={2: 0},
        compiler_params=_cparams(("arbitrary",)),
        name="moe_dispatch",
    )(dest.reshape(nblk, 1, 2 * tm), h, zeros)


def _experts_kernel(be_ref, nu_ref, x_ref, wg_ref, wu_ref, wd_ref, o_ref):
    i = pl.program_id(0)

    @pl.when(i < nu_ref[0])
    def _():
        x = x_ref[...].astype(BF16)
        g = jnp.dot(x, wg_ref[...], preferred_element_type=F32)
        u = jnp.dot(x, wu_ref[...], preferred_element_type=F32)
        a = (_silu(g) * u).astype(BF16)
        o_ref[...] = jnp.dot(a, wd_ref[...], preferred_element_type=F32)

    @pl.when(i >= nu_ref[0])
    def _():
        o_ref[...] = jnp.zeros_like(o_ref)


def _experts(xs, block_expert, n_used, w_gate, w_up, w_down, tm=MOE_TM):
    n_pad, d = xs.shape
    f = w_gate.shape[-1]
    nblk = n_pad // tm

    def x_index(i, be, nu):
        return (jnp.minimum(i, nu[0] - 1), 0)

    grid_spec = pltpu.PrefetchScalarGridSpec(
        num_scalar_prefetch=2,
        grid=(nblk,),
        in_specs=[pl.BlockSpec((tm, d), x_index),
                  pl.BlockSpec((None, d, f), lambda i, be, nu: (be[i], 0, 0)),
                  pl.BlockSpec((None, d, f), lambda i, be, nu: (be[i], 0, 0)),
                  pl.BlockSpec((None, f, d), lambda i, be, nu: (be[i], 0, 0))],
        out_specs=pl.BlockSpec((tm, d), lambda i, be, nu: (i, 0)),
    )
    return pl.pallas_call(
        _experts_kernel,
        out_shape=jax.ShapeDtypeStruct((n_pad, d), F32),
        grid_spec=grid_spec,
        compiler_params=_cparams(("arbitrary",)),
        name="moe_experts",
    )(block_expert, n_used, xs, w_gate, w_up, w_down)


def _combine_kernel(dest_ref, y_hbm, x_ref, route_ref, gate_ref, o_ref, buf, sem):
    tm = x_ref.shape[0]

    def issue(r, _):
        for k in range(2):
            d = dest_ref[0, 0, 2 * r + k]
            pltpu.make_async_copy(y_hbm.at[pl.ds(d, 1)], buf.at[k, pl.ds(r, 1)], sem).start()
        return 0

    lax.fori_loop(0, tm, issue, 0)
    for k in range(2):
        pltpu.make_async_copy(y_hbm.at[pl.ds(0, tm)], buf.at[k], sem).wait()
    route = route_ref[...]
    w0 = route[:, _R_W:_R_W + 1]
    w1 = route[:, _R_W + 1:_R_W + 2]
    o_ref[...] = x_ref[...] + gate_ref[...] * (buf[0] * w0 + buf[1] * w1)


def _combine(x, y, dest, route, mods, layer, tm=ROW_TM):
    n, d = x.shape
    nblk = n // tm
    return pl.pallas_call(
        _combine_kernel,
        out_shape=jax.ShapeDtypeStruct((n, d), F32),
        grid=(nblk,),
        in_specs=[pl.BlockSpec((1, 1, 2 * tm), lambda i: (i, 0, 0), memory_space=pltpu.SMEM),
                  pl.BlockSpec(memory_space=pl.ANY),
                  pl.BlockSpec((tm, d), lambda i: (i, 0)),
                  pl.BlockSpec((tm, ROUTE_LANES), lambda i: (i, 0)),
                  mods.spec(layer, 5, tm)],
        out_specs=pl.BlockSpec((tm, d), lambda i: (i, 0)),
        scratch_shapes=[pltpu.VMEM((2, tm, d), F32), pltpu.SemaphoreType.DMA(())],
        compiler_params=_cparams(("arbitrary",)),
        name="moe_combine",
    )(dest.reshape(nblk, 1, 2 * tm), y, x, route, mods.mods)


def _moe_layer(x, mods, layer, norm_g, w_group, b_group, w_expert, b_expert, w_gate, w_up, w_down):
    n, d = x.shape
    pad = ROUTE_LANES - MOE_GROUPS - MOE_EXPERTS
    w_route = jnp.concatenate([w_group, w_expert, jnp.zeros((d, pad), F32)], axis=1)
    b_route = jnp.concatenate([b_group, b_expert, jnp.zeros((pad,), F32)]).reshape(1, ROUTE_LANES)
    h, route, counts = _router(x, norm_g, mods, layer, w_route, b_route)
    cnt = counts[0, _R_LOGIT0:_R_LOGIT0 + MOE_EXPERTS].astype(jnp.int32)
    pcnt = (cnt + MOE_TM - 1) // MOE_TM * MOE_TM
    pends = jnp.cumsum(pcnt)
    pstarts = pends - pcnt
    eid = route[:, _R_EID:_R_EID + 2].astype(jnp.int32)
    rank = route[:, _R_RANK:_R_RANK + 2].astype(jnp.int32)
    dest = (pstarts[eid] + rank).reshape(-1)
    n_blocks = (n * 2) // MOE_TM + MOE_EXPERTS
    n_pad = n_blocks * MOE_TM
    block_expert = jnp.minimum(
        jnp.searchsorted(pends, jnp.arange(n_blocks, dtype=jnp.int32) * MOE_TM, side="right"),
        MOE_EXPERTS - 1).astype(jnp.int32)
    n_used = (pends[-1:] // MOE_TM).astype(jnp.int32)
    xs = _dispatch(h, dest, n_pad)
    y = _experts(xs, block_expert, n_used, w_gate.astype(BF16), w_up.astype(BF16), w_down.astype(BF16))
    return _combine(x, y, dest, route, mods, layer)


class _Streams:
    def __init__(self, batch, seq, dec_batch, dec_seq):
        self.ctx = (0, batch, seq)
        self.lat = (batch * seq, dec_batch, dec_seq)


def _s5_layer(x, mods, layer, streams, norm_g, lam_re, lam_im, log_step, b_re, b_im, c_re, c_im, d_skip,
              w_a, b_a, w_b, b_b, s0_re, s0_im):
    n, d = x.shape
    g, p = lam_re.shape[1:]
    ab_re, ab_im, bb_re, bb_im = _s5_discretize(lam_re, lam_im, log_step, b_re, b_im)
    bmat, cmat, avec = _s5_operands(ab_re, ab_im, bb_re, bb_im, c_re, c_im)
    h = _adanorm(x, norm_g, mods, layer, 0, 1)
    dvec = d_skip.reshape(1, d)
    row0, nseq, seq = streams.ctx
    zero = jnp.zeros((2, d // LANES, nseq, avec.shape[-1]), F32)
    z, sf = _s5_scan(h, bmat, cmat, avec, dvec, zero, row0, nseq, seq)
    row0, nseq, seq = streams.lat
    z, _ = _s5_scan(h, bmat, cmat, avec, dvec, _s5_state_to_lanes(s0_re, s0_im), row0, nseq, seq, z_prev=z)
    x = _mm_residual(z, [w_a.astype(BF16), w_b.astype(BF16)], [b_a, b_b], x, mods, layer, 2)
    new_re, new_im = _s5_state_from_lanes(sf, g, p)
    return x, new_re, new_im


def _mlstm_layer(x, mods, layer, streams, norm_g, w_in, b_gates, head_g, w_out, c0, n0, m0):
    n, d = x.shape
    hh = ML_HEADS
    dv = d // hh
    dk = dv // 2
    qk = hh * dk
    n_main = 2 * qk + 2 * d
    col_scale = jnp.concatenate([jnp.ones((qk,), F32), jnp.full((qk,), dk ** -0.5, F32),
                                 jnp.ones((2 * d,), F32)])
    w_main = (w_in[:, :n_main] * col_scale).astype(BF16)
    w_gates = jnp.pad(w_in[:, n_main:], ((0, 0), (0, LANES - 4 * hh)))
    proj = _mm_adanorm(x, norm_g, mods, layer, 0, 1, w_main, BF16)
    gates = _mm_adanorm(x, norm_g, mods, layer, 0, 1, w_gates, F32)
    bias = jnp.transpose(b_gates.reshape(4, hh)).reshape(hh, 1, 4)

    def gate_views(row0, nseq, seq):
        gt = gates[row0:row0 + nseq * seq, :4 * hh].reshape(nseq, seq, 4, hh)
        return jnp.transpose(gt, (0, 3, 1, 2)), jnp.transpose(gt, (0, 3, 2, 1))

    row0, nseq, seq = streams.ctx
    gcol, grow = gate_views(row0, nseq, seq)
    hn, c_f, n_f, m_f = _mlstm(
        proj, gcol, grow, bias, head_g,
        jnp.zeros((nseq, 2, hh, dk, dv), F32), jnp.zeros((nseq, hh, 2, dk), F32),
        jnp.full((nseq, hh, 2, dk), NEG_BIG, F32), row0, nseq, seq)
    row0, nseq, seq = streams.lat
    gcol, grow = gate_views(row0, nseq, seq)
    hn, _, _, _ = _mlstm(
        proj, gcol, grow, bias, head_g, c0, jnp.transpose(n0, (0, 2, 1, 3)),
        jnp.broadcast_to(jnp.transpose(m0, (0, 2, 1))[..., None], (nseq, hh, 2, dk)),
        row0, nseq, seq, hn_prev=hn)
    x = _mm_residual(hn, [w_out.astype(BF16)], None, x, mods, layer, 2)
    return x, c_f, jnp.transpose(n_f, (0, 2, 1, 3)), jnp.transpose(m_f[..., 0], (0, 2, 1))


def kernel(x_prompt, x_sample, state_s5_re, state_s5_im, state_mlstm_C, state_mlstm_n, state_mlstm_m, c, c_ctx, w_ada, b_ada, norm1_g, norm2_g, final_norm_g, s5_lambda_re, s5_lambda_im, s5_log_step, s5_b_re, s5_b_im, s5_c_re, s5_c_im, s5_d, s5_w_glu_a, s5_b_glu_a, s5_w_glu_b, s5_b_glu_b, ml_w_in, ml_b_gates, ml_head_norm_g, ml_w_out, moe_w_group, moe_b_group, moe_w_expert, moe_b_expert, moe_w_gate, moe_w_up, moe_w_down):
    batch, seq, d = x_prompt.shape
    dec_batch, dec_seq, _ = x_sample.shape
    depth = w_ada.shape[0]
    n_ctx = batch * seq
    streams = _Streams(batch, seq, dec_batch, dec_seq)

    cond = jnp.concatenate([c_ctx[None, :], c, jnp.zeros((MOD_ROWS - 1 - dec_batch, d), F32)], axis=0)
    mods = _modulation(cond, w_ada, b_ada).reshape(depth * MOD_ROWS * N_MOD, 1, d)
    mods = _Mods(mods, n_ctx, dec_seq)

    x = jnp.concatenate([x_prompt.reshape(n_ctx, d), _embed(x_sample).reshape(dec_batch * dec_seq, d)], axis=0)
    new_s5_re, new_s5_im, new_c, new_n, new_m = [], [], [], [], []
    for l in range(depth):
        j = l // N_MIXERS
        if l % N_MIXERS == 0:
            x, s_re, s_im = _s5_layer(
                x, mods, l, streams, norm1_g[l], s5_lambda_re[j], s5_lambda_im[j], s5_log_step[j],
                s5_b_re[j], s5_b_im[j], s5_c_re[j], s5_c_im[j], s5_d[j],
                s5_w_glu_a[j], s5_b_glu_a[j], s5_w_glu_b[j], s5_b_glu_b[j],
                state_s5_re[:, j], state_s5_im[:, j])
            new_s5_re.append(s_re)
            new_s5_im.append(s_im)
        else:
            x, c_f, n_f, m_f = _mlstm_layer(
                x, mods, l, streams, norm1_g[l], ml_w_in[j], ml_b_gates[j], ml_head_norm_g[j], ml_w_out[j],
                state_mlstm_C[:, j], state_mlstm_n[:, j], state_mlstm_m[:, j])
            new_c.append(c_f)
            new_n.append(n_f)
            new_m.append(m_f)
        x = _moe_layer(x, mods, l, norm2_g[l], moe_w_group[l], moe_b_group[l], moe_w_expert[l],
                       moe_b_expert[l], moe_w_gate[l], moe_w_up[l], moe_w_down[l])
    y_prompt = _final_norm(x, final_norm_g, 0, n_ctx).reshape(batch, seq, d)
    y_sample = _final_norm(x, final_norm_g, n_ctx, dec_batch * dec_seq).reshape(dec_batch, dec_seq, d)
    return (y_prompt, y_sample, jnp.stack(new_s5_re, axis=1), jnp.stack(new_s5_im, axis=1),
            jnp.stack(new_c, axis=1), jnp.stack(new_n, axis=1), jnp.stack(new_m, axis=1))
```

```python
import functools
import math

import jax
import jax.numpy as jnp
from jax import lax
from jax.experimental import pallas as pl
from jax.experimental.pallas import tpu as pltpu

F32 = jnp.float32
BF16 = jnp.bfloat16

S5_GROUP_CH = 16
ML_HEADS = 8
MOE_GROUPS = 4
MOE_EPG = 8
MOE_EXPERTS = MOE_GROUPS * MOE_EPG
GRID_W = 64
POS_BASE = 10000.0
RMS_EPS = 1e-6
NEG_BIG = -1e30
N_MIXERS = 2
N_MOD = 6

LANES = 128
SUBLANES = 8
VMEM_LIMIT_BYTES = 56 * 1024 * 1024
EXPERT_VMEM_LIMIT_BYTES = 60 * 1024 * 1024

MOD_ROWS = 16
S5_LANE_GROUPS = LANES // S5_GROUP_CH
S5_TC = 64
ML_T = 256
ROUTE_LANES = LANES
MOE_TM = 256
ROW_TM = 256
ROW_ISSUE_UNROLL = 8


def _cparams(sem, vmem=VMEM_LIMIT_BYTES):
    return pltpu.CompilerParams(dimension_semantics=sem, vmem_limit_bytes=vmem)


def _silu(x):
    return x * jax.nn.sigmoid(x)


def _gelu_tanh(x):
    c = math.sqrt(2.0 / math.pi)
    return 0.5 * x * (1.0 + jnp.tanh(c * (x + 0.044715 * (x * x * x))))


def _ada_norm_tile(x, g, shift, scale):
    r = lax.rsqrt(jnp.mean(x * x, axis=-1, keepdims=True) + RMS_EPS)
    return (x * r * g) * (1.0 + scale) + shift


def _mod_kernel(c_ref, w_ref, b_ref, o_ref):
    s = _silu(c_ref[...])
    o_ref[...] = jnp.dot(s, w_ref[...], preferred_element_type=F32,
                         precision=lax.Precision.HIGHEST) + b_ref[...]


def _modulation(cond, w_ada, b_ada, tn=1024):
    depth, d, n = w_ada.shape
    return pl.pallas_call(
        _mod_kernel,
        out_shape=jax.ShapeDtypeStruct((depth, MOD_ROWS, n), F32),
        grid=(depth, n // tn),
        in_specs=[
            pl.BlockSpec((MOD_ROWS, d), lambda l, j: (0, 0)),
            pl.BlockSpec((None, d, tn), lambda l, j: (l, 0, j)),
            pl.BlockSpec((None, 1, tn), lambda l, j: (l, 0, j)),
        ],
        out_specs=pl.BlockSpec((None, MOD_ROWS, tn), lambda l, j: (l, 0, j)),
        compiler_params=_cparams(("parallel", "parallel")),
        name="modulation",
    )(cond, w_ada, b_ada.reshape(depth, 1, n))


class _Mods:
    def __init__(self, mods, n_ctx, dec_seq):
        self.mods = mods
        self.n_ctx = n_ctx
        self.dec_seq = dec_seq
        self.d = mods.shape[-1]

    def spec(self, layer, k, tm, row_offset=0):
        base = layer * MOD_ROWS * N_MOD + k
        n_ctx, dec_seq = self.n_ctx, self.dec_seq

        def index(i, *_):
            row0 = i * tm + row_offset
            r = jnp.where(row0 < n_ctx, 0, 1 + (row0 - n_ctx) // dec_seq)
            return (base + r * N_MOD, 0, 0)

        return pl.BlockSpec((None, 1, self.d), index)


def _embed_kernel(x_ref, p_ref, o_ref):
    o_ref[...] = x_ref[...] + p_ref[...]


def _grid_pos_embed(n_tokens, d):
    rows = n_tokens // GRID_W
    r, col = jnp.meshgrid(jnp.arange(rows, dtype=F32), jnp.arange(GRID_W, dtype=F32), indexing="ij")
    quarter = d // 4
    omega = 1.0 / (POS_BASE ** (jnp.arange(quarter, dtype=F32) / quarter))

    def axis_embed(pos):
        ang = pos.reshape(-1, 1) * omega[None, :]
        return jnp.concatenate([jnp.sin(ang), jnp.cos(ang)], axis=-1)

    return jnp.concatenate([axis_embed(r), axis_embed(col)], axis=-1)


def _embed(x_sample):
    b, l, d = x_sample.shape
    pos = _grid_pos_embed(l, d)
    return pl.pallas_call(
        _embed_kernel,
        out_shape=jax.ShapeDtypeStruct((b, l, d), F32),
        grid=(b,),
        in_specs=[pl.BlockSpec((None, l, d), lambda i: (i, 0, 0)),
                  pl.BlockSpec((l, d), lambda i: (0, 0))],
        out_specs=pl.BlockSpec((None, l, d), lambda i: (i, 0, 0)),
        compiler_params=_cparams(("parallel",)),
        name="pos_embed",
    )(x_sample, pos)


def _adanorm_kernel(x_ref, g_ref, sh_ref, sc_ref, o_ref):
    o_ref[...] = _ada_norm_tile(x_ref[...], g_ref[...], sh_ref[...], sc_ref[...]).astype(o_ref.dtype)


def _adanorm(x, g, mods, layer, k_shift, k_scale, tm=512):
    n, d = x.shape
    return pl.pallas_call(
        _adanorm_kernel,
        out_shape=jax.ShapeDtypeStruct((n, d), F32),
        grid=(n // tm,),
        in_specs=[pl.BlockSpec((tm, d), lambda i: (i, 0)),
                  pl.BlockSpec((1, d), lambda i: (0, 0)),
                  mods.spec(layer, k_shift, tm),
                  mods.spec(layer, k_scale, tm)],
        out_specs=pl.BlockSpec((tm, d), lambda i: (i, 0)),
        compiler_params=_cparams(("parallel",)),
        name="adanorm",
    )(x, g.reshape(1, d), mods.mods, mods.mods)


def _final_norm_kernel(x_ref, g_ref, o_ref):
    x = x_ref[...]
    r = lax.rsqrt(jnp.mean(x * x, axis=-1, keepdims=True) + RMS_EPS)
    o_ref[...] = x * r * g_ref[...]


def _final_norm(x, g, row0, nrows, tm=512):
    d = x.shape[1]
    off = row0 // tm
    return pl.pallas_call(
        _final_norm_kernel,
        out_shape=jax.ShapeDtypeStruct((nrows, d), F32),
        grid=(nrows // tm,),
        in_specs=[pl.BlockSpec((tm, d), lambda i: (i + off, 0)),
                  pl.BlockSpec((1, d), lambda i: (0, 0))],
        out_specs=pl.BlockSpec((tm, d), lambda i: (i, 0)),
        compiler_params=_cparams(("parallel",)),
        name="final_norm",
    )(x, g.reshape(1, d))


def _s5_disc_kernel(lre_ref, lim_ref, ls_ref, bre_ref, bim_ref, are_ref, aim_ref, bbre_ref, bbim_ref):
    lr = lre_ref[...]
    li = lim_ref[...]
    dt = jnp.exp(ls_ref[...])
    mag = jnp.exp(lr * dt)
    ar = mag * jnp.cos(li * dt)
    ai = mag * jnp.sin(li * dt)
    den = lr * lr + li * li
    zr = ((ar - 1.0) * lr + ai * li) / den
    zi = (ai * lr - (ar - 1.0) * li) / den
    are_ref[...] = ar
    aim_ref[...] = ai
    br = bre_ref[...]
    bi = bim_ref[...]
    bbre_ref[...] = zr[None] * br - zi[None] * bi
    bbim_ref[...] = zr[None] * bi + zi[None] * br


def _s5_discretize(lam_re, lam_im, log_step, b_re, b_im):
    _, g, p = lam_re.shape
    h = b_re.shape[-1]
    bt_re = jnp.transpose(b_re, (0, 3, 1, 2))
    bt_im = jnp.transpose(b_im, (0, 3, 1, 2))
    vec = pl.BlockSpec((None, g, p), lambda i: (i, 0, 0))
    mat = pl.BlockSpec((None, h, g, p), lambda i: (i, 0, 0, 0))
    return pl.pallas_call(
        _s5_disc_kernel,
        out_shape=(jax.ShapeDtypeStruct((2, g, p), F32), jax.ShapeDtypeStruct((2, g, p), F32),
                   jax.ShapeDtypeStruct((2, h, g, p), F32), jax.ShapeDtypeStruct((2, h, g, p), F32)),
        grid=(2,),
        in_specs=[vec, vec, pl.BlockSpec((None, g, 1), lambda i: (i, 0, 0)), mat, mat],
        out_specs=(vec, vec, mat, mat),
        compiler_params=_cparams(("parallel",)),
        name="s5_discretize",
    )(lam_re, lam_im, log_step.reshape(2, g, 1), bt_re, bt_im)


def _s5_operands(ab_re, ab_im, bb_re, bb_im, c_re, c_im):
    _, g, p = ab_re.shape
    h = bb_re.shape[1]
    gl = S5_LANE_GROUPS
    nlc = g // gl
    eye = jnp.eye(gl, dtype=F32)

    def bblock(bb):
        t = jnp.transpose(bb, (0, 2, 1, 3)).reshape(2, nlc, gl, h, p)
        return (t[:, :, :, :, None, :] * eye[None, None, :, None, :, None]).reshape(2, nlc, gl * h, gl * p)

    def cblock(c):
        t = c.reshape(2, nlc, gl, h, p)
        t = jnp.transpose(t, (0, 1, 2, 4, 3))
        return (t[:, :, :, :, None, :] * eye[None, None, :, None, :, None]).reshape(2, nlc, gl * p, gl * h)

    bmat = jnp.concatenate([bblock(bb_re), bblock(bb_im)], axis=-1).astype(BF16)
    cmat = jnp.concatenate([cblock(c_re), cblock(-c_im)], axis=-2).astype(BF16)
    avec = jnp.concatenate([ab_re.reshape(2, nlc, 1, gl * p), ab_im.reshape(2, nlc, 1, gl * p)], axis=-1)
    return bmat, cmat, avec


def _s5_state_to_lanes(s_re, s_im):
    b, _, g, p = s_re.shape
    gl = S5_LANE_GROUPS
    nlc = g // gl

    def lay(s):
        return jnp.transpose(s.reshape(b, 2, nlc, gl * p), (1, 2, 0, 3))

    return jnp.concatenate([lay(s_re), lay(s_im)], axis=-1)


def _s5_state_from_lanes(s, g, p):
    _, nlc, b, s2 = s.shape
    half = s2 // 2

    def unlay(t):
        return jnp.transpose(t, (2, 0, 1, 3)).reshape(b, 2, g, p)

    return unlay(s[..., :half]), unlay(s[..., half:])


def _s5_scan_kernel(*refs, seq, tc, aliased):
    if aliased:
        h_ref, bm_ref, cm_ref, a_ref, d_ref, s0_ref, _, z_ref, sf_ref = refs[:9]
    else:
        h_ref, bm_ref, cm_ref, a_ref, d_ref, s0_ref, z_ref, sf_ref = refs[:8]
    scr = (refs[-10:-5], refs[-5:])
    nb = SUBLANES
    half = a_ref.shape[-1] // 2
    nc = seq // tc
    a_re = [jnp.broadcast_to(a_ref[dr][:, :half], (nb, half)) for dr in range(2)]
    a_im = [jnp.broadcast_to(a_ref[dr][:, half:], (nb, half)) for dr in range(2)]

    def chunk(c, carry):
        t0s = (c * tc, (nc - 1 - c) * tc)
        carry = list(carry)
        for dr in range(2):
            u_scr, bu_scr, _, _, _ = scr[dr]
            for t in range(tc):
                u_scr[t * nb:(t + 1) * nb, :] = h_ref[pl.ds(t0s[dr] + t, nb, stride=seq), :]
            bu_scr[...] = jnp.dot(u_scr[...].astype(BF16), bm_ref[dr], preferred_element_type=F32)
        for dr in range(2):
            _, bu_scr, x_scr, _, _ = scr[dr]
            xr, xi = carry[2 * dr], carry[2 * dr + 1]
            for t in (range(tc) if dr == 0 else range(tc - 1, -1, -1)):
                bu = bu_scr[t * nb:(t + 1) * nb, :]
                xr, xi = (a_re[dr] * xr - a_im[dr] * xi + bu[:, :half],
                          a_re[dr] * xi + a_im[dr] * xr + bu[:, half:])
                x_scr[t * nb:(t + 1) * nb, :half] = xr
                x_scr[t * nb:(t + 1) * nb, half:] = xi
            carry[2 * dr], carry[2 * dr + 1] = xr, xi
        for dr in range(2):
            _, _, x_scr, yo_scr, y_scr = scr[dr]
            yo_scr[...] = jnp.dot(x_scr[...].astype(BF16), cm_ref[dr], preferred_element_type=F32)
            for t in range(tc):
                y_scr[pl.ds(t0s[dr] + t, nb, stride=seq), :] = yo_scr[t * nb:(t + 1) * nb, :]
        return tuple(carry)

    s0f = s0_ref[0]
    s0b = s0_ref[1]
    init = (s0f[:, :half], s0f[:, half:], s0b[:, :half], s0b[:, half:])
    xr_f, xi_f, xr_b, xi_b = lax.fori_loop(0, nc, chunk, init)
    sf_ref[0, :, :half] = xr_f
    sf_ref[0, :, half:] = xi_f
    sf_ref[1, :, :half] = xr_b
    sf_ref[1, :, half:] = xi_b
    y = h_ref[...] * d_ref[...] + scr[0][4][...] + scr[1][4][...]
    z_ref[...] = _gelu_tanh(y).astype(z_ref.dtype)


def _s5_scan(h, bmat, cmat, avec, dvec, s0, row0, nseq, seq, z_prev=None):
    n, d = h.shape
    nlc = d // LANES
    s2 = avec.shape[-1]
    nb = SUBLANES
    rows = nb * seq
    off = row0 // rows
    aliased = z_prev is not None
    in_specs = [
        pl.BlockSpec((rows, LANES), lambda i, j: (i + off, j)),
        pl.BlockSpec((2, None, LANES, s2), lambda i, j: (0, j, 0, 0)),
        pl.BlockSpec((2, None, s2, LANES), lambda i, j: (0, j, 0, 0)),
        pl.BlockSpec((2, None, 1, s2), lambda i, j: (0, j, 0, 0)),
        pl.BlockSpec((1, LANES), lambda i, j: (0, j)),
        pl.BlockSpec((2, None, nb, s2), lambda i, j: (0, j, i, 0)),
    ]
    args = [h, bmat, cmat, avec, dvec, s0]
    io_alias = {}
    if aliased:
        in_specs.append(pl.BlockSpec(memory_space=pl.ANY))
        args.append(z_prev)
        io_alias = {6: 0}
    return pl.pallas_call(
        functools.partial(_s5_scan_kernel, seq=seq, tc=S5_TC, aliased=aliased),
        out_shape=(jax.ShapeDtypeStruct((n, d), BF16), jax.ShapeDtypeStruct((2, nlc, nseq, s2), F32)),
        grid=(nseq // nb, nlc),
        in_specs=in_specs,
        out_specs=(pl.BlockSpec((rows, LANES), lambda i, j: (i + off, j)),
                   pl.BlockSpec((2, None, nb, s2), lambda i, j: (0, j, i, 0))),
        scratch_shapes=2 * [pltpu.VMEM((S5_TC * nb, LANES), F32),
                            pltpu.VMEM((S5_TC * nb, s2), F32),
                            pltpu.VMEM((S5_TC * nb, s2), F32),
                            pltpu.VMEM((S5_TC * nb, LANES), F32),
                            pltpu.VMEM((rows, LANES), F32)],
        input_output_aliases=io_alias,
        compiler_params=_cparams(("parallel", "parallel")),
        name="s5_scan",
    )(*args)


def _mm_res_kernel(*refs, n_w, has_bias):
    z_ref = refs[0]
    w_refs = refs[1:1 + n_w]
    pos = 1 + n_w
    b_refs = refs[pos:pos + n_w] if has_bias else ()
    pos += n_w if has_bias else 0
    x_ref, gate_ref, o_ref = refs[pos:pos + 3]
    z = z_ref[...]
    acc = [jnp.dot(z, w[...], preferred_element_type=F32) for w in w_refs]
    if has_bias:
        acc = [a + b[...] for a, b in zip(acc, b_refs)]
    y = acc[0] if n_w == 1 else acc[0] * jax.nn.sigmoid(acc[1])
    o_ref[...] = x_ref[...] + gate_ref[...] * y


def _mm_residual(z, ws, bs, x, mods, layer, k_gate, tm=1024, tn=512):
    n, k = z.shape
    n_out = ws[0].shape[1]
    n_w = len(ws)
    has_bias = bs is not None
    in_specs = [pl.BlockSpec((tm, k), lambda i, j: (i, 0))]
    in_specs += [pl.BlockSpec((k, tn), lambda i, j: (0, j)) for _ in ws]
    args = [z, *ws]
    if has_bias:
        in_specs += [pl.BlockSpec((1, tn), lambda i, j: (0, j)) for _ in bs]
        args += [b.reshape(1, n_out) for b in bs]
    gate_spec = mods.spec(layer, k_gate, tm)
    gate_spec = pl.BlockSpec((None, 1, tn), lambda i, j, f=gate_spec.index_map: (f(i)[0], 0, j))
    in_specs += [pl.BlockSpec((tm, tn), lambda i, j: (i, j)), gate_spec]
    args += [x, mods.mods]
    return pl.pallas_call(
        functools.partial(_mm_res_kernel, n_w=n_w, has_bias=has_bias),
        out_shape=jax.ShapeDtypeStruct((n, n_out), F32),
        grid=(n // tm, n_out // tn),
        in_specs=in_specs,
        out_specs=pl.BlockSpec((tm, tn), lambda i, j: (i, j)),
        compiler_params=_cparams(("parallel", "parallel")),
        name=f"proj_residual_{n_w}w",
    )(*args)


def _mm_norm_kernel(x_ref, g_ref, sh_ref, sc_ref, w_ref, o_ref, h_scr, *, precision):
    @pl.when(pl.program_id(1) == 0)
    def _():
        h_scr[...] = _ada_norm_tile(x_ref[...], g_ref[...], sh_ref[...], sc_ref[...]).astype(h_scr.dtype)

    o_ref[...] = jnp.dot(h_scr[...], w_ref[...], preferred_element_type=F32,
                         precision=precision).astype(o_ref.dtype)


def _mm_adanorm(x, g, mods, layer, k_shift, k_scale, w, out_dtype, tm=1024, tn=1024):
    n, d = x.shape
    n_out = w.shape[1]
    tn = min(tn, n_out)
    exact = w.dtype == F32
    return pl.pallas_call(
        functools.partial(_mm_norm_kernel, precision=lax.Precision.HIGHEST if exact else None),
        out_shape=jax.ShapeDtypeStruct((n, n_out), out_dtype),
        grid=(n // tm, n_out // tn),
        in_specs=[pl.BlockSpec((tm, d), lambda i, j: (i, 0)),
                  pl.BlockSpec((1, d), lambda i, j: (0, 0)),
                  mods.spec(layer, k_shift, tm),
                  mods.spec(layer, k_scale, tm),
                  pl.BlockSpec((d, tn), lambda i, j: (0, j))],
        out_specs=pl.BlockSpec((tm, tn), lambda i, j: (i, j)),
        scratch_shapes=[pltpu.VMEM((tm, d), F32 if exact else BF16)],
        compiler_params=_cparams(("parallel", "arbitrary")),
        name="adanorm_proj",
    )(x, g.reshape(1, d), mods.mods, mods.mods, w)


_G_FWD = ML_HEADS
_G_BWD = 3 * ML_HEADS


def _gate_kernel(xc_ref, xr_ref, bc_ref, br_ref, g_ref, w_ref, e_ref, dr_ref, wr_ref):
    t = xc_ref.shape[0]
    r_i = lax.broadcasted_iota(jnp.int32, (t, t), 0)
    c_i = lax.broadcasted_iota(jnp.int32, (t, t), 1)
    lower = (c_i <= r_i).astype(F32)
    upper = (c_i >= r_i).astype(F32)
    hi = lax.Precision.HIGHEST
    xc = xc_ref[...] + bc_ref[...]
    fc = jax.nn.log_sigmoid(xc)
    lane = lax.broadcasted_iota(jnp.int32, xc.shape, 1)
    g_c = jnp.where(lane < 2 * ML_HEADS,
                    jnp.dot(lower, fc, preferred_element_type=F32, precision=hi),
                    jnp.dot(upper, fc, preferred_element_type=F32, precision=hi))
    e_c = jnp.broadcast_to(jnp.sum(fc, axis=0, keepdims=True), xc.shape)
    g_ref[...] = g_c
    e_ref[...] = e_c
    w_ref[...] = e_c - g_c + pltpu.roll(xc, ML_HEADS, 1)
    xr = xr_ref[...] + br_ref[...]
    fr = jax.nn.log_sigmoid(xr)
    row = lax.broadcasted_iota(jnp.int32, xr.shape, 0)
    g_r = jnp.where(row < 2 * ML_HEADS,
                    jnp.dot(fr, upper, preferred_element_type=F32, precision=hi),
                    jnp.dot(fr, lower, preferred_element_type=F32, precision=hi))
    i_r = pltpu.roll(xr, ML_HEADS, 0)
    dr_ref[...] = i_r - g_r
    wr_ref[...] = jnp.sum(fr, axis=1, keepdims=True) - g_r + i_r


def _gate_prep(gates, b_gates, tch):
    n = gates.shape[0]
    ng = 4 * ML_HEADS
    bias_c = jnp.pad(b_gates, (0, LANES - ng)).reshape(1, LANES)
    bias_r = b_gates.reshape(ng, 1)
    gates_r = jnp.transpose(gates[:, :ng])
    col = pl.BlockSpec((tch, LANES), lambda i: (i, 0))
    rowb = pl.BlockSpec((ng, tch), lambda i: (0, i))
    return pl.pallas_call(
        _gate_kernel,
        out_shape=(jax.ShapeDtypeStruct((n, LANES), F32),) * 3 + (jax.ShapeDtypeStruct((ng, n), F32),) * 2,
        grid=(n // tch,),
        in_specs=[col, rowb, pl.BlockSpec((1, LANES), lambda i: (0, 0)),
                  pl.BlockSpec((ng, 1), lambda i: (0, 0))],
        out_specs=(col, col, col, rowb, rowb),
        compiler_params=_cparams(("parallel",)),
        name="mlstm_gates",
    )(gates, gates_r, bias_c, bias_r)


def _mlstm_kernel(q_ref, kt_ref, v_ref, o_ref, gcol_ref, grow_ref, hg_ref,
                  c0_ref, n0_ref, m0_ref, hn_ref, cf_ref, nf_ref, mf_ref,
                  hacc_f, hacc_b, c_f, c_b, vx_f, vx_b, *, seq, tch):
    nc = seq // tch
    dv = v_ref.shape[-1]
    tt = lax.broadcasted_iota(jnp.int32, (tch, tch), 0)
    ss = lax.broadcasted_iota(jnp.int32, (tch, tch), 1)
    ones_col = (lax.broadcasted_iota(jnp.int32, (tch, LANES), 1) == 0).astype(BF16)
    for dr, (c_scr, vx) in enumerate(((c_f, vx_f), (c_b, vx_b))):
        c_scr[:, :dv] = c0_ref[dr]
        c_scr[:, dv:] = n0_ref[dr]
        vx[:, dv:] = ones_col

    def chunk_dir(dr, r0, m):
        hacc = (hacc_f, hacc_b)[dr]
        c_scr = (c_f, c_b)[dr]
        vx = (vx_f, vx_b)[dr]
        rows = pl.ds(pl.multiple_of(r0, tch), tch)
        q = q_ref[rows, :]
        kt = kt_ref[:, rows]
        vx[:, :dv] = v_ref[rows, :]
        gc = gcol_ref[rows, :]
        gr = grow_ref[:, rows]
        g_col = gc[:, 3 * dr:3 * dr + 1]
        w_col = gc[:, 3 * dr + 1:3 * dr + 2]
        e_col = gc[:, 3 * dr + 2:3 * dr + 3]
        d_row = gr[2 * dr:2 * dr + 1, :]
        w_row = gr[2 * dr + 1:2 * dr + 2, :]
        mask = (ss <= tt) if dr == 0 else (ss >= tt)
        a_col = g_col + m
        dmat = jnp.where(mask, g_col + d_row, -jnp.inf)
        mt = jnp.maximum(a_col, jnp.max(dmat, axis=1, keepdims=True))
        qk = jnp.dot(q, kt, preferred_element_type=F32)
        s = (qk * jnp.exp(dmat - mt)).astype(BF16)
        inter = jnp.exp(a_col - mt)
        vext = vx[...]
        tot = jnp.dot(s, vext, preferred_element_type=F32) \
            + inter * jnp.dot(q, c_scr[...].astype(BF16), preferred_element_type=F32)
        den = tot[:, dv:dv + 1]
        hacc[rows, :] = tot[:, :dv] / jnp.maximum(jnp.abs(den), jnp.exp(-mt))
        g_end = jnp.max(e_col, axis=0, keepdims=True)
        m_new = jnp.maximum(g_end + m, jnp.max(w_col, axis=0, keepdims=True))
        decay = jnp.exp(g_end + m - m_new)
        kw = (kt.astype(F32) * jnp.exp(w_row - m_new)).astype(BF16)
        c_scr[...] = decay * c_scr[...] + jnp.dot(kw, vext, preferred_element_type=F32)
        return m_new

    def body(c, carry):
        m_f, m_b = carry
        return chunk_dir(0, c * tch, m_f), chunk_dir(1, (nc - 1 - c) * tch, m_b)

    m_f, m_b = lax.fori_loop(0, nc, body, (m0_ref[0:1, 0:1], m0_ref[1:2, 0:1]))
    for dr, (c_scr, m_fin) in enumerate(((c_f, m_f), (c_b, m_b))):
        cf_ref[dr] = c_scr[:, :dv]
        nf_ref[dr] = c_scr[:, dv:]
        mf_ref[dr:dr + 1, :] = jnp.broadcast_to(m_fin, (1, LANES))
    hs = hacc_f[...] + hacc_b[...]
    hn = hs * lax.rsqrt(jnp.mean(hs * hs, axis=-1, keepdims=True) + RMS_EPS)
    hn = hn * hg_ref[...] * jax.nn.sigmoid(o_ref[...].astype(F32))
    hn_ref[...] = hn.astype(hn_ref.dtype)


def _mlstm(proj, k_t, gcol, grow, head_g, c0, n0, m0, row0, nseq, seq, hn_prev=None):
    n = proj.shape[0]
    h = ML_HEADS
    dk = c0.shape[-2]
    dv = c0.shape[-1]
    d = h * dv
    qk = h * dk
    off = row0 // seq
    tch = min(ML_T, seq)
    aliased = hn_prev is not None
    c_spec = pl.BlockSpec((None, 2, None, dk, dv), lambda b, j: (b, 0, j, 0, 0))
    n_spec = pl.BlockSpec((None, None, 2, dk, LANES), lambda b, j: (b, j, 0, 0, 0))
    m_spec = pl.BlockSpec((None, None, 2, LANES), lambda b, j: (b, j, 0, 0))
    in_specs = [
        pl.BlockSpec((seq, dk), lambda b, j: (b + off, j)),
        pl.BlockSpec((dk, seq), lambda b, j: (j, b + off)),
        pl.BlockSpec((seq, dv), lambda b, j: (b + off, 2 * qk // dv + j)),
        pl.BlockSpec((seq, dv), lambda b, j: (b + off, (2 * qk + d) // dv + j)),
        pl.BlockSpec((None, None, seq, 8), lambda b, j: (b, j, 0, 0)),
        pl.BlockSpec((None, None, 4, seq), lambda b, j: (b, j, 0, 0)),
        pl.BlockSpec((1, dv), lambda b, j: (0, j)),
        c_spec, n_spec, m_spec,
    ]
    args = [proj, k_t, proj, proj, gcol, grow, head_g.reshape(1, d), c0, n0, m0]
    n_in = len(args)
    io_alias = {}
    if aliased:
        in_specs.append(pl.BlockSpec(memory_space=pl.ANY))
        args.append(hn_prev)
        io_alias = {n_in: 0}

    def kern(*refs):
        if aliased:
            refs = refs[:n_in] + refs[n_in + 1:]
        _mlstm_kernel(*refs, seq=seq, tch=tch)

    return pl.pallas_call(
        kern,
        out_shape=(jax.ShapeDtypeStruct((n, d), BF16),
                   jax.ShapeDtypeStruct((nseq, 2, h, dk, dv), F32),
                   jax.ShapeDtypeStruct((nseq, h, 2, dk, LANES), F32),
                   jax.ShapeDtypeStruct((nseq, h, 2, LANES), F32)),
        grid=(nseq, h),
        in_specs=in_specs,
        out_specs=(pl.BlockSpec((seq, dv), lambda b, j: (b + off, j)), c_spec, n_spec, m_spec),
        scratch_shapes=[pltpu.VMEM((seq, dv), F32), pltpu.VMEM((seq, dv), F32),
                        pltpu.VMEM((dk, dv + LANES), F32), pltpu.VMEM((dk, dv + LANES), F32),
                        pltpu.VMEM((tch, dv + LANES), BF16), pltpu.VMEM((tch, dv + LANES), BF16)],
        input_output_aliases=io_alias,
        compiler_params=_cparams(("parallel", "parallel")),
        name="mlstm",
    )(*args)


_R_EID, _R_W, _R_RANK = 0, 2, 4
_R_LOGIT0 = MOE_GROUPS


def _router_kernel(x_ref, g_ref, sh_ref, sc_ref, w_ref, b_ref, h_ref, route_ref, cnt_ref, carry):
    i = pl.program_id(0)

    @pl.when(i == 0)
    def _():
        carry[...] = jnp.zeros_like(carry)

    h = _ada_norm_tile(x_ref[...], g_ref[...], sh_ref[...], sc_ref[...])
    h_ref[...] = h
    tm = h.shape[0]
    logits = jnp.dot(h, w_ref[...], preferred_element_type=F32,
                     precision=lax.Precision.HIGHEST) + b_ref[...]
    lane = lax.broadcasted_iota(jnp.int32, logits.shape, 1)
    big = jnp.int32(ROUTE_LANES)

    def first_lane(cond):
        return jnp.min(jnp.where(cond, lane, big), axis=1, keepdims=True)

    glog = jnp.where(lane < MOE_GROUPS, logits, -jnp.inf)
    ge = jnp.exp(glog - jnp.max(glog, axis=1, keepdims=True))
    pgrp = ge / jnp.sum(ge, axis=1, keepdims=True)
    pg = jnp.max(pgrp, axis=1, keepdims=True)
    grp = first_lane(pgrp == pg)
    e_lane = lane - _R_LOGIT0
    in_grp = (e_lane >= 0) & (e_lane < MOE_EXPERTS) & ((e_lane // MOE_EPG) == grp)
    elog = jnp.where(in_grp, logits, -jnp.inf)
    ee = jnp.exp(elog - jnp.max(elog, axis=1, keepdims=True))
    pe = jnp.where(in_grp, ee / jnp.sum(ee, axis=1, keepdims=True), -1.0)
    p0 = jnp.max(pe, axis=1, keepdims=True)
    l0 = first_lane(pe == p0)
    pe1 = jnp.where(lane == l0, -1.0, pe)
    p1 = jnp.max(pe1, axis=1, keepdims=True)
    l1 = first_lane(pe1 == p1)
    psum = p0 + p1
    w0 = pg * p0 / psum
    w1 = pg * p1 / psum
    onehot = ((lane == l0) | (lane == l1)).astype(BF16)
    r_i = lax.broadcasted_iota(jnp.int32, (tm, tm), 0)
    c_i = lax.broadcasted_iota(jnp.int32, (tm, tm), 1)
    tri = (c_i < r_i).astype(BF16)
    before = jnp.dot(tri, onehot, preferred_element_type=F32) + carry[...]
    rank0 = jnp.sum(jnp.where(lane == l0, before, 0.0), axis=1, keepdims=True)
    rank1 = jnp.sum(jnp.where(lane == l1, before, 0.0), axis=1, keepdims=True)
    carry[...] = carry[...] + jnp.sum(onehot.astype(F32), axis=0, keepdims=True)
    cnt_ref[...] = carry[...]
    cols = [(l0 - _R_LOGIT0).astype(F32), (l1 - _R_LOGIT0).astype(F32), w0, w1, rank0, rank1]
    route = jnp.zeros(logits.shape, F32)
    for c, val in enumerate(cols):
        route = jnp.where(lane == c, val, route)
    route_ref[...] = route


def _router(x, g, mods, layer, w_route, b_route, tm=512):
    n, d = x.shape
    return pl.pallas_call(
        _router_kernel,
        out_shape=(jax.ShapeDtypeStruct((n, d), F32),
                   jax.ShapeDtypeStruct((n, ROUTE_LANES), F32),
                   jax.ShapeDtypeStruct((1, ROUTE_LANES), F32)),
        grid=(n // tm,),
        in_specs=[pl.BlockSpec((tm, d), lambda i: (i, 0)),
                  pl.BlockSpec((1, d), lambda i: (0, 0)),
                  mods.spec(layer, 3, tm),
                  mods.spec(layer, 4, tm),
                  pl.BlockSpec((d, ROUTE_LANES), lambda i: (0, 0)),
                  pl.BlockSpec((1, ROUTE_LANES), lambda i: (0, 0))],
        out_specs=(pl.BlockSpec((tm, d), lambda i: (i, 0)),
                   pl.BlockSpec((tm, ROUTE_LANES), lambda i: (i, 0)),
                   pl.BlockSpec((1, ROUTE_LANES), lambda i: (0, 0))),
        scratch_shapes=[pltpu.VMEM((1, ROUTE_LANES), F32)],
        compiler_params=_cparams(("arbitrary",)),
        name="moe_router",
    )(x, g.reshape(1, d), mods.mods, mods.mods, w_route, b_route)


def _dispatch_kernel(pcnt_ref, pend_ref, dest_ref, h_ref, xs_ref, zbuf, sem, zsem):
    tm = h_ref.shape[0]
    zrows = zbuf.shape[0]

    @pl.when(pl.program_id(0) == 0)
    def _():
        zbuf[...] = jnp.zeros_like(zbuf)

        def clear(e, _):
            @pl.when(pcnt_ref[e] > 0)
            def _():
                start = pl.multiple_of(pend_ref[e] - zrows, zrows)
                cp = pltpu.make_async_copy(zbuf, xs_ref.at[pl.ds(start, zrows)], zsem)
                cp.start()
                cp.wait()
            return 0

        lax.fori_loop(0, MOE_EXPERTS, clear, 0)

    def issue(r, _):
        for k in range(2):
            d = dest_ref[0, 0, 2 * r + k]
            pltpu.make_async_copy(h_ref.at[pl.ds(r, 1)], xs_ref.at[pl.ds(d, 1)], sem).start(priority=k)
        return 0

    lax.fori_loop(0, tm, issue, 0, unroll=ROW_ISSUE_UNROLL)
    for _ in range(2):
        pltpu.make_async_copy(h_ref, xs_ref.at[pl.ds(0, tm)], sem).wait()


def _dispatch(h, dest, pcnt, pends, n_pad, tm=ROW_TM):
    n, d = h.shape
    nblk = n // tm
    grid_spec = pltpu.PrefetchScalarGridSpec(
        num_scalar_prefetch=2,
        grid=(nblk,),
        in_specs=[pl.BlockSpec((1, 1, 2 * tm), lambda i, *_: (i, 0, 0), memory_space=pltpu.SMEM),
                  pl.BlockSpec((tm, d), lambda i, *_: (i, 0))],
        out_specs=pl.BlockSpec(memory_space=pl.ANY),
        scratch_shapes=[pltpu.VMEM((MOE_TM, d), F32), pltpu.SemaphoreType.DMA(()),
                        pltpu.SemaphoreType.DMA(())],
    )
    return pl.pallas_call(
        _dispatch_kernel,
        out_shape=jax.ShapeDtypeStruct((n_pad, d), F32),
        grid_spec=grid_spec,
        compiler_params=_cparams(("arbitrary",)),
        name="moe_dispatch",
    )(pcnt, pends, dest.reshape(nblk, 1, 2 * tm), h)


def _experts_kernel(be_ref, first_ref, nxt_ref, nu_ref, x_ref, wg_hbm, wu_hbm, wd_hbm, o_ref,
                    stg_g, stg_u, stg_d, wg_bf, wu_bf, wd_bf, sem, *, layer, cast_rows):
    i = pl.program_id(0)
    active = i < nu_ref[0]
    stages = ((wg_hbm, stg_g, wg_bf), (wu_hbm, stg_u, wu_bf), (wd_hbm, stg_d, wd_bf))

    def weight_copies(e):
        return [pltpu.make_async_copy(hbm.at[layer, e], stg, sem.at[k])
                for k, (hbm, stg, _) in enumerate(stages)]

    @pl.when(i == 0)
    def _():
        for cp in weight_copies(be_ref[0]):
            cp.start()

    @pl.when(active & (first_ref[i] == 1))
    def _():
        for cp in weight_copies(be_ref[i]):
            cp.wait()
        for _, stg, wbf in stages:
            def cast(r, _, stg=stg, wbf=wbf):
                rows = pl.ds(pl.multiple_of(r * cast_rows, cast_rows), cast_rows)
                wbf[rows, :] = stg[rows, :].astype(BF16)
                return 0

            lax.fori_loop(0, stg.shape[0] // cast_rows, cast, 0)

        @pl.when(nxt_ref[i] >= 0)
        def _():
            for cp in weight_copies(nxt_ref[i]):
                cp.start()

    @pl.when(active)
    def _():
        x = x_ref[...].astype(BF16)
        g = jnp.dot(x, wg_bf[...], preferred_element_type=F32)
        u = jnp.dot(x, wu_bf[...], preferred_element_type=F32)
        a = (_silu(g) * u).astype(BF16)
        o_ref[...] = jnp.dot(a, wd_bf[...], preferred_element_type=F32)

    @pl.when(jnp.logical_not(active))
    def _():
        o_ref[...] = jnp.zeros_like(o_ref)


def _experts(xs, block_expert, block_first, block_next, n_used, w_gate, w_up, w_down, layer, tm=MOE_TM):
    n_pad, d = xs.shape
    f = w_gate.shape[-1]
    nblk = n_pad // tm
    grid_spec = pltpu.PrefetchScalarGridSpec(
        num_scalar_prefetch=4,
        grid=(nblk,),
        in_specs=[pl.BlockSpec((tm, d), lambda i, be, fi, nx, nu: (jnp.minimum(i, nu[0] - 1), 0)),
                  pl.BlockSpec(memory_space=pl.ANY),
                  pl.BlockSpec(memory_space=pl.ANY),
                  pl.BlockSpec(memory_space=pl.ANY)],
        out_specs=pl.BlockSpec((tm, d), lambda i, *_: (i, 0)),
        scratch_shapes=[pltpu.VMEM((d, f), F32), pltpu.VMEM((d, f), F32), pltpu.VMEM((f, d), F32),
                        pltpu.VMEM((d, f), BF16), pltpu.VMEM((d, f), BF16), pltpu.VMEM((f, d), BF16),
                        pltpu.SemaphoreType.DMA((3,))],
    )
    return pl.pallas_call(
        functools.partial(_experts_kernel, layer=layer, cast_rows=256),
        out_shape=jax.ShapeDtypeStruct((n_pad, d), F32),
        grid_spec=grid_spec,
        compiler_params=_cparams(("arbitrary",), vmem=EXPERT_VMEM_LIMIT_BYTES),
        name="moe_experts",
    )(block_expert, block_first, block_next, n_used, xs, w_gate, w_up, w_down)


def _combine_kernel(dest_ref, y_hbm, x_ref, route_ref, gate_ref, o_ref, buf, sem):
    tm = x_ref.shape[0]

    def issue(r, _):
        for k in range(2):
            d = dest_ref[0, 0, 2 * r + k]
            pltpu.make_async_copy(y_hbm.at[pl.ds(d, 1)], buf.at[k, pl.ds(r, 1)], sem).start(priority=k)
        return 0

    lax.fori_loop(0, tm, issue, 0, unroll=ROW_ISSUE_UNROLL)
    for k in range(2):
        pltpu.make_async_copy(y_hbm.at[pl.ds(0, tm)], buf.at[k], sem).wait()
    route = route_ref[...]
    w0 = route[:, _R_W:_R_W + 1]
    w1 = route[:, _R_W + 1:_R_W + 2]
    o_ref[...] = x_ref[...] + gate_ref[...] * (buf[0] * w0 + buf[1] * w1)


def _combine(x, y, dest, route, mods, layer, tm=ROW_TM):
    n, d = x.shape
    nblk = n // tm
    return pl.pallas_call(
        _combine_kernel,
        out_shape=jax.ShapeDtypeStruct((n, d), F32),
        grid=(nblk,),
        in_specs=[pl.BlockSpec((1, 1, 2 * tm), lambda i: (i, 0, 0), memory_space=pltpu.SMEM),
                  pl.BlockSpec(memory_space=pl.ANY),
                  pl.BlockSpec((tm, d), lambda i: (i, 0)),
                  pl.BlockSpec((tm, ROUTE_LANES), lambda i: (i, 0)),
                  mods.spec(layer, 5, tm)],
        out_specs=pl.BlockSpec((tm, d), lambda i: (i, 0)),
        scratch_shapes=[pltpu.VMEM((2, tm, d), F32), pltpu.SemaphoreType.DMA(())],
        compiler_params=_cparams(("arbitrary",)),
        name="moe_combine",
    )(dest.reshape(nblk, 1, 2 * tm), y, x, route, mods.mods)


def _moe_layer(x, mods, layer, norm_g, w_group, b_group, w_expert, b_expert, w_gate, w_up, w_down):
    n, d = x.shape
    ne = MOE_EXPERTS
    pad = ROUTE_LANES - MOE_GROUPS - ne
    w_route = jnp.concatenate([w_group, w_expert, jnp.zeros((d, pad), F32)], axis=1)
    b_route = jnp.concatenate([b_group, b_expert, jnp.zeros((pad,), F32)]).reshape(1, ROUTE_LANES)
    h, route, counts = _router(x, norm_g, mods, layer, w_route, b_route)
    cnt = counts[0, _R_LOGIT0:_R_LOGIT0 + ne].astype(jnp.int32)
    pcnt = (cnt + MOE_TM - 1) // MOE_TM * MOE_TM
    pends = jnp.cumsum(pcnt)
    pstarts = pends - pcnt
    experts = jnp.arange(ne, dtype=jnp.int32)
    eid = route[:, _R_EID:_R_EID + 2].astype(jnp.int32)
    rank = route[:, _R_RANK:_R_RANK + 2].astype(jnp.int32)
    dest = (jnp.sum(jnp.where(eid[..., None] == experts, pstarts, 0), axis=-1) + rank).reshape(-1)
    n_blocks = (n * 2) // MOE_TM + ne
    n_pad = n_blocks * MOE_TM
    block_row = jnp.arange(n_blocks, dtype=jnp.int32) * MOE_TM
    block_expert = jnp.minimum(jnp.sum((pends[None, :] <= block_row[:, None]).astype(jnp.int32), axis=1), ne - 1)
    block_first = jnp.concatenate([jnp.ones((1,), jnp.int32),
                                   (block_expert[1:] != block_expert[:-1]).astype(jnp.int32)])
    later = (experts[None, :] > experts[:, None]) & (pcnt[None, :] > 0)
    next_expert = jnp.min(jnp.where(later, experts[None, :], ne), axis=1)
    next_expert = jnp.where(next_expert == ne, -1, next_expert)
    block_next = jnp.sum(jnp.where(block_expert[:, None] == experts, next_expert, 0), axis=1)
    n_used = (pends[-1:] // MOE_TM).astype(jnp.int32)
    xs = _dispatch(h, dest, pcnt, pends, n_pad)
    y = _experts(xs, block_expert, block_first, block_next, n_used, w_gate, w_up, w_down, layer)
    return _combine(x, y, dest, route, mods, layer)


class _Streams:
    def __init__(self, batch, seq, dec_batch, dec_seq):
        self.ctx = (0, batch, seq)
        self.lat = (batch * seq, dec_batch, dec_seq)


def _s5_layer(x, mods, layer, streams, norm_g, lam_re, lam_im, log_step, b_re, b_im, c_re, c_im, d_skip,
              w_a, b_a, w_b, b_b, s0_re, s0_im):
    n, d = x.shape
    g, p = lam_re.shape[1:]
    ab_re, ab_im, bb_re, bb_im = _s5_discretize(lam_re, lam_im, log_step, b_re, b_im)
    bmat, cmat, avec = _s5_operands(ab_re, ab_im, bb_re, bb_im, c_re, c_im)
    h = _adanorm(x, norm_g, mods, layer, 0, 1)
    dvec = d_skip.reshape(1, d)
    row0, nseq, seq = streams.ctx
    zero = jnp.zeros((2, d // LANES, nseq, avec.shape[-1]), F32)
    z, sf = _s5_scan(h, bmat, cmat, avec, dvec, zero, row0, nseq, seq)
    row0, nseq, seq = streams.lat
    z, _ = _s5_scan(h, bmat, cmat, avec, dvec, _s5_state_to_lanes(s0_re, s0_im), row0, nseq, seq, z_prev=z)
    x = _mm_residual(z, [w_a.astype(BF16), w_b.astype(BF16)], [b_a, b_b], x, mods, layer, 2)
    new_re, new_im = _s5_state_from_lanes(sf, g, p)
    return x, new_re, new_im


def _mlstm_layer(x, mods, layer, streams, norm_g, w_in, b_gates, head_g, w_out, c0, n0, m0):
    n, d = x.shape
    hh = ML_HEADS
    dv = d // hh
    dk = dv // 2
    qk = hh * dk
    n_main = 2 * qk + 2 * d
    col_scale = jnp.concatenate([jnp.ones((qk,), F32), jnp.full((qk,), dk ** -0.5, F32),
                                 jnp.ones((2 * d,), F32)])
    w_main = (w_in[:, :n_main] * col_scale).astype(BF16)
    w_gates = jnp.pad(w_in[:, n_main:], ((0, 0), (0, LANES - 4 * hh)))
    proj = _mm_adanorm(x, norm_g, mods, layer, 0, 1, w_main, BF16)
    gates = _mm_adanorm(x, norm_g, mods, layer, 0, 1, w_gates, F32)
    k_t = jnp.transpose(proj[:, qk:2 * qk])
    g_c, w_c, e_c, d_r, w_r = _gate_prep(gates, b_gates, ML_T)
    fwd, bwd = slice(_G_FWD, _G_FWD + hh), slice(_G_BWD, _G_BWD + hh)
    cols = jnp.stack([g_c[:, fwd], w_c[:, fwd], e_c[:, fwd], g_c[:, bwd], w_c[:, bwd], e_c[:, bwd]], axis=-1)
    cols = jnp.pad(cols, ((0, 0), (0, 0), (0, 2)))
    rows = jnp.stack([d_r[fwd], w_r[fwd], d_r[bwd], w_r[bwd]], axis=1)

    def gate_views(row0, nseq, seq):
        gc = cols[row0:row0 + nseq * seq].reshape(nseq, seq, hh, 8)
        gr = rows[:, :, row0:row0 + nseq * seq].reshape(hh, 4, nseq, seq)
        return jnp.transpose(gc, (0, 2, 1, 3)), jnp.transpose(gr, (2, 0, 1, 3))

    def n_lanes(nv):
        return jnp.pad(jnp.transpose(nv, (0, 2, 1, 3))[..., None], ((0, 0),) * 4 + ((0, LANES - 1),))

    def m_lanes(mv):
        return jnp.broadcast_to(jnp.transpose(mv, (0, 2, 1))[..., None], mv.shape[:1] + (hh, 2, LANES))

    row0, nseq, seq = streams.ctx
    gcol, grow = gate_views(row0, nseq, seq)
    hn, c_f, n_f, m_f = _mlstm(
        proj, k_t, gcol, grow, head_g,
        jnp.zeros((nseq, 2, hh, dk, dv), F32), jnp.zeros((nseq, hh, 2, dk, LANES), F32),
        jnp.full((nseq, hh, 2, LANES), NEG_BIG, F32), row0, nseq, seq)
    row0, nseq, seq = streams.lat
    gcol, grow = gate_views(row0, nseq, seq)
    hn, _, _, _ = _mlstm(proj, k_t, gcol, grow, head_g, c0, n_lanes(n0), m_lanes(m0),
                         row0, nseq, seq, hn_prev=hn)
    x = _mm_residual(hn, [w_out.astype(BF16)], None, x, mods, layer, 2)
    return x, c_f, jnp.transpose(n_f[..., 0], (0, 2, 1, 3)), jnp.transpose(m_f[..., 0], (0, 2, 1))


def kernel(x_prompt, x_sample, state_s5_re, state_s5_im, state_mlstm_C, state_mlstm_n, state_mlstm_m, c, c_ctx, w_ada, b_ada, norm1_g, norm2_g, final_norm_g, s5_lambda_re, s5_lambda_im, s5_log_step, s5_b_re, s5_b_im, s5_c_re, s5_c_im, s5_d, s5_w_glu_a, s5_b_glu_a, s5_w_glu_b, s5_b_glu_b, ml_w_in, ml_b_gates, ml_head_norm_g, ml_w_out, moe_w_group, moe_b_group, moe_w_expert, moe_b_expert, moe_w_gate, moe_w_up, moe_w_down):
    batch, seq, d = x_prompt.shape
    dec_batch, dec_seq, _ = x_sample.shape
    depth = w_ada.shape[0]
    n_ctx = batch * seq
    streams = _Streams(batch, seq, dec_batch, dec_seq)

    cond = jnp.concatenate([c_ctx[None, :], c, jnp.zeros((MOD_ROWS - 1 - dec_batch, d), F32)], axis=0)
    mods = _modulation(cond, w_ada, b_ada).reshape(depth * MOD_ROWS * N_MOD, 1, d)
    mods = _Mods(mods, n_ctx, dec_seq)

    x = jnp.concatenate([x_prompt.reshape(n_ctx, d), _embed(x_sample).reshape(dec_batch * dec_seq, d)], axis=0)
    new_s5_re, new_s5_im, new_c, new_n, new_m = [], [], [], [], []
    for l in range(depth):
        j = l // N_MIXERS
        if l % N_MIXERS == 0:
            x, s_re, s_im = _s5_layer(
                x, mods, l, streams, norm1_g[l], s5_lambda_re[j], s5_lambda_im[j], s5_log_step[j],
                s5_b_re[j], s5_b_im[j], s5_c_re[j], s5_c_im[j], s5_d[j],
                s5_w_glu_a[j], s5_b_glu_a[j], s5_w_glu_b[j], s5_b_glu_b[j],
                state_s5_re[:, j], state_s5_im[:, j])
            new_s5_re.append(s_re)
            new_s5_im.append(s_im)
        else:
            x, c_f, n_f, m_f = _mlstm_layer(
                x, mods, l, streams, norm1_g[l], ml_w_in[j], ml_b_gates[j], ml_head_norm_g[j], ml_w_out[j],
                state_mlstm_C[:, j], state_mlstm_n[:, j], state_mlstm_m[:, j])
            new_c.append(c_f)
            new_n.append(n_f)
            new_m.append(m_f)
        x = _moe_layer(x, mods, l, norm2_g[l], moe_w_group[l], moe_b_group[l], moe_w_expert[l],
                       moe_b_expert[l], moe_w_gate, moe_w_up, moe_w_down)
    y_prompt = _final_norm(x, final_norm_g, 0, n_ctx).reshape(batch, seq, d)
    y_sample = _final_norm(x, final_norm_g, n_ctx, dec_batch * dec_seq).reshape(dec_batch, dec_seq, d)
    return (y_prompt, y_sample, jnp.stack(new_s5_re, axis=1), jnp.stack(new_s5_im, axis=1),
            jnp.stack(new_c, axis=1), jnp.stack(new_n, axis=1), jnp.stack(new_m, axis=1))
```

```python
import functools
import math

import jax
import jax.numpy as jnp
from jax import lax
from jax.experimental import pallas as pl
from jax.experimental.pallas import tpu as pltpu

F32 = jnp.float32
BF16 = jnp.bfloat16

S5_GROUP_CH = 16
ML_HEADS = 8
MOE_GROUPS = 4
MOE_EPG = 8
MOE_EXPERTS = MOE_GROUPS * MOE_EPG
GRID_W = 64
POS_BASE = 10000.0
RMS_EPS = 1e-6
NEG_BIG = -1e30
N_MIXERS = 2
N_MOD = 6

LANES = 128
SUBLANES = 8
VMEM_LIMIT_BYTES = 56 * 1024 * 1024
EXPERT_VMEM_LIMIT_BYTES = 60 * 1024 * 1024

MOD_ROWS = 16
S5_LANE_GROUPS = LANES // S5_GROUP_CH
S5_TC = 64
ML_T = 256
ML_HP = 2
ROUTE_LANES = LANES
MOE_TM = 256
ROW_TM = 256
ROW_ISSUE_UNROLL = 8


def _cparams(sem, vmem=VMEM_LIMIT_BYTES):
    return pltpu.CompilerParams(dimension_semantics=sem, vmem_limit_bytes=vmem)


def _silu(x):
    return x * jax.nn.sigmoid(x)


def _gelu_tanh(x):
    c = math.sqrt(2.0 / math.pi)
    return 0.5 * x * (1.0 + jnp.tanh(c * (x + 0.044715 * (x * x * x))))


def _ada_norm_tile(x, g, shift, scale):
    r = lax.rsqrt(jnp.mean(x * x, axis=-1, keepdims=True) + RMS_EPS)
    return (x * r * g) * (1.0 + scale) + shift


def _mod_kernel(c_ref, w_ref, b_ref, o_ref):
    s = _silu(c_ref[...])
    o_ref[...] = jnp.dot(s, w_ref[...], preferred_element_type=F32,
                         precision=lax.Precision.HIGHEST) + b_ref[...]


def _modulation(cond, w_ada, b_ada, tn=1024):
    depth, d, n = w_ada.shape
    return pl.pallas_call(
        _mod_kernel,
        out_shape=jax.ShapeDtypeStruct((depth, MOD_ROWS, n), F32),
        grid=(depth, n // tn),
        in_specs=[
            pl.BlockSpec((MOD_ROWS, d), lambda l, j: (0, 0)),
            pl.BlockSpec((None, d, tn), lambda l, j: (l, 0, j)),
            pl.BlockSpec((None, 1, tn), lambda l, j: (l, 0, j)),
        ],
        out_specs=pl.BlockSpec((None, MOD_ROWS, tn), lambda l, j: (l, 0, j)),
        compiler_params=_cparams(("parallel", "parallel")),
        name="modulation",
    )(cond, w_ada, b_ada.reshape(depth, 1, n))


class _Mods:
    def __init__(self, mods, n_ctx, dec_seq):
        self.mods = mods
        self.n_ctx = n_ctx
        self.dec_seq = dec_seq
        self.d = mods.shape[-1]

    def spec(self, layer, k, tm, row_offset=0):
        base = layer * MOD_ROWS * N_MOD + k
        n_ctx, dec_seq = self.n_ctx, self.dec_seq

        def index(i, *_):
            row0 = i * tm + row_offset
            r = jnp.where(row0 < n_ctx, 0, 1 + (row0 - n_ctx) // dec_seq)
            return (base + r * N_MOD, 0, 0)

        return pl.BlockSpec((None, 1, self.d), index)


def _embed_kernel(x_ref, p_ref, o_ref):
    o_ref[...] = x_ref[...] + p_ref[...]


def _grid_pos_embed(n_tokens, d):
    rows = n_tokens // GRID_W
    r, col = jnp.meshgrid(jnp.arange(rows, dtype=F32), jnp.arange(GRID_W, dtype=F32), indexing="ij")
    quarter = d // 4
    omega = 1.0 / (POS_BASE ** (jnp.arange(quarter, dtype=F32) / quarter))

    def axis_embed(pos):
        ang = pos.reshape(-1, 1) * omega[None, :]
        return jnp.concatenate([jnp.sin(ang), jnp.cos(ang)], axis=-1)

    return jnp.concatenate([axis_embed(r), axis_embed(col)], axis=-1)


def _embed(x_sample):
    b, l, d = x_sample.shape
    pos = _grid_pos_embed(l, d)
    return pl.pallas_call(
        _embed_kernel,
        out_shape=jax.ShapeDtypeStruct((b, l, d), F32),
        grid=(b,),
        in_specs=[pl.BlockSpec((None, l, d), lambda i: (i, 0, 0)),
                  pl.BlockSpec((l, d), lambda i: (0, 0))],
        out_specs=pl.BlockSpec((None, l, d), lambda i: (i, 0, 0)),
        compiler_params=_cparams(("parallel",)),
        name="pos_embed",
    )(x_sample, pos)


def _adanorm_kernel(x_ref, g_ref, sh_ref, sc_ref, o_ref):
    o_ref[...] = _ada_norm_tile(x_ref[...], g_ref[...], sh_ref[...], sc_ref[...]).astype(o_ref.dtype)


def _adanorm(x, g, mods, layer, k_shift, k_scale, tm=512):
    n, d = x.shape
    return pl.pallas_call(
        _adanorm_kernel,
        out_shape=jax.ShapeDtypeStruct((n, d), F32),
        grid=(n // tm,),
        in_specs=[pl.BlockSpec((tm, d), lambda i: (i, 0)),
                  pl.BlockSpec((1, d), lambda i: (0, 0)),
                  mods.spec(layer, k_shift, tm),
                  mods.spec(layer, k_scale, tm)],
        out_specs=pl.BlockSpec((tm, d), lambda i: (i, 0)),
        compiler_params=_cparams(("parallel",)),
        name="adanorm",
    )(x, g.reshape(1, d), mods.mods, mods.mods)


def _final_norm_kernel(x_ref, g_ref, o_ref):
    x = x_ref[...]
    r = lax.rsqrt(jnp.mean(x * x, axis=-1, keepdims=True) + RMS_EPS)
    o_ref[...] = x * r * g_ref[...]


def _final_norm(x, g, row0, nrows, tm=512):
    d = x.shape[1]
    off = row0 // tm
    return pl.pallas_call(
        _final_norm_kernel,
        out_shape=jax.ShapeDtypeStruct((nrows, d), F32),
        grid=(nrows // tm,),
        in_specs=[pl.BlockSpec((tm, d), lambda i: (i + off, 0)),
                  pl.BlockSpec((1, d), lambda i: (0, 0))],
        out_specs=pl.BlockSpec((tm, d), lambda i: (i, 0)),
        compiler_params=_cparams(("parallel",)),
        name="final_norm",
    )(x, g.reshape(1, d))


def _cmul(ar, ai, br, bi):
    return ar * br - ai * bi, ar * bi + ai * br


def _s5_disc_kernel(lre_ref, lim_ref, ls_ref, bre_ref, bim_ref, cre_ref, cim_ref, *out_refs):
    lr = lre_ref[...]
    li = lim_ref[...]
    dt = jnp.exp(ls_ref[...])
    mag = jnp.exp(lr * dt)
    ar = mag * jnp.cos(li * dt)
    ai = mag * jnp.sin(li * dt)
    den = lr * lr + li * li
    zr = ((ar - 1.0) * lr + ai * li) / den
    zi = (ai * lr - (ar - 1.0) * li) / den
    a2 = _cmul(ar, ai, ar, ai)
    bb = _cmul(zr[None], zi[None], bre_ref[...], bim_ref[...])
    abb = _cmul(ar[None], ai[None], *bb)
    ca = _cmul(cre_ref[...], cim_ref[...], ar[None], ai[None])
    ca2 = _cmul(cre_ref[...], cim_ref[...], a2[0][None], a2[1][None])
    for ref, val in zip(out_refs, (*a2, *bb, *abb, *ca, *ca2)):
        ref[...] = val


def _s5_discretize(lam_re, lam_im, log_step, b_re, b_im, c_re, c_im):
    _, g, p = lam_re.shape
    h = b_re.shape[-1]
    vec = pl.BlockSpec((None, g, p), lambda i: (i, 0, 0))
    mat = pl.BlockSpec((None, h, g, p), lambda i: (i, 0, 0, 0))
    vec_t = jax.ShapeDtypeStruct((2, g, p), F32)
    mat_t = jax.ShapeDtypeStruct((2, h, g, p), F32)
    return pl.pallas_call(
        _s5_disc_kernel,
        out_shape=(vec_t,) * 2 + (mat_t,) * 8,
        grid=(2,),
        in_specs=[vec, vec, pl.BlockSpec((None, g, 1), lambda i: (i, 0, 0)), mat, mat, mat, mat],
        out_specs=(vec,) * 2 + (mat,) * 8,
        compiler_params=_cparams(("parallel",)),
        name="s5_discretize",
    )(lam_re, lam_im, log_step.reshape(2, g, 1),
      jnp.transpose(b_re, (0, 3, 1, 2)), jnp.transpose(b_im, (0, 3, 1, 2)),
      jnp.transpose(c_re, (0, 2, 1, 3)), jnp.transpose(c_im, (0, 2, 1, 3)))


def _s5_feed_kernel(b_ref, ab_ref, c_ref, cb_ref, cab_ref):
    hi = lax.Precision.HIGHEST
    cb_ref[...] = jnp.dot(b_ref[...], c_ref[...], preferred_element_type=F32, precision=hi)
    cab_ref[...] = jnp.dot(ab_ref[...], c_ref[...], preferred_element_type=F32, precision=hi)


def _s5_feedthrough(bmat, abmat, cmat):
    _, nlc, lanes, s2 = bmat.shape
    bspec = pl.BlockSpec((None, None, lanes, s2), lambda d, j: (d, j, 0, 0))
    cspec = pl.BlockSpec((None, None, s2, lanes), lambda d, j: (d, j, 0, 0))
    ospec = pl.BlockSpec((None, None, lanes, lanes), lambda d, j: (d, j, 0, 0))
    otype = jax.ShapeDtypeStruct((2, nlc, lanes, lanes), F32)
    return pl.pallas_call(
        _s5_feed_kernel,
        out_shape=(otype, otype),
        grid=(2, nlc),
        in_specs=[bspec, bspec, cspec],
        out_specs=(ospec, ospec),
        compiler_params=_cparams(("parallel", "parallel")),
        name="s5_feedthrough",
    )(bmat, abmat, cmat)


def _s5_operands(disc, c_re, c_im):
    a2_re, a2_im, bb_re, bb_im, abb_re, abb_im, ca_re, ca_im, ca2_re, ca2_im = disc
    _, g, p = a2_re.shape
    h = bb_re.shape[1]
    gl = S5_LANE_GROUPS
    nlc = g // gl
    eye = jnp.eye(gl, dtype=F32)

    def bblock(re, im):
        def one(bb):
            t = jnp.transpose(bb, (0, 2, 1, 3)).reshape(2, nlc, gl, h, p)
            return (t[:, :, :, :, None, :] * eye[None, None, :, None, :, None]).reshape(2, nlc, gl * h, gl * p)
        return jnp.concatenate([one(re), one(im)], axis=-1)

    def cblock(re, im):
        def one(c):
            t = jnp.transpose(c, (0, 2, 3, 1)).reshape(2, nlc, gl, p, h)
            return (t[:, :, :, :, None, :] * eye[None, None, :, None, :, None]).reshape(2, nlc, gl * p, gl * h)
        return jnp.concatenate([one(re), one(-im)], axis=-2)

    bmat = bblock(bb_re, bb_im)
    abmat = bblock(abb_re, abb_im)
    cmat = cblock(jnp.transpose(c_re, (0, 2, 1, 3)), jnp.transpose(c_im, (0, 2, 1, 3)))
    cb, cab = _s5_feedthrough(bmat, abmat, cmat)
    b2 = jnp.concatenate([abmat, bmat], axis=-2).astype(BF16)
    c2 = jnp.concatenate([cblock(ca_re, ca_im), cblock(ca2_re, ca2_im)], axis=-1).astype(BF16)
    d2 = jnp.concatenate([jnp.concatenate([cb, cab], axis=-1),
                          jnp.concatenate([jnp.zeros_like(cb), cb], axis=-1)], axis=-2).astype(BF16)
    a2 = jnp.concatenate([a2_re.reshape(2, nlc, 1, gl * p), a2_im.reshape(2, nlc, 1, gl * p)], axis=-1)
    return b2, c2, d2, a2


def _s5_state_to_lanes(s_re, s_im):
    b, _, g, p = s_re.shape
    gl = S5_LANE_GROUPS
    nlc = g // gl

    def lay(s):
        return jnp.transpose(s.reshape(b, 2, nlc, gl * p), (1, 2, 0, 3))

    return jnp.concatenate([lay(s_re), lay(s_im)], axis=-1)


def _s5_state_from_lanes(s, g, p):
    _, nlc, b, s2 = s.shape
    half = s2 // 2

    def unlay(t):
        return jnp.transpose(t, (2, 0, 1, 3)).reshape(b, 2, g, p)

    return unlay(s[..., :half]), unlay(s[..., half:])


def _s5_scan_kernel(*refs, seq, tc, aliased):
    if aliased:
        h_ref, b2_ref, c2_ref, d2_ref, a_ref, d_ref, s0_ref, _, z_ref, sf_ref = refs[:10]
    else:
        h_ref, b2_ref, c2_ref, d2_ref, a_ref, d_ref, s0_ref, z_ref, sf_ref = refs[:9]
    scr = (refs[-10:-5], refs[-5:])
    nb = SUBLANES
    half = a_ref.shape[-1] // 2
    nc = seq // tc
    npair = tc // 2
    a_re = [jnp.broadcast_to(a_ref[dr][:, :half], (nb, half)) for dr in range(2)]
    a_im = [jnp.broadcast_to(a_ref[dr][:, half:], (nb, half)) for dr in range(2)]

    def chunk(c, carry):
        t0s = (c * tc, (nc - 1 - c) * tc)

        def pair_rows(dr, p):
            if dr == 0:
                first = t0s[0] + 2 * p
                return pl.ds(first, nb, stride=seq), pl.ds(first + 1, nb, stride=seq)
            first = t0s[1] + tc - 1 - 2 * p
            return pl.ds(first, nb, stride=seq), pl.ds(first - 1, nb, stride=seq)

        carry = list(carry)
        for dr in range(2):
            u_scr, bu_scr, _, _, _ = scr[dr]
            for p in range(npair):
                r1, r2 = pair_rows(dr, p)
                u_scr[p * nb:(p + 1) * nb, :LANES] = h_ref[r1, :]
                u_scr[p * nb:(p + 1) * nb, LANES:] = h_ref[r2, :]
            bu_scr[...] = jnp.dot(u_scr[...].astype(BF16), b2_ref[dr], preferred_element_type=F32)
        for dr in range(2):
            _, bu_scr, x_scr, _, _ = scr[dr]
            xr, xi = carry[2 * dr], carry[2 * dr + 1]
            for p in range(npair):
                x_scr[p * nb:(p + 1) * nb, :half] = xr
                x_scr[p * nb:(p + 1) * nb, half:] = xi
                bu = bu_scr[p * nb:(p + 1) * nb, :]
                xr, xi = (a_re[dr] * xr - a_im[dr] * xi + bu[:, :half],
                          a_re[dr] * xi + a_im[dr] * xr + bu[:, half:])
            carry[2 * dr], carry[2 * dr + 1] = xr, xi
        for dr in range(2):
            u_scr, _, x_scr, yo_scr, y_scr = scr[dr]
            yo_scr[...] = jnp.dot(x_scr[...].astype(BF16), c2_ref[dr], preferred_element_type=F32) \
                + jnp.dot(u_scr[...].astype(BF16), d2_ref[dr], preferred_element_type=F32)
            for p in range(npair):
                r1, r2 = pair_rows(dr, p)
                y_scr[r1, :] = yo_scr[p * nb:(p + 1) * nb, :LANES]
                y_scr[r2, :] = yo_scr[p * nb:(p + 1) * nb, LANES:]
        return tuple(carry)

    s0f = s0_ref[0]
    s0b = s0_ref[1]
    init = (s0f[:, :half], s0f[:, half:], s0b[:, :half], s0b[:, half:])
    xr_f, xi_f, xr_b, xi_b = lax.fori_loop(0, nc, chunk, init)
    sf_ref[0, :, :half] = xr_f
    sf_ref[0, :, half:] = xi_f
    sf_ref[1, :, :half] = xr_b
    sf_ref[1, :, half:] = xi_b
    y = h_ref[...] * d_ref[...] + scr[0][4][...] + scr[1][4][...]
    z_ref[...] = _gelu_tanh(y).astype(z_ref.dtype)


def _s5_scan(h, operands, dvec, s0, row0, nseq, seq, z_prev=None):
    b2, c2, d2, avec = operands
    n, d = h.shape
    nlc = d // LANES
    s2 = avec.shape[-1]
    nb = SUBLANES
    rows = nb * seq
    off = row0 // rows
    aliased = z_prev is not None
    in_specs = [
        pl.BlockSpec((rows, LANES), lambda i, j: (i + off, j)),
        pl.BlockSpec((2, None, 2 * LANES, s2), lambda i, j: (0, j, 0, 0)),
        pl.BlockSpec((2, None, s2, 2 * LANES), lambda i, j: (0, j, 0, 0)),
        pl.BlockSpec((2, None, 2 * LANES, 2 * LANES), lambda i, j: (0, j, 0, 0)),
        pl.BlockSpec((2, None, 1, s2), lambda i, j: (0, j, 0, 0)),
        pl.BlockSpec((1, LANES), lambda i, j: (0, j)),
        pl.BlockSpec((2, None, nb, s2), lambda i, j: (0, j, i, 0)),
    ]
    args = [h, b2, c2, d2, avec, dvec, s0]
    io_alias = {}
    if aliased:
        in_specs.append(pl.BlockSpec(memory_space=pl.ANY))
        args.append(z_prev)
        io_alias = {len(args) - 1: 0}
    npair_rows = S5_TC // 2 * nb
    return pl.pallas_call(
        functools.partial(_s5_scan_kernel, seq=seq, tc=S5_TC, aliased=aliased),
        out_shape=(jax.ShapeDtypeStruct((n, d), BF16), jax.ShapeDtypeStruct((2, nlc, nseq, s2), F32)),
        grid=(nseq // nb, nlc),
        in_specs=in_specs,
        out_specs=(pl.BlockSpec((rows, LANES), lambda i, j: (i + off, j)),
                   pl.BlockSpec((2, None, nb, s2), lambda i, j: (0, j, i, 0))),
        scratch_shapes=2 * [pltpu.VMEM((npair_rows, 2 * LANES), F32),
                            pltpu.VMEM((npair_rows, s2), F32),
                            pltpu.VMEM((npair_rows, s2), F32),
                            pltpu.VMEM((npair_rows, 2 * LANES), F32),
                            pltpu.VMEM((rows, LANES), F32)],
        input_output_aliases=io_alias,
        compiler_params=_cparams(("parallel", "parallel")),
        name="s5_scan",
    )(*args)


def _mm_res_kernel(*refs, n_w, has_bias):
    z_ref = refs[0]
    w_refs = refs[1:1 + n_w]
    pos = 1 + n_w
    b_refs = refs[pos:pos + n_w] if has_bias else ()
    pos += n_w if has_bias else 0
    x_ref, gate_ref, o_ref = refs[pos:pos + 3]
    z = z_ref[...]
    acc = [jnp.dot(z, w[...], preferred_element_type=F32) for w in w_refs]
    if has_bias:
        acc = [a + b[...] for a, b in zip(acc, b_refs)]
    y = acc[0] if n_w == 1 else acc[0] * jax.nn.sigmoid(acc[1])
    o_ref[...] = x_ref[...] + gate_ref[...] * y


def _mm_residual(z, ws, bs, x, mods, layer, k_gate, tm=1024, tn=512):
    n, k = z.shape
    n_out = ws[0].shape[1]
    n_w = len(ws)
    has_bias = bs is not None
    in_specs = [pl.BlockSpec((tm, k), lambda i, j: (i, 0))]
    in_specs += [pl.BlockSpec((k, tn), lambda i, j: (0, j)) for _ in ws]
    args = [z, *ws]
    if has_bias:
        in_specs += [pl.BlockSpec((1, tn), lambda i, j: (0, j)) for _ in bs]
        args += [b.reshape(1, n_out) for b in bs]
    gate_spec = mods.spec(layer, k_gate, tm)
    gate_spec = pl.BlockSpec((None, 1, tn), lambda i, j, f=gate_spec.index_map: (f(i)[0], 0, j))
    in_specs += [pl.BlockSpec((tm, tn), lambda i, j: (i, j)), gate_spec]
    args += [x, mods.mods]
    return pl.pallas_call(
        functools.partial(_mm_res_kernel, n_w=n_w, has_bias=has_bias),
        out_shape=jax.ShapeDtypeStruct((n, n_out), F32),
        grid=(n // tm, n_out // tn),
        in_specs=in_specs,
        out_specs=pl.BlockSpec((tm, tn), lambda i, j: (i, j)),
        compiler_params=_cparams(("parallel", "parallel")),
        name=f"proj_residual_{n_w}w",
    )(*args)


def _split_bf16(a):
    hi = a.astype(BF16)
    return hi, (a - hi.astype(F32)).astype(BF16)


def _dot_3pass(a, b):
    a_hi, a_lo = _split_bf16(a)
    b_hi, b_lo = _split_bf16(b)
    return (jnp.dot(a_hi, b_hi, preferred_element_type=F32) + jnp.dot(a_hi, b_lo, preferred_element_type=F32)
            + jnp.dot(a_lo, b_hi, preferred_element_type=F32))


def _mm_norm_kernel(x_ref, g_ref, sh_ref, sc_ref, w_ref, o_ref, h_scr, *, split):
    @pl.when(pl.program_id(1) == 0)
    def _():
        h_scr[...] = _ada_norm_tile(x_ref[...], g_ref[...], sh_ref[...], sc_ref[...]).astype(h_scr.dtype)

    if split:
        o_ref[...] = _dot_3pass(h_scr[...], w_ref[...]).astype(o_ref.dtype)
    else:
        o_ref[...] = jnp.dot(h_scr[...], w_ref[...], preferred_element_type=F32).astype(o_ref.dtype)


def _mm_adanorm(x, g, mods, layer, k_shift, k_scale, w, out_dtype, tm=1024, tn=1024):
    n, d = x.shape
    n_out = w.shape[1]
    tn = min(tn, n_out)
    exact = w.dtype == F32
    return pl.pallas_call(
        functools.partial(_mm_norm_kernel, split=exact),
        out_shape=jax.ShapeDtypeStruct((n, n_out), out_dtype),
        grid=(n // tm, n_out // tn),
        in_specs=[pl.BlockSpec((tm, d), lambda i, j: (i, 0)),
                  pl.BlockSpec((1, d), lambda i, j: (0, 0)),
                  mods.spec(layer, k_shift, tm),
                  mods.spec(layer, k_scale, tm),
                  pl.BlockSpec((d, tn), lambda i, j: (0, j))],
        out_specs=pl.BlockSpec((tm, tn), lambda i, j: (i, j)),
        scratch_shapes=[pltpu.VMEM((tm, d), F32 if exact else BF16)],
        compiler_params=_cparams(("parallel", "arbitrary")),
        name="adanorm_proj",
    )(x, g.reshape(1, d), mods.mods, mods.mods, w)


_G_FWD = ML_HEADS
_G_BWD = 3 * ML_HEADS


def _gate_kernel(xc_ref, xr_ref, bc_ref, br_ref, g_ref, w_ref, e_ref, dr_ref, wr_ref):
    t = xc_ref.shape[0]
    r_i = lax.broadcasted_iota(jnp.int32, (t, t), 0)
    c_i = lax.broadcasted_iota(jnp.int32, (t, t), 1)
    lower = (c_i <= r_i).astype(F32)
    upper = (c_i >= r_i).astype(F32)
    hi = lax.Precision.HIGHEST
    xc = xc_ref[...] + bc_ref[...]
    fc = jax.nn.log_sigmoid(xc)
    lane = lax.broadcasted_iota(jnp.int32, xc.shape, 1)
    g_c = jnp.where(lane < 2 * ML_HEADS,
                    jnp.dot(lower, fc, preferred_element_type=F32, precision=hi),
                    jnp.dot(upper, fc, preferred_element_type=F32, precision=hi))
    e_c = jnp.broadcast_to(jnp.sum(fc, axis=0, keepdims=True), xc.shape)
    g_ref[...] = g_c
    e_ref[...] = e_c
    w_ref[...] = e_c - g_c + pltpu.roll(xc, ML_HEADS, 1)
    xr = xr_ref[...] + br_ref[...]
    fr = jax.nn.log_sigmoid(xr)
    row = lax.broadcasted_iota(jnp.int32, xr.shape, 0)
    g_r = jnp.where(row < 2 * ML_HEADS,
                    jnp.dot(fr, upper, preferred_element_type=F32, precision=hi),
                    jnp.dot(fr, lower, preferred_element_type=F32, precision=hi))
    i_r = pltpu.roll(xr, ML_HEADS, 0)
    dr_ref[...] = i_r - g_r
    wr_ref[...] = jnp.sum(fr, axis=1, keepdims=True) - g_r + i_r


def _gate_prep(gates, b_gates, tch):
    n = gates.shape[0]
    ng = 4 * ML_HEADS
    bias_c = jnp.pad(b_gates, (0, LANES - ng)).reshape(1, LANES)
    bias_r = b_gates.reshape(ng, 1)
    gates_r = jnp.transpose(gates[:, :ng])
    col = pl.BlockSpec((tch, LANES), lambda i: (i, 0))
    rowb = pl.BlockSpec((ng, tch), lambda i: (0, i))
    return pl.pallas_call(
        _gate_kernel,
        out_shape=(jax.ShapeDtypeStruct((n, LANES), F32),) * 3 + (jax.ShapeDtypeStruct((ng, n), F32),) * 2,
        grid=(n // tch,),
        in_specs=[col, rowb, pl.BlockSpec((1, LANES), lambda i: (0, 0)),
                  pl.BlockSpec((ng, 1), lambda i: (0, 0))],
        out_specs=(col, col, col, rowb, rowb),
        compiler_params=_cparams(("parallel",)),
        name="mlstm_gates",
    )(gates, gates_r, bias_c, bias_r)


def _mlstm_kernel(q_ref, kt_ref, v_ref, o_ref, gcol_ref, grow_ref, hg_ref,
                  c0_ref, n0_ref, m0_ref, hn_ref, cf_ref, nf_ref, mf_ref, *scratch, seq, tch, hp):
    nc = seq // tch
    dv = v_ref.shape[-1] // hp
    dk = q_ref.shape[-1] // hp
    tt = lax.broadcasted_iota(jnp.int32, (tch, tch), 0)
    ss = lax.broadcasted_iota(jnp.int32, (tch, tch), 1)
    ones_col = (lax.broadcasted_iota(jnp.int32, (tch, LANES), 1) == 0).astype(BF16)
    scr = {(hd, dr): scratch[3 * (2 * hd + dr):3 * (2 * hd + dr) + 3] for hd in range(hp) for dr in range(2)}
    for (hd, dr), (_, c_scr, vx) in scr.items():
        c_scr[:, :dv] = c0_ref[dr, hd]
        c_scr[:, dv:] = n0_ref[hd, dr]
        vx[:, dv:] = ones_col

    def chunk_dir(hd, dr, r0, m):
        hacc, c_scr, vx = scr[hd, dr]
        rows = pl.ds(pl.multiple_of(r0, tch), tch)
        q = q_ref[rows, hd * dk:(hd + 1) * dk]
        kt = kt_ref[hd * dk:(hd + 1) * dk, rows]
        vx[:, :dv] = v_ref[rows, hd * dv:(hd + 1) * dv]
        gc = gcol_ref[hd, rows, :]
        gr = grow_ref[hd, :, rows]
        g_col = gc[:, 3 * dr:3 * dr + 1]
        w_col = gc[:, 3 * dr + 1:3 * dr + 2]
        e_col = gc[:, 3 * dr + 2:3 * dr + 3]
        d_row = gr[2 * dr:2 * dr + 1, :]
        w_row = gr[2 * dr + 1:2 * dr + 2, :]
        mask = (ss <= tt) if dr == 0 else (ss >= tt)
        a_col = g_col + m
        dmat = jnp.where(mask, g_col + d_row, -jnp.inf)
        mt = jnp.maximum(a_col, jnp.max(dmat, axis=1, keepdims=True))
        qk = jnp.dot(q, kt, preferred_element_type=F32)
        s = (qk * jnp.exp(dmat - mt)).astype(BF16)
        inter = jnp.exp(a_col - mt)
        vext = vx[...]
        tot = jnp.dot(s, vext, preferred_element_type=F32) \
            + inter * jnp.dot(q, c_scr[...].astype(BF16), preferred_element_type=F32)
        den = tot[:, dv:dv + 1]
        hacc[rows, :] = tot[:, :dv] / jnp.maximum(jnp.abs(den), jnp.exp(-mt))
        g_end = jnp.max(e_col, axis=0, keepdims=True)
        m_new = jnp.maximum(g_end + m, jnp.max(w_col, axis=0, keepdims=True))
        decay = jnp.exp(g_end + m - m_new)
        kw = (kt.astype(F32) * jnp.exp(w_row - m_new)).astype(BF16)
        c_scr[...] = decay * c_scr[...] + jnp.dot(kw, vext, preferred_element_type=F32)
        return m_new

    keys = list(scr)

    def body(c, carry):
        return tuple(chunk_dir(hd, dr, (c if dr == 0 else nc - 1 - c) * tch, m)
                     for (hd, dr), m in zip(keys, carry))

    m_fin = lax.fori_loop(0, nc, body, tuple(m0_ref[hd, dr:dr + 1, 0:1] for hd, dr in keys))
    for (hd, dr), m in zip(keys, m_fin):
        c_scr = scr[hd, dr][1]
        cf_ref[dr, hd] = c_scr[:, :dv]
        nf_ref[hd, dr] = c_scr[:, dv:]
        mf_ref[hd, dr:dr + 1, :] = jnp.broadcast_to(m, (1, LANES))
    for hd in range(hp):
        cols = slice(hd * dv, (hd + 1) * dv)
        hs = scr[hd, 0][0][...] + scr[hd, 1][0][...]
        hn = hs * lax.rsqrt(jnp.mean(hs * hs, axis=-1, keepdims=True) + RMS_EPS)
        hn = hn * hg_ref[:, cols] * jax.nn.sigmoid(o_ref[:, cols].astype(F32))
        hn_ref[:, cols] = hn.astype(hn_ref.dtype)


def _mlstm(proj, k_t, gcol, grow, head_g, c0, n0, m0, row0, nseq, seq, hn_prev=None, hp=ML_HP):
    n = proj.shape[0]
    h = ML_HEADS
    dk = c0.shape[-2]
    dv = c0.shape[-1]
    d = h * dv
    qk = h * dk
    off = row0 // seq
    tch = min(ML_T, seq)
    aliased = hn_prev is not None
    c_spec = pl.BlockSpec((None, 2, hp, dk, dv), lambda b, j: (b, 0, j, 0, 0))
    n_spec = pl.BlockSpec((None, hp, 2, dk, LANES), lambda b, j: (b, j, 0, 0, 0))
    m_spec = pl.BlockSpec((None, hp, 2, LANES), lambda b, j: (b, j, 0, 0))
    wk, wv = hp * dk, hp * dv
    in_specs = [
        pl.BlockSpec((seq, wk), lambda b, j: (b + off, j)),
        pl.BlockSpec((wk, seq), lambda b, j: (j, b + off)),
        pl.BlockSpec((seq, wv), lambda b, j: (b + off, 2 * qk // wv + j)),
        pl.BlockSpec((seq, wv), lambda b, j: (b + off, (2 * qk + d) // wv + j)),
        pl.BlockSpec((None, hp, seq, 8), lambda b, j: (b, j, 0, 0)),
        pl.BlockSpec((None, hp, 4, seq), lambda b, j: (b, j, 0, 0)),
        pl.BlockSpec((1, wv), lambda b, j: (0, j)),
        c_spec, n_spec, m_spec,
    ]
    args = [proj, k_t, proj, proj, gcol, grow, head_g.reshape(1, d), c0, n0, m0]
    n_in = len(args)
    io_alias = {}
    if aliased:
        in_specs.append(pl.BlockSpec(memory_space=pl.ANY))
        args.append(hn_prev)
        io_alias = {n_in: 0}

    def kern(*refs):
        if aliased:
            refs = refs[:n_in] + refs[n_in + 1:]
        _mlstm_kernel(*refs, seq=seq, tch=tch, hp=hp)

    return pl.pallas_call(
        kern,
        out_shape=(jax.ShapeDtypeStruct((n, d), BF16),
                   jax.ShapeDtypeStruct((nseq, 2, h, dk, dv), F32),
                   jax.ShapeDtypeStruct((nseq, h, 2, dk, LANES), F32),
                   jax.ShapeDtypeStruct((nseq, h, 2, LANES), F32)),
        grid=(nseq, h // hp),
        in_specs=in_specs,
        out_specs=(pl.BlockSpec((seq, wv), lambda b, j: (b + off, j)), c_spec, n_spec, m_spec),
        scratch_shapes=2 * hp * [pltpu.VMEM((seq, dv), F32), pltpu.VMEM((dk, dv + LANES), F32),
                                 pltpu.VMEM((tch, dv + LANES), BF16)],
        input_output_aliases=io_alias,
        compiler_params=_cparams(("parallel", "parallel")),
        name="mlstm",
    )(*args)


_R_EID, _R_W, _R_RANK = 0, 2, 4
_R_LOGIT0 = MOE_GROUPS


def _router_kernel(x_ref, g_ref, sh_ref, sc_ref, whi_ref, wlo_ref, b_ref, h_ref, route_ref, cnt_ref, carry):
    i = pl.program_id(0)

    @pl.when(i == 0)
    def _():
        carry[...] = jnp.zeros_like(carry)

    h = _ada_norm_tile(x_ref[...], g_ref[...], sh_ref[...], sc_ref[...])
    h_ref[...] = h
    tm = h.shape[0]
    h_hi = h.astype(BF16)
    h_lo = (h - h_hi.astype(F32)).astype(BF16)
    logits = (jnp.dot(h_hi, whi_ref[...], preferred_element_type=F32)
              + jnp.dot(h_hi, wlo_ref[...], preferred_element_type=F32)
              + jnp.dot(h_lo, whi_ref[...], preferred_element_type=F32)) + b_ref[...]
    lane = lax.broadcasted_iota(jnp.int32, logits.shape, 1)
    big = jnp.int32(ROUTE_LANES)

    def first_lane(cond):
        return jnp.min(jnp.where(cond, lane, big), axis=1, keepdims=True)

    glog = jnp.where(lane < MOE_GROUPS, logits, -jnp.inf)
    ge = jnp.exp(glog - jnp.max(glog, axis=1, keepdims=True))
    pgrp = ge / jnp.sum(ge, axis=1, keepdims=True)
    pg = jnp.max(pgrp, axis=1, keepdims=True)
    grp = first_lane(pgrp == pg)
    e_lane = lane - _R_LOGIT0
    in_grp = (e_lane >= 0) & (e_lane < MOE_EXPERTS) & ((e_lane // MOE_EPG) == grp)
    elog = jnp.where(in_grp, logits, -jnp.inf)
    ee = jnp.exp(elog - jnp.max(elog, axis=1, keepdims=True))
    pe = jnp.where(in_grp, ee / jnp.sum(ee, axis=1, keepdims=True), -1.0)
    p0 = jnp.max(pe, axis=1, keepdims=True)
    l0 = first_lane(pe == p0)
    pe1 = jnp.where(lane == l0, -1.0, pe)
    p1 = jnp.max(pe1, axis=1, keepdims=True)
    l1 = first_lane(pe1 == p1)
    psum = p0 + p1
    w0 = pg * p0 / psum
    w1 = pg * p1 / psum
    onehot = ((lane == l0) | (lane == l1)).astype(BF16)
    r_i = lax.broadcasted_iota(jnp.int32, (tm, tm), 0)
    c_i = lax.broadcasted_iota(jnp.int32, (tm, tm), 1)
    tri = (c_i < r_i).astype(BF16)
    before = jnp.dot(tri, onehot, preferred_element_type=F32) + carry[...]
    rank0 = jnp.sum(jnp.where(lane == l0, before, 0.0), axis=1, keepdims=True)
    rank1 = jnp.sum(jnp.where(lane == l1, before, 0.0), axis=1, keepdims=True)
    carry[...] = carry[...] + jnp.sum(onehot.astype(F32), axis=0, keepdims=True)
    cnt_ref[...] = carry[...]
    cols = [(l0 - _R_LOGIT0).astype(F32), (l1 - _R_LOGIT0).astype(F32), w0, w1, rank0, rank1]
    route = jnp.zeros(logits.shape, F32)
    for c, val in enumerate(cols):
        route = jnp.where(lane == c, val, route)
    route_ref[...] = route


def _router(x, g, mods, layer, w_route, b_route, tm=512):
    n, d = x.shape
    w_hi = w_route.astype(BF16)
    w_lo = (w_route - w_hi.astype(F32)).astype(BF16)
    return pl.pallas_call(
        _router_kernel,
        out_shape=(jax.ShapeDtypeStruct((n, d), F32),
                   jax.ShapeDtypeStruct((n, ROUTE_LANES), F32),
                   jax.ShapeDtypeStruct((1, ROUTE_LANES), F32)),
        grid=(n // tm,),
        in_specs=[pl.BlockSpec((tm, d), lambda i: (i, 0)),
                  pl.BlockSpec((1, d), lambda i: (0, 0)),
                  mods.spec(layer, 3, tm),
                  mods.spec(layer, 4, tm),
                  pl.BlockSpec((d, ROUTE_LANES), lambda i: (0, 0)),
                  pl.BlockSpec((d, ROUTE_LANES), lambda i: (0, 0)),
                  pl.BlockSpec((1, ROUTE_LANES), lambda i: (0, 0))],
        out_specs=(pl.BlockSpec((tm, d), lambda i: (i, 0)),
                   pl.BlockSpec((tm, ROUTE_LANES), lambda i: (i, 0)),
                   pl.BlockSpec((1, ROUTE_LANES), lambda i: (0, 0))),
        scratch_shapes=[pltpu.VMEM((1, ROUTE_LANES), F32)],
        compiler_params=_cparams(("arbitrary",)),
        name="moe_router",
    )(x, g.reshape(1, d), mods.mods, mods.mods, w_hi, w_lo, b_route)


def _dispatch_kernel(pcnt_ref, pend_ref, dest_ref, h_ref, xs_ref, zbuf, sem, zsem):
    tm = h_ref.shape[0]
    zrows = zbuf.shape[0]

    @pl.when(pl.program_id(0) == 0)
    def _():
        zbuf[...] = jnp.zeros_like(zbuf)

        def clear(e, _):
            @pl.when(pcnt_ref[e] > 0)
            def _():
                start = pl.multiple_of(pend_ref[e] - zrows, zrows)
                cp = pltpu.make_async_copy(zbuf, xs_ref.at[pl.ds(start, zrows)], zsem)
                cp.start()
                cp.wait()
            return 0

        lax.fori_loop(0, MOE_EXPERTS, clear, 0)

    def issue(r, _):
        for k in range(2):
            d = dest_ref[0, 0, 2 * r + k]
            pltpu.make_async_copy(h_ref.at[pl.ds(r, 1)], xs_ref.at[pl.ds(d, 1)], sem).start(priority=k)
        return 0

    lax.fori_loop(0, tm, issue, 0, unroll=ROW_ISSUE_UNROLL)
    for _ in range(2):
        pltpu.make_async_copy(h_ref, xs_ref.at[pl.ds(0, tm)], sem).wait()


def _dispatch(h, dest, pcnt, pends, n_pad, tm=ROW_TM):
    n, d = h.shape
    nblk = n // tm
    grid_spec = pltpu.PrefetchScalarGridSpec(
        num_scalar_prefetch=2,
        grid=(nblk,),
        in_specs=[pl.BlockSpec((1, 1, 2 * tm), lambda i, *_: (i, 0, 0), memory_space=pltpu.SMEM),
                  pl.BlockSpec((tm, d), lambda i, *_: (i, 0))],
        out_specs=pl.BlockSpec(memory_space=pl.ANY),
        scratch_shapes=[pltpu.VMEM((MOE_TM, d), F32), pltpu.SemaphoreType.DMA(()),
                        pltpu.SemaphoreType.DMA(())],
    )
    return pl.pallas_call(
        _dispatch_kernel,
        out_shape=jax.ShapeDtypeStruct((n_pad, d), F32),
        grid_spec=grid_spec,
        compiler_params=_cparams(("arbitrary",)),
        name="moe_dispatch",
    )(pcnt, pends, dest.reshape(nblk, 1, 2 * tm), h)


def _experts_kernel(be_ref, first_ref, nxt_ref, nu_ref, x_ref, wg_hbm, wu_hbm, wd_hbm, o_ref,
                    stg_g, stg_u, stg_d, wg_bf, wu_bf, wd_bf, sem, *, layer, cast_rows):
    i = pl.program_id(0)
    active = i < nu_ref[0]
    stages = ((wg_hbm, stg_g, wg_bf), (wu_hbm, stg_u, wu_bf), (wd_hbm, stg_d, wd_bf))

    def weight_copies(e):
        return [pltpu.make_async_copy(hbm.at[layer, e], stg, sem.at[k])
                for k, (hbm, stg, _) in enumerate(stages)]

    @pl.when(i == 0)
    def _():
        for cp in weight_copies(be_ref[0]):
            cp.start()

    @pl.when(active & (first_ref[i] == 1))
    def _():
        for cp in weight_copies(be_ref[i]):
            cp.wait()
        for _, stg, wbf in stages:
            def cast(r, _, stg=stg, wbf=wbf):
                rows = pl.ds(pl.multiple_of(r * cast_rows, cast_rows), cast_rows)
                wbf[rows, :] = stg[rows, :].astype(BF16)
                return 0

            lax.fori_loop(0, stg.shape[0] // cast_rows, cast, 0)

        @pl.when(nxt_ref[i] >= 0)
        def _():
            for cp in weight_copies(nxt_ref[i]):
                cp.start()

    @pl.when(active)
    def _():
        x = x_ref[...].astype(BF16)
        g = jnp.dot(x, wg_bf[...], preferred_element_type=F32)
        u = jnp.dot(x, wu_bf[...], preferred_element_type=F32)
        a = (_silu(g) * u).astype(BF16)
        o_ref[...] = jnp.dot(a, wd_bf[...], preferred_element_type=F32)

    @pl.when(jnp.logical_not(active))
    def _():
        o_ref[...] = jnp.zeros_like(o_ref)


def _experts(xs, block_expert, block_first, block_next, n_used, w_gate, w_up, w_down, layer, tm=MOE_TM):
    n_pad, d = xs.shape
    f = w_gate.shape[-1]
    nblk = n_pad // tm
    grid_spec = pltpu.PrefetchScalarGridSpec(
        num_scalar_prefetch=4,
        grid=(nblk,),
        in_specs=[pl.BlockSpec((tm, d), lambda i, be, fi, nx, nu: (jnp.minimum(i, nu[0] - 1), 0)),
                  pl.BlockSpec(memory_space=pl.ANY),
                  pl.BlockSpec(memory_space=pl.ANY),
                  pl.BlockSpec(memory_space=pl.ANY)],
        out_specs=pl.BlockSpec((tm, d), lambda i, *_: (i, 0)),
        scratch_shapes=[pltpu.VMEM((d, f), F32), pltpu.VMEM((d, f), F32), pltpu.VMEM((f, d), F32),
                        pltpu.VMEM((d, f), BF16), pltpu.VMEM((d, f), BF16), pltpu.VMEM((f, d), BF16),
                        pltpu.SemaphoreType.DMA((3,))],
    )
    return pl.pallas_call(
        functools.partial(_experts_kernel, layer=layer, cast_rows=256),
        out_shape=jax.ShapeDtypeStruct((n_pad, d), F32),
        grid_spec=grid_spec,
        compiler_params=_cparams(("arbitrary",), vmem=EXPERT_VMEM_LIMIT_BYTES),
        name="moe_experts",
    )(block_expert, block_first, block_next, n_used, xs, w_gate, w_up, w_down)


def _combine_kernel(dest_ref, y_hbm, x_ref, route_ref, gate_ref, o_ref, buf, sem):
    tm = x_ref.shape[0]

    def issue(r, _):
        for k in range(2):
            d = dest_ref[0, 0, 2 * r + k]
            pltpu.make_async_copy(y_hbm.at[pl.ds(d, 1)], buf.at[k, pl.ds(r, 1)], sem).start(priority=k)
        return 0

    lax.fori_loop(0, tm, issue, 0, unroll=ROW_ISSUE_UNROLL)
    for k in range(2):
        pltpu.make_async_copy(y_hbm.at[pl.ds(0, tm)], buf.at[k], sem).wait()
    route = route_ref[...]
    w0 = route[:, _R_W:_R_W + 1]
    w1 = route[:, _R_W + 1:_R_W + 2]
    o_ref[...] = x_ref[...] + gate_ref[...] * (buf[0] * w0 + buf[1] * w1)


def _combine(x, y, dest, route, mods, layer, tm=ROW_TM):
    n, d = x.shape
    nblk = n // tm
    return pl.pallas_call(
        _combine_kernel,
        out_shape=jax.ShapeDtypeStruct((n, d), F32),
        grid=(nblk,),
        in_specs=[pl.BlockSpec((1, 1, 2 * tm), lambda i: (i, 0, 0), memory_space=pltpu.SMEM),
                  pl.BlockSpec(memory_space=pl.ANY),
                  pl.BlockSpec((tm, d), lambda i: (i, 0)),
                  pl.BlockSpec((tm, ROUTE_LANES), lambda i: (i, 0)),
                  mods.spec(layer, 5, tm)],
        out_specs=pl.BlockSpec((tm, d), lambda i: (i, 0)),
        scratch_shapes=[pltpu.VMEM((2, tm, d), F32), pltpu.SemaphoreType.DMA(())],
        compiler_params=_cparams(("arbitrary",)),
        name="moe_combine",
    )(dest.reshape(nblk, 1, 2 * tm), y, x, route, mods.mods)


def _moe_layer(x, mods, layer, norm_g, w_group, b_group, w_expert, b_expert, w_gate, w_up, w_down):
    n, d = x.shape
    ne = MOE_EXPERTS
    pad = ROUTE_LANES - MOE_GROUPS - ne
    w_route = jnp.concatenate([w_group, w_expert, jnp.zeros((d, pad), F32)], axis=1)
    b_route = jnp.concatenate([b_group, b_expert, jnp.zeros((pad,), F32)]).reshape(1, ROUTE_LANES)
    h, route, counts = _router(x, norm_g, mods, layer, w_route, b_route)
    cnt = counts[0, _R_LOGIT0:_R_LOGIT0 + ne].astype(jnp.int32)
    pcnt = (cnt + MOE_TM - 1) // MOE_TM * MOE_TM
    pends = jnp.cumsum(pcnt)
    pstarts = pends - pcnt
    experts = jnp.arange(ne, dtype=jnp.int32)
    eid = route[:, _R_EID:_R_EID + 2].astype(jnp.int32)
    rank = route[:, _R_RANK:_R_RANK + 2].astype(jnp.int32)
    dest = (jnp.sum(jnp.where(eid[..., None] == experts, pstarts, 0), axis=-1) + rank).reshape(-1)
    n_blocks = (n * 2) // MOE_TM + ne
    n_pad = n_blocks * MOE_TM
    block_row = jnp.arange(n_blocks, dtype=jnp.int32) * MOE_TM
    block_expert = jnp.minimum(jnp.sum((pends[None, :] <= block_row[:, None]).astype(jnp.int32), axis=1), ne - 1)
    block_first = jnp.concatenate([jnp.ones((1,), jnp.int32),
                                   (block_expert[1:] != block_expert[:-1]).astype(jnp.int32)])
    later = (experts[None, :] > experts[:, None]) & (pcnt[None, :] > 0)
    next_expert = jnp.min(jnp.where(later, experts[None, :], ne), axis=1)
    next_expert = jnp.where(next_expert == ne, -1, next_expert)
    block_next = jnp.sum(jnp.where(block_expert[:, None] == experts, next_expert, 0), axis=1)
    n_used = (pends[-1:] // MOE_TM).astype(jnp.int32)
    xs = _dispatch(h, dest, pcnt, pends, n_pad)
    y = _experts(xs, block_expert, block_first, block_next, n_used, w_gate, w_up, w_down, layer)
    return _combine(x, y, dest, route, mods, layer)


class _Streams:
    def __init__(self, batch, seq, dec_batch, dec_seq):
        self.ctx = (0, batch, seq)
        self.lat = (batch * seq, dec_batch, dec_seq)


def _s5_layer(x, mods, layer, streams, norm_g, lam_re, lam_im, log_step, b_re, b_im, c_re, c_im, d_skip,
              w_a, b_a, w_b, b_b, s0_re, s0_im):
    n, d = x.shape
    g, p = lam_re.shape[1:]
    operands = _s5_operands(_s5_discretize(lam_re, lam_im, log_step, b_re, b_im, c_re, c_im), c_re, c_im)
    h = _adanorm(x, norm_g, mods, layer, 0, 1)
    dvec = d_skip.reshape(1, d)
    row0, nseq, seq = streams.ctx
    zero = jnp.zeros((2, d // LANES, nseq, operands[3].shape[-1]), F32)
    z, sf = _s5_scan(h, operands, dvec, zero, row0, nseq, seq)
    row0, nseq, seq = streams.lat
    z, _ = _s5_scan(h, operands, dvec, _s5_state_to_lanes(s0_re, s0_im), row0, nseq, seq, z_prev=z)
    x = _mm_residual(z, [w_a.astype(BF16), w_b.astype(BF16)], [b_a, b_b], x, mods, layer, 2)
    new_re, new_im = _s5_state_from_lanes(sf, g, p)
    return x, new_re, new_im


def _mlstm_layer(x, mods, layer, streams, norm_g, w_in, b_gates, head_g, w_out, c0, n0, m0):
    n, d = x.shape
    hh = ML_HEADS
    dv = d // hh
    dk = dv // 2
    qk = hh * dk
    n_main = 2 * qk + 2 * d
    col_scale = jnp.concatenate([jnp.ones((qk,), F32), jnp.full((qk,), dk ** -0.5, F32),
                                 jnp.ones((2 * d,), F32)])
    w_main = (w_in[:, :n_main] * col_scale).astype(BF16)
    w_gates = jnp.pad(w_in[:, n_main:], ((0, 0), (0, LANES - 4 * hh)))
    proj = _mm_adanorm(x, norm_g, mods, layer, 0, 1, w_main, BF16)
    gates = _mm_adanorm(x, norm_g, mods, layer, 0, 1, w_gates, F32)
    k_t = jnp.transpose(proj[:, qk:2 * qk])
    g_c, w_c, e_c, d_r, w_r = _gate_prep(gates, b_gates, ML_T)
    fwd, bwd = slice(_G_FWD, _G_FWD + hh), slice(_G_BWD, _G_BWD + hh)
    cols = jnp.stack([g_c[:, fwd], w_c[:, fwd], e_c[:, fwd], g_c[:, bwd], w_c[:, bwd], e_c[:, bwd]], axis=-1)
    cols = jnp.pad(cols, ((0, 0), (0, 0), (0, 2)))
    rows = jnp.stack([d_r[fwd], w_r[fwd], d_r[bwd], w_r[bwd]], axis=1)

    def gate_views(row0, nseq, seq):
        gc = cols[row0:row0 + nseq * seq].reshape(nseq, seq, hh, 8)
        gr = rows[:, :, row0:row0 + nseq * seq].reshape(hh, 4, nseq, seq)
        return jnp.transpose(gc, (0, 2, 1, 3)), jnp.transpose(gr, (2, 0, 1, 3))

    def n_lanes(nv):
        return jnp.pad(jnp.transpose(nv, (0, 2, 1, 3))[..., None], ((0, 0),) * 4 + ((0, LANES - 1),))

    def m_lanes(mv):
        return jnp.broadcast_to(jnp.transpose(mv, (0, 2, 1))[..., None], mv.shape[:1] + (hh, 2, LANES))

    row0, nseq, seq = streams.ctx
    gcol, grow = gate_views(row0, nseq, seq)
    hn, c_f, n_f, m_f = _mlstm(
        proj, k_t, gcol, grow, head_g,
        jnp.zeros((nseq, 2, hh, dk, dv), F32), jnp.zeros((nseq, hh, 2, dk, LANES), F32),
        jnp.full((nseq, hh, 2, LANES), NEG_BIG, F32), row0, nseq, seq)
    row0, nseq, seq = streams.lat
    gcol, grow = gate_views(row0, nseq, seq)
    hn, _, _, _ = _mlstm(proj, k_t, gcol, grow, head_g, c0, n_lanes(n0), m_lanes(m0),
                         row0, nseq, seq, hn_prev=hn)
    x = _mm_residual(hn, [w_out.astype(BF16)], None, x, mods, layer, 2)
    return x, c_f, jnp.transpose(n_f[..., 0], (0, 2, 1, 3)), jnp.transpose(m_f[..., 0], (0, 2, 1))


def kernel(x_prompt, x_sample, state_s5_re, state_s5_im, state_mlstm_C, state_mlstm_n, state_mlstm_m, c, c_ctx, w_ada, b_ada, norm1_g, norm2_g, final_norm_g, s5_lambda_re, s5_lambda_im, s5_log_step, s5_b_re, s5_b_im, s5_c_re, s5_c_im, s5_d, s5_w_glu_a, s5_b_glu_a, s5_w_glu_b, s5_b_glu_b, ml_w_in, ml_b_gates, ml_head_norm_g, ml_w_out, moe_w_group, moe_b_group, moe_w_expert, moe_b_expert, moe_w_gate, moe_w_up, moe_w_down):
    batch, seq, d = x_prompt.shape
    dec_batch, dec_seq, _ = x_sample.shape
    depth = w_ada.shape[0]
    n_ctx = batch * seq
    streams = _Streams(batch, seq, dec_batch, dec_seq)

    cond = jnp.concatenate([c_ctx[None, :], c, jnp.zeros((MOD_ROWS - 1 - dec_batch, d), F32)], axis=0)
    mods = _modulation(cond, w_ada, b_ada).reshape(depth * MOD_ROWS * N_MOD, 1, d)
    mods = _Mods(mods, n_ctx, dec_seq)

    x = jnp.concatenate([x_prompt.reshape(n_ctx, d), _embed(x_sample).reshape(dec_batch * dec_seq, d)], axis=0)
    new_s5_re, new_s5_im, new_c, new_n, new_m = [], [], [], [], []
    for l in range(depth):
        j = l // N_MIXERS
        if l % N_MIXERS == 0:
            x, s_re, s_im = _s5_layer(
                x, mods, l, streams, norm1_g[l], s5_lambda_re[j], s5_lambda_im[j], s5_log_step[j],
                s5_b_re[j], s5_b_im[j], s5_c_re[j], s5_c_im[j], s5_d[j],
                s5_w_glu_a[j], s5_b_glu_a[j], s5_w_glu_b[j], s5_b_glu_b[j],
                state_s5_re[:, j], state_s5_im[:, j])
            new_s5_re.append(s_re)
            new_s5_im.append(s_im)
        else:
            x, c_f, n_f, m_f = _mlstm_layer(
                x, mods, l, streams, norm1_g[l], ml_w_in[j], ml_b_gates[j], ml_head_norm_g[j], ml_w_out[j],
                state_mlstm_C[:, j], state_mlstm_n[:, j], state_mlstm_m[:, j])
            new_c.append(c_f)
            new_n.append(n_f)
            new_m.append(m_f)
        x = _moe_layer(x, mods, l, norm2_g[l], moe_w_group[l], moe_b_group[l], moe_w_expert[l],
                       moe_b_expert[l], moe_w_gate, moe_w_up, moe_w_down)
    y_prompt = _final_norm(x, final_norm_g, 0, n_ctx).reshape(batch, seq, d)
    y_sample = _final_norm(x, final_norm_g, n_ctx, dec_batch * dec_seq).reshape(dec_batch, dec_seq, d)
    return (y_prompt, y_sample, jnp.stack(new_s5_re, axis=1), jnp.stack(new_s5_im, axis=1),
            jnp.stack(new_c, axis=1), jnp.stack(new_n, axis=1), jnp.stack(new_m, axis=1))
```

```python
import functools
import math

import jax
import jax.numpy as jnp
from jax import lax
from jax.experimental import pallas as pl
from jax.experimental.pallas import tpu as pltpu

F32 = jnp.float32
BF16 = jnp.bfloat16

S5_GROUP_CH = 16
ML_HEADS = 8
MOE_GROUPS = 4
MOE_EPG = 8
MOE_EXPERTS = MOE_GROUPS * MOE_EPG
GRID_W = 64
POS_BASE = 10000.0
RMS_EPS = 1e-6
NEG_BIG = -1e30
N_MIXERS = 2
N_MOD = 6

LANES = 128
SUBLANES = 8
VMEM_LIMIT_BYTES = 56 * 1024 * 1024
EXPERT_VMEM_LIMIT_BYTES = 60 * 1024 * 1024

MOD_ROWS = 16
S5_LANE_GROUPS = LANES // S5_GROUP_CH
S5_TC = 64
ML_T = 256
ML_HP = 2
ROUTE_LANES = LANES
MOE_TM = 256
ROW_TM = 256
ROW_ISSUE_UNROLL = 8


def _cparams(sem, vmem=VMEM_LIMIT_BYTES):
    return pltpu.CompilerParams(dimension_semantics=sem, vmem_limit_bytes=vmem)


def _silu(x):
    return x * jax.nn.sigmoid(x)


def _gelu_tanh(x):
    c = math.sqrt(2.0 / math.pi)
    return 0.5 * x * (1.0 + jnp.tanh(c * (x + 0.044715 * (x * x * x))))


def _ada_norm_tile(x, g, shift, scale):
    r = lax.rsqrt(jnp.mean(x * x, axis=-1, keepdims=True) + RMS_EPS)
    return (x * r * g) * (1.0 + scale) + shift


def _mod_kernel(c_ref, w_ref, b_ref, o_ref):
    o_ref[...] = _dot_3pass(_silu(c_ref[...]), w_ref[...]) + b_ref[...]


def _modulation(cond, w_ada, b_ada, tn=1024):
    depth, d, n = w_ada.shape
    return pl.pallas_call(
        _mod_kernel,
        out_shape=jax.ShapeDtypeStruct((depth, MOD_ROWS, n), F32),
        grid=(depth, n // tn),
        in_specs=[
            pl.BlockSpec((MOD_ROWS, d), lambda l, j: (0, 0)),
            pl.BlockSpec((None, d, tn), lambda l, j: (l, 0, j)),
            pl.BlockSpec((None, 1, tn), lambda l, j: (l, 0, j)),
        ],
        out_specs=pl.BlockSpec((None, MOD_ROWS, tn), lambda l, j: (l, 0, j)),
        compiler_params=_cparams(("parallel", "parallel")),
        name="modulation",
    )(cond, w_ada, b_ada.reshape(depth, 1, n))


class _Mods:
    def __init__(self, mods, n_ctx, dec_seq):
        self.mods = mods
        self.n_ctx = n_ctx
        self.dec_seq = dec_seq
        self.d = mods.shape[-1]

    def spec(self, layer, k, tm, row_offset=0):
        base = layer * MOD_ROWS * N_MOD + k
        n_ctx, dec_seq = self.n_ctx, self.dec_seq

        def index(i, *_):
            row0 = i * tm + row_offset
            r = jnp.where(row0 < n_ctx, 0, 1 + (row0 - n_ctx) // dec_seq)
            return (base + r * N_MOD, 0, 0)

        return pl.BlockSpec((None, 1, self.d), index)


def _embed_kernel(x_ref, p_ref, o_ref):
    o_ref[...] = x_ref[...] + p_ref[...]


def _grid_pos_embed(n_tokens, d):
    rows = n_tokens // GRID_W
    r, col = jnp.meshgrid(jnp.arange(rows, dtype=F32), jnp.arange(GRID_W, dtype=F32), indexing="ij")
    quarter = d // 4
    omega = 1.0 / (POS_BASE ** (jnp.arange(quarter, dtype=F32) / quarter))

    def axis_embed(pos):
        ang = pos.reshape(-1, 1) * omega[None, :]
        return jnp.concatenate([jnp.sin(ang), jnp.cos(ang)], axis=-1)

    return jnp.concatenate([axis_embed(r), axis_embed(col)], axis=-1)


def _embed(x_sample):
    b, l, d = x_sample.shape
    pos = _grid_pos_embed(l, d)
    return pl.pallas_call(
        _embed_kernel,
        out_shape=jax.ShapeDtypeStruct((b, l, d), F32),
        grid=(b,),
        in_specs=[pl.BlockSpec((None, l, d), lambda i: (i, 0, 0)),
                  pl.BlockSpec((l, d), lambda i: (0, 0))],
        out_specs=pl.BlockSpec((None, l, d), lambda i: (i, 0, 0)),
        compiler_params=_cparams(("parallel",)),
        name="pos_embed",
    )(x_sample, pos)


def _adanorm_kernel(x_ref, g_ref, sh_ref, sc_ref, o_ref):
    o_ref[...] = _ada_norm_tile(x_ref[...], g_ref[...], sh_ref[...], sc_ref[...]).astype(o_ref.dtype)


def _adanorm(x, g, mods, layer, k_shift, k_scale, tm=512):
    n, d = x.shape
    return pl.pallas_call(
        _adanorm_kernel,
        out_shape=jax.ShapeDtypeStruct((n, d), F32),
        grid=(n // tm,),
        in_specs=[pl.BlockSpec((tm, d), lambda i: (i, 0)),
                  pl.BlockSpec((1, d), lambda i: (0, 0)),
                  mods.spec(layer, k_shift, tm),
                  mods.spec(layer, k_scale, tm)],
        out_specs=pl.BlockSpec((tm, d), lambda i: (i, 0)),
        compiler_params=_cparams(("parallel",)),
        name="adanorm",
    )(x, g.reshape(1, d), mods.mods, mods.mods)


def _final_norm_kernel(x_ref, g_ref, o_ref):
    x = x_ref[...]
    r = lax.rsqrt(jnp.mean(x * x, axis=-1, keepdims=True) + RMS_EPS)
    o_ref[...] = x * r * g_ref[...]


def _final_norm(x, g, row0, nrows, tm=512):
    d = x.shape[1]
    off = row0 // tm
    return pl.pallas_call(
        _final_norm_kernel,
        out_shape=jax.ShapeDtypeStruct((nrows, d), F32),
        grid=(nrows // tm,),
        in_specs=[pl.BlockSpec((tm, d), lambda i: (i + off, 0)),
                  pl.BlockSpec((1, d), lambda i: (0, 0))],
        out_specs=pl.BlockSpec((tm, d), lambda i: (i, 0)),
        compiler_params=_cparams(("parallel",)),
        name="final_norm",
    )(x, g.reshape(1, d))


def _cmul(ar, ai, br, bi):
    return ar * br - ai * bi, ar * bi + ai * br


def _s5_param_kernel(lre_ref, lim_ref, ls_ref, bre_ref, bim_ref, cre_ref, cim_ref,
                     b2_ref, c2_ref, d2_ref, a2_ref):
    h, s = bre_ref.shape
    gl = LANES // h
    p = s // gl
    lr = lre_ref[...]
    li = lim_ref[...]
    dt = jnp.exp(ls_ref[...])
    mag = jnp.exp(lr * dt)
    ar = mag * jnp.cos(li * dt)
    ai = mag * jnp.sin(li * dt)
    den = lr * lr + li * li
    zr = ((ar - 1.0) * lr + ai * li) / den
    zi = (ai * lr - (ar - 1.0) * li) / den
    a2 = _cmul(ar, ai, ar, ai)
    bb = _cmul(zr, zi, bre_ref[...], bim_ref[...])
    abb = _cmul(ar, ai, *bb)
    cc = (cre_ref[...], cim_ref[...])
    ca = _cmul(*cc, ar, ai)
    ca2 = _cmul(*cc, *a2)

    same_b = (lax.broadcasted_iota(jnp.int32, (LANES, s), 0) // h
              == lax.broadcasted_iota(jnp.int32, (LANES, s), 1) // p)
    same_c = (lax.broadcasted_iota(jnp.int32, (s, LANES), 0) // p
              == lax.broadcasted_iota(jnp.int32, (s, LANES), 1) // h)
    spread = (lax.broadcasted_iota(jnp.int32, (h, LANES), 1) % h
              == lax.broadcasted_iota(jnp.int32, (h, LANES), 0)).astype(BF16)

    def bblock(re, im):
        def one(x):
            return jnp.where(same_b, jnp.concatenate([x] * gl, axis=0), 0.0)
        return jnp.concatenate([one(re), one(im)], axis=1)

    def cblock(re, im):
        def one(x):
            t = sum(lax.dot_general(part, spread, (((0,), (0,)), ((), ())), preferred_element_type=F32)
                    for part in _split_bf16(x))
            return jnp.where(same_c, t, 0.0)
        return jnp.concatenate([one(re), one(-im)], axis=0)

    b_blk = bblock(*bb)
    ab_blk = bblock(*abb)
    c_blk = cblock(*cc)
    cb = _dot_3pass(b_blk, c_blk)
    cab = _dot_3pass(ab_blk, c_blk)
    b2_ref[...] = jnp.concatenate([ab_blk, b_blk], axis=0).astype(BF16)
    c2_ref[...] = jnp.concatenate([cblock(*ca), cblock(*ca2)], axis=1).astype(BF16)
    d2_ref[...] = jnp.concatenate([jnp.concatenate([cb, cab], axis=1),
                                   jnp.concatenate([jnp.zeros_like(cb), cb], axis=1)], axis=0).astype(BF16)
    a2_ref[...] = jnp.concatenate(a2, axis=1)


def _s5_params(lam_re, lam_im, log_step, b_re, b_im, c_re, c_im):
    _, g, p = lam_re.shape
    h = b_re.shape[-1]
    gl = S5_LANE_GROUPS
    nlc = g // gl
    s = gl * p

    def lanes(v):
        return v.reshape(2, nlc, 1, s)

    def rows(m):
        return jnp.transpose(m.reshape(2, h, nlc, s), (0, 2, 1, 3))

    step = jnp.broadcast_to(log_step[:, :, None], (2, g, p))
    vec = pl.BlockSpec((None, None, 1, s), lambda d, j: (d, j, 0, 0))
    mat = pl.BlockSpec((None, None, h, s), lambda d, j: (d, j, 0, 0))

    def out(r, c, dt):
        return (jax.ShapeDtypeStruct((2, nlc, r, c), dt),
                pl.BlockSpec((None, None, r, c), lambda d, j: (d, j, 0, 0)))

    outs = [out(2 * LANES, 2 * s, BF16), out(2 * s, 2 * LANES, BF16), out(2 * LANES, 2 * LANES, BF16),
            out(1, 2 * s, F32)]
    return pl.pallas_call(
        _s5_param_kernel,
        out_shape=tuple(o[0] for o in outs),
        grid=(2, nlc),
        in_specs=[vec, vec, vec, mat, mat, mat, mat],
        out_specs=tuple(o[1] for o in outs),
        compiler_params=_cparams(("parallel", "parallel")),
        name="s5_params",
    )(lanes(lam_re), lanes(lam_im), lanes(step),
      rows(jnp.transpose(b_re, (0, 3, 1, 2))), rows(jnp.transpose(b_im, (0, 3, 1, 2))),
      rows(jnp.transpose(c_re, (0, 2, 1, 3))), rows(jnp.transpose(c_im, (0, 2, 1, 3))))


def _s5_state_to_lanes(s_re, s_im):
    b, _, g, p = s_re.shape
    gl = S5_LANE_GROUPS
    nlc = g // gl

    def lay(s):
        return jnp.transpose(s.reshape(b, 2, nlc, gl * p), (1, 2, 0, 3))

    return jnp.concatenate([lay(s_re), lay(s_im)], axis=-1)


def _s5_state_from_lanes(s, g, p):
    _, nlc, b, s2 = s.shape
    half = s2 // 2

    def unlay(t):
        return jnp.transpose(t, (2, 0, 1, 3)).reshape(b, 2, g, p)

    return unlay(s[..., :half]), unlay(s[..., half:])


def _s5_scan_kernel(*refs, seq, tc, aliased):
    if aliased:
        h_ref, b2_ref, c2_ref, d2_ref, a_ref, d_ref, s0_ref, _, z_ref, sf_ref = refs[:10]
    else:
        h_ref, b2_ref, c2_ref, d2_ref, a_ref, d_ref, s0_ref, z_ref, sf_ref = refs[:9]
    scr = (refs[-10:-5], refs[-5:])
    nb = SUBLANES
    half = a_ref.shape[-1] // 2
    nc = seq // tc
    npair = tc // 2
    a_re = [jnp.broadcast_to(a_ref[dr][:, :half], (nb, half)) for dr in range(2)]
    a_im = [jnp.broadcast_to(a_ref[dr][:, half:], (nb, half)) for dr in range(2)]

    def chunk(c, carry):
        t0s = (c * tc, (nc - 1 - c) * tc)

        def pair_rows(dr, p):
            if dr == 0:
                first = t0s[0] + 2 * p
                return pl.ds(first, nb, stride=seq), pl.ds(first + 1, nb, stride=seq)
            first = t0s[1] + tc - 1 - 2 * p
            return pl.ds(first, nb, stride=seq), pl.ds(first - 1, nb, stride=seq)

        carry = list(carry)
        for dr in range(2):
            u_scr, bu_scr, _, _, _ = scr[dr]
            for p in range(npair):
                r1, r2 = pair_rows(dr, p)
                u_scr[p * nb:(p + 1) * nb, :LANES] = h_ref[r1, :]
                u_scr[p * nb:(p + 1) * nb, LANES:] = h_ref[r2, :]
            bu_scr[...] = jnp.dot(u_scr[...].astype(BF16), b2_ref[dr], preferred_element_type=F32)
        for dr in range(2):
            _, bu_scr, x_scr, _, _ = scr[dr]
            xr, xi = carry[2 * dr], carry[2 * dr + 1]
            for p in range(npair):
                x_scr[p * nb:(p + 1) * nb, :half] = xr
                x_scr[p * nb:(p + 1) * nb, half:] = xi
                bu = bu_scr[p * nb:(p + 1) * nb, :]
                xr, xi = (a_re[dr] * xr - a_im[dr] * xi + bu[:, :half],
                          a_re[dr] * xi + a_im[dr] * xr + bu[:, half:])
            carry[2 * dr], carry[2 * dr + 1] = xr, xi
        for dr in range(2):
            u_scr, _, x_scr, yo_scr, y_scr = scr[dr]
            yo_scr[...] = jnp.dot(x_scr[...].astype(BF16), c2_ref[dr], preferred_element_type=F32) \
                + jnp.dot(u_scr[...].astype(BF16), d2_ref[dr], preferred_element_type=F32)
            for p in range(npair):
                r1, r2 = pair_rows(dr, p)
                y_scr[r1, :] = yo_scr[p * nb:(p + 1) * nb, :LANES]
                y_scr[r2, :] = yo_scr[p * nb:(p + 1) * nb, LANES:]
        return tuple(carry)

    s0f = s0_ref[0]
    s0b = s0_ref[1]
    init = (s0f[:, :half], s0f[:, half:], s0b[:, :half], s0b[:, half:])
    xr_f, xi_f, xr_b, xi_b = lax.fori_loop(0, nc, chunk, init)
    sf_ref[0, :, :half] = xr_f
    sf_ref[0, :, half:] = xi_f
    sf_ref[1, :, :half] = xr_b
    sf_ref[1, :, half:] = xi_b
    y = h_ref[...] * d_ref[...] + scr[0][4][...] + scr[1][4][...]
    z_ref[...] = _gelu_tanh(y).astype(z_ref.dtype)


def _s5_scan(h, operands, dvec, s0, row0, nseq, seq, z_prev=None):
    b2, c2, d2, avec = operands
    n, d = h.shape
    nlc = d // LANES
    s2 = avec.shape[-1]
    nb = SUBLANES
    rows = nb * seq
    off = row0 // rows
    aliased = z_prev is not None
    in_specs = [
        pl.BlockSpec((rows, LANES), lambda i, j: (i + off, j)),
        pl.BlockSpec((2, None, 2 * LANES, s2), lambda i, j: (0, j, 0, 0)),
        pl.BlockSpec((2, None, s2, 2 * LANES), lambda i, j: (0, j, 0, 0)),
        pl.BlockSpec((2, None, 2 * LANES, 2 * LANES), lambda i, j: (0, j, 0, 0)),
        pl.BlockSpec((2, None, 1, s2), lambda i, j: (0, j, 0, 0)),
        pl.BlockSpec((1, LANES), lambda i, j: (0, j)),
        pl.BlockSpec((2, None, nb, s2), lambda i, j: (0, j, i, 0)),
    ]
    args = [h, b2, c2, d2, avec, dvec, s0]
    io_alias = {}
    if aliased:
        in_specs.append(pl.BlockSpec(memory_space=pl.ANY))
        args.append(z_prev)
        io_alias = {len(args) - 1: 0}
    npair_rows = S5_TC // 2 * nb
    return pl.pallas_call(
        functools.partial(_s5_scan_kernel, seq=seq, tc=S5_TC, aliased=aliased),
        out_shape=(jax.ShapeDtypeStruct((n, d), BF16), jax.ShapeDtypeStruct((2, nlc, nseq, s2), F32)),
        grid=(nseq // nb, nlc),
        in_specs=in_specs,
        out_specs=(pl.BlockSpec((rows, LANES), lambda i, j: (i + off, j)),
                   pl.BlockSpec((2, None, nb, s2), lambda i, j: (0, j, i, 0))),
        scratch_shapes=2 * [pltpu.VMEM((npair_rows, 2 * LANES), F32),
                            pltpu.VMEM((npair_rows, s2), F32),
                            pltpu.VMEM((npair_rows, s2), F32),
                            pltpu.VMEM((npair_rows, 2 * LANES), F32),
                            pltpu.VMEM((rows, LANES), F32)],
        input_output_aliases=io_alias,
        compiler_params=_cparams(("parallel", "parallel")),
        name="s5_scan",
    )(*args)


def _mm_res_kernel(*refs, n_w, has_bias):
    z_ref = refs[0]
    w_refs = refs[1:1 + n_w]
    pos = 1 + n_w
    b_refs = refs[pos:pos + n_w] if has_bias else ()
    pos += n_w if has_bias else 0
    x_ref, gate_ref, o_ref = refs[pos:pos + 3]
    z = z_ref[...]
    acc = [jnp.dot(z, w[...], preferred_element_type=F32) for w in w_refs]
    if has_bias:
        acc = [a + b[...] for a, b in zip(acc, b_refs)]
    y = acc[0] if n_w == 1 else acc[0] * jax.nn.sigmoid(acc[1])
    o_ref[...] = x_ref[...] + gate_ref[...] * y


def _mm_residual(z, ws, bs, x, mods, layer, k_gate, tm=1024, tn=512):
    n, k = z.shape
    n_out = ws[0].shape[1]
    n_w = len(ws)
    has_bias = bs is not None
    in_specs = [pl.BlockSpec((tm, k), lambda i, j: (i, 0))]
    in_specs += [pl.BlockSpec((k, tn), lambda i, j: (0, j)) for _ in ws]
    args = [z, *ws]
    if has_bias:
        in_specs += [pl.BlockSpec((1, tn), lambda i, j: (0, j)) for _ in bs]
        args += [b.reshape(1, n_out) for b in bs]
    gate_spec = mods.spec(layer, k_gate, tm)
    gate_spec = pl.BlockSpec((None, 1, tn), lambda i, j, f=gate_spec.index_map: (f(i)[0], 0, j))
    in_specs += [pl.BlockSpec((tm, tn), lambda i, j: (i, j)), gate_spec]
    args += [x, mods.mods]
    return pl.pallas_call(
        functools.partial(_mm_res_kernel, n_w=n_w, has_bias=has_bias),
        out_shape=jax.ShapeDtypeStruct((n, n_out), F32),
        grid=(n // tm, n_out // tn),
        in_specs=in_specs,
        out_specs=pl.BlockSpec((tm, tn), lambda i, j: (i, j)),
        compiler_params=_cparams(("parallel", "parallel")),
        name=f"proj_residual_{n_w}w",
    )(*args)


def _split_bf16(a):
    hi = a.astype(BF16)
    return hi, (a - hi.astype(F32)).astype(BF16)


def _dot_3pass(a, b):
    a_hi, a_lo = _split_bf16(a)
    b_hi, b_lo = _split_bf16(b)
    return (jnp.dot(a_hi, b_hi, preferred_element_type=F32) + jnp.dot(a_hi, b_lo, preferred_element_type=F32)
            + jnp.dot(a_lo, b_hi, preferred_element_type=F32))


def _mm_norm_kernel(x_ref, g_ref, sh_ref, sc_ref, w_ref, o_ref, h_scr, *, split):
    @pl.when(pl.program_id(1) == 0)
    def _():
        h_scr[...] = _ada_norm_tile(x_ref[...], g_ref[...], sh_ref[...], sc_ref[...]).astype(h_scr.dtype)

    if split:
        o_ref[...] = _dot_3pass(h_scr[...], w_ref[...]).astype(o_ref.dtype)
    else:
        o_ref[...] = jnp.dot(h_scr[...], w_ref[...], preferred_element_type=F32).astype(o_ref.dtype)


def _mm_adanorm(x, g, mods, layer, k_shift, k_scale, w, out_dtype, tm=1024, tn=1024):
    n, d = x.shape
    n_out = w.shape[1]
    tn = min(tn, n_out)
    exact = w.dtype == F32
    return pl.pallas_call(
        functools.partial(_mm_norm_kernel, split=exact),
        out_shape=jax.ShapeDtypeStruct((n, n_out), out_dtype),
        grid=(n // tm, n_out // tn),
        in_specs=[pl.BlockSpec((tm, d), lambda i, j: (i, 0)),
                  pl.BlockSpec((1, d), lambda i, j: (0, 0)),
                  mods.spec(layer, k_shift, tm),
                  mods.spec(layer, k_scale, tm),
                  pl.BlockSpec((d, tn), lambda i, j: (0, j))],
        out_specs=pl.BlockSpec((tm, tn), lambda i, j: (i, j)),
        scratch_shapes=[pltpu.VMEM((tm, d), F32 if exact else BF16)],
        compiler_params=_cparams(("parallel", "arbitrary")),
        name="adanorm_proj",
    )(x, g.reshape(1, d), mods.mods, mods.mods, w)


_G_FWD = ML_HEADS
_G_BWD = 3 * ML_HEADS


def _gate_kernel(xc_ref, xr_ref, bc_ref, br_ref, g_ref, w_ref, e_ref, dr_ref, wr_ref):
    t = xc_ref.shape[0]
    r_i = lax.broadcasted_iota(jnp.int32, (t, t), 0)
    c_i = lax.broadcasted_iota(jnp.int32, (t, t), 1)
    lower = (c_i <= r_i).astype(F32)
    upper = (c_i >= r_i).astype(F32)
    hi = lax.Precision.HIGHEST
    xc = xc_ref[...] + bc_ref[...]
    fc = jax.nn.log_sigmoid(xc)
    lane = lax.broadcasted_iota(jnp.int32, xc.shape, 1)
    g_c = jnp.where(lane < 2 * ML_HEADS,
                    jnp.dot(lower, fc, preferred_element_type=F32, precision=hi),
                    jnp.dot(upper, fc, preferred_element_type=F32, precision=hi))
    e_c = jnp.broadcast_to(jnp.sum(fc, axis=0, keepdims=True), xc.shape)
    g_ref[...] = g_c
    e_ref[...] = e_c
    w_ref[...] = e_c - g_c + pltpu.roll(xc, ML_HEADS, 1)
    xr = xr_ref[...] + br_ref[...]
    fr = jax.nn.log_sigmoid(xr)
    row = lax.broadcasted_iota(jnp.int32, xr.shape, 0)
    g_r = jnp.where(row < 2 * ML_HEADS,
                    jnp.dot(fr, upper, preferred_element_type=F32, precision=hi),
                    jnp.dot(fr, lower, preferred_element_type=F32, precision=hi))
    i_r = pltpu.roll(xr, ML_HEADS, 0)
    dr_ref[...] = i_r - g_r
    wr_ref[...] = jnp.sum(fr, axis=1, keepdims=True) - g_r + i_r


def _gate_prep(gates, b_gates, tch):
    n = gates.shape[0]
    ng = 4 * ML_HEADS
    bias_c = jnp.pad(b_gates, (0, LANES - ng)).reshape(1, LANES)
    bias_r = b_gates.reshape(ng, 1)
    gates_r = jnp.transpose(gates[:, :ng])
    col = pl.BlockSpec((tch, LANES), lambda i: (i, 0))
    rowb = pl.BlockSpec((ng, tch), lambda i: (0, i))
    return pl.pallas_call(
        _gate_kernel,
        out_shape=(jax.ShapeDtypeStruct((n, LANES), F32),) * 3 + (jax.ShapeDtypeStruct((ng, n), F32),) * 2,
        grid=(n // tch,),
        in_specs=[col, rowb, pl.BlockSpec((1, LANES), lambda i: (0, 0)),
                  pl.BlockSpec((ng, 1), lambda i: (0, 0))],
        out_specs=(col, col, col, rowb, rowb),
        compiler_params=_cparams(("parallel",)),
        name="mlstm_gates",
    )(gates, gates_r, bias_c, bias_r)


def _mlstm_kernel(q_ref, kt_ref, v_ref, o_ref, gcol_ref, grow_ref, hg_ref,
                  c0_ref, n0_ref, m0_ref, hn_ref, cf_ref, nf_ref, mf_ref, *scratch, seq, tch, hp):
    nc = seq // tch
    dv = v_ref.shape[-1] // hp
    dk = q_ref.shape[-1] // hp
    tt = lax.broadcasted_iota(jnp.int32, (tch, tch), 0)
    ss = lax.broadcasted_iota(jnp.int32, (tch, tch), 1)
    ones_col = (lax.broadcasted_iota(jnp.int32, (tch, LANES), 1) == 0).astype(BF16)
    scr = {(hd, dr): scratch[3 * (2 * hd + dr):3 * (2 * hd + dr) + 3] for hd in range(hp) for dr in range(2)}
    for (hd, dr), (_, c_scr, vx) in scr.items():
        c_scr[:, :dv] = c0_ref[dr, hd]
        c_scr[:, dv:] = n0_ref[hd, dr]
        vx[:, dv:] = ones_col

    def chunk_dir(hd, dr, r0, m):
        hacc, c_scr, vx = scr[hd, dr]
        rows = pl.ds(pl.multiple_of(r0, tch), tch)
        q = q_ref[rows, hd * dk:(hd + 1) * dk]
        kt = kt_ref[hd * dk:(hd + 1) * dk, rows]
        vx[:, :dv] = v_ref[rows, hd * dv:(hd + 1) * dv]
        gc = gcol_ref[hd, rows, :]
        gr = grow_ref[hd, :, rows]
        g_col = gc[:, 3 * dr:3 * dr + 1]
        w_col = gc[:, 3 * dr + 1:3 * dr + 2]
        e_col = gc[:, 3 * dr + 2:3 * dr + 3]
        d_row = gr[2 * dr:2 * dr + 1, :]
        w_row = gr[2 * dr + 1:2 * dr + 2, :]
        mask = (ss <= tt) if dr == 0 else (ss >= tt)
        a_col = g_col + m
        dmat = jnp.where(mask, g_col + d_row, -jnp.inf)
        mt = jnp.maximum(a_col, jnp.max(dmat, axis=1, keepdims=True))
        qk = jnp.dot(q, kt, preferred_element_type=F32)
        s = (qk * jnp.exp(dmat - mt)).astype(BF16)
        inter = jnp.exp(a_col - mt)
        vext = vx[...]
        tot = jnp.dot(s, vext, preferred_element_type=F32) \
            + inter * jnp.dot(q, c_scr[...].astype(BF16), preferred_element_type=F32)
        den = tot[:, dv:dv + 1]
        hacc[rows, :] = tot[:, :dv] / jnp.maximum(jnp.abs(den), jnp.exp(-mt))
        g_end = jnp.max(e_col, axis=0, keepdims=True)
        m_new = jnp.maximum(g_end + m, jnp.max(w_col, axis=0, keepdims=True))
        decay = jnp.exp(g_end + m - m_new)
        kw = (kt.astype(F32) * jnp.exp(w_row - m_new)).astype(BF16)
        c_scr[...] = decay * c_scr[...] + jnp.dot(kw, vext, preferred_element_type=F32)
        return m_new

    keys = list(scr)

    def body(c, carry):
        return tuple(chunk_dir(hd, dr, (c if dr == 0 else nc - 1 - c) * tch, m)
                     for (hd, dr), m in zip(keys, carry))

    m_fin = lax.fori_loop(0, nc, body, tuple(m0_ref[hd, dr:dr + 1, 0:1] for hd, dr in keys))
    for (hd, dr), m in zip(keys, m_fin):
        c_scr = scr[hd, dr][1]
        cf_ref[dr, hd] = c_scr[:, :dv]
        nf_ref[hd, dr] = c_scr[:, dv:]
        mf_ref[hd, dr:dr + 1, :] = jnp.broadcast_to(m, (1, LANES))
    for hd in range(hp):
        cols = slice(hd * dv, (hd + 1) * dv)
        hs = scr[hd, 0][0][...] + scr[hd, 1][0][...]
        hn = hs * lax.rsqrt(jnp.mean(hs * hs, axis=-1, keepdims=True) + RMS_EPS)
        hn = hn * hg_ref[:, cols] * jax.nn.sigmoid(o_ref[:, cols].astype(F32))
        hn_ref[:, cols] = hn.astype(hn_ref.dtype)


def _mlstm(proj, k_t, gcol, grow, head_g, c0, n0, m0, row0, nseq, seq, hn_prev=None, hp=ML_HP):
    n = proj.shape[0]
    h = ML_HEADS
    dk = c0.shape[-2]
    dv = c0.shape[-1]
    d = h * dv
    qk = h * dk
    off = row0 // seq
    tch = min(ML_T, seq)
    aliased = hn_prev is not None
    c_spec = pl.BlockSpec((None, 2, hp, dk, dv), lambda b, j: (b, 0, j, 0, 0))
    n_spec = pl.BlockSpec((None, hp, 2, dk, LANES), lambda b, j: (b, j, 0, 0, 0))
    m_spec = pl.BlockSpec((None, hp, 2, LANES), lambda b, j: (b, j, 0, 0))
    wk, wv = hp * dk, hp * dv
    in_specs = [
        pl.BlockSpec((seq, wk), lambda b, j: (b + off, j)),
        pl.BlockSpec((wk, seq), lambda b, j: (j, b + off)),
        pl.BlockSpec((seq, wv), lambda b, j: (b + off, 2 * qk // wv + j)),
        pl.BlockSpec((seq, wv), lambda b, j: (b + off, (2 * qk + d) // wv + j)),
        pl.BlockSpec((None, hp, seq, 8), lambda b, j: (b, j, 0, 0)),
        pl.BlockSpec((None, hp, 4, seq), lambda b, j: (b, j, 0, 0)),
        pl.BlockSpec((1, wv), lambda b, j: (0, j)),
        c_spec, n_spec, m_spec,
    ]
    args = [proj, k_t, proj, proj, gcol, grow, head_g.reshape(1, d), c0, n0, m0]
    n_in = len(args)
    io_alias = {}
    if aliased:
        in_specs.append(pl.BlockSpec(memory_space=pl.ANY))
        args.append(hn_prev)
        io_alias = {n_in: 0}

    def kern(*refs):
        if aliased:
            refs = refs[:n_in] + refs[n_in + 1:]
        _mlstm_kernel(*refs, seq=seq, tch=tch, hp=hp)

    return pl.pallas_call(
        kern,
        out_shape=(jax.ShapeDtypeStruct((n, d), BF16),
                   jax.ShapeDtypeStruct((nseq, 2, h, dk, dv), F32),
                   jax.ShapeDtypeStruct((nseq, h, 2, dk, LANES), F32),
                   jax.ShapeDtypeStruct((nseq, h, 2, LANES), F32)),
        grid=(nseq, h // hp),
        in_specs=in_specs,
        out_specs=(pl.BlockSpec((seq, wv), lambda b, j: (b + off, j)), c_spec, n_spec, m_spec),
        scratch_shapes=2 * hp * [pltpu.VMEM((seq, dv), F32), pltpu.VMEM((dk, dv + LANES), F32),
                                 pltpu.VMEM((tch, dv + LANES), BF16)],
        input_output_aliases=io_alias,
        compiler_params=_cparams(("parallel", "parallel")),
        name="mlstm",
    )(*args)


_R_EID, _R_W, _R_RANK = 0, 2, 4
_R_LOGIT0 = MOE_GROUPS


def _router_kernel(x_ref, g_ref, sh_ref, sc_ref, whi_ref, wlo_ref, b_ref, h_ref, route_ref, cnt_ref, carry):
    i = pl.program_id(0)

    @pl.when(i == 0)
    def _():
        carry[...] = jnp.zeros_like(carry)

    h = _ada_norm_tile(x_ref[...], g_ref[...], sh_ref[...], sc_ref[...])
    h_ref[...] = h
    tm = h.shape[0]
    h_hi = h.astype(BF16)
    h_lo = (h - h_hi.astype(F32)).astype(BF16)
    logits = (jnp.dot(h_hi, whi_ref[...], preferred_element_type=F32)
              + jnp.dot(h_hi, wlo_ref[...], preferred_element_type=F32)
              + jnp.dot(h_lo, whi_ref[...], preferred_element_type=F32)) + b_ref[...]
    lane = lax.broadcasted_iota(jnp.int32, logits.shape, 1)
    big = jnp.int32(ROUTE_LANES)

    def first_lane(cond):
        return jnp.min(jnp.where(cond, lane, big), axis=1, keepdims=True)

    glog = jnp.where(lane < MOE_GROUPS, logits, -jnp.inf)
    ge = jnp.exp(glog - jnp.max(glog, axis=1, keepdims=True))
    pgrp = ge / jnp.sum(ge, axis=1, keepdims=True)
    pg = jnp.max(pgrp, axis=1, keepdims=True)
    grp = first_lane(pgrp == pg)
    e_lane = lane - _R_LOGIT0
    in_grp = (e_lane >= 0) & (e_lane < MOE_EXPERTS) & ((e_lane // MOE_EPG) == grp)
    elog = jnp.where(in_grp, logits, -jnp.inf)
    ee = jnp.exp(elog - jnp.max(elog, axis=1, keepdims=True))
    pe = jnp.where(in_grp, ee / jnp.sum(ee, axis=1, keepdims=True), -1.0)
    p0 = jnp.max(pe, axis=1, keepdims=True)
    l0 = first_lane(pe == p0)
    pe1 = jnp.where(lane == l0, -1.0, pe)
    p1 = jnp.max(pe1, axis=1, keepdims=True)
    l1 = first_lane(pe1 == p1)
    psum = p0 + p1
    w0 = pg * p0 / psum
    w1 = pg * p1 / psum
    onehot = ((lane == l0) | (lane == l1)).astype(BF16)
    r_i = lax.broadcasted_iota(jnp.int32, (tm, tm), 0)
    c_i = lax.broadcasted_iota(jnp.int32, (tm, tm), 1)
    tri = (c_i < r_i).astype(BF16)
    before = jnp.dot(tri, onehot, preferred_element_type=F32) + carry[...]
    rank0 = jnp.sum(jnp.where(lane == l0, before, 0.0), axis=1, keepdims=True)
    rank1 = jnp.sum(jnp.where(lane == l1, before, 0.0), axis=1, keepdims=True)
    carry[...] = carry[...] + jnp.sum(onehot.astype(F32), axis=0, keepdims=True)
    cnt_ref[...] = carry[...]
    cols = [(l0 - _R_LOGIT0).astype(F32), (l1 - _R_LOGIT0).astype(F32), w0, w1, rank0, rank1]
    route = jnp.zeros(logits.shape, F32)
    for c, val in enumerate(cols):
        route = jnp.where(lane == c, val, route)
    route_ref[...] = route


def _router(x, g, mods, layer, w_route, b_route, tm=512):
    n, d = x.shape
    w_hi = w_route.astype(BF16)
    w_lo = (w_route - w_hi.astype(F32)).astype(BF16)
    return pl.pallas_call(
        _router_kernel,
        out_shape=(jax.ShapeDtypeStruct((n, d), F32),
                   jax.ShapeDtypeStruct((n, ROUTE_LANES), F32),
                   jax.ShapeDtypeStruct((1, ROUTE_LANES), F32)),
        grid=(n // tm,),
        in_specs=[pl.BlockSpec((tm, d), lambda i: (i, 0)),
                  pl.BlockSpec((1, d), lambda i: (0, 0)),
                  mods.spec(layer, 3, tm),
                  mods.spec(layer, 4, tm),
                  pl.BlockSpec((d, ROUTE_LANES), lambda i: (0, 0)),
                  pl.BlockSpec((d, ROUTE_LANES), lambda i: (0, 0)),
                  pl.BlockSpec((1, ROUTE_LANES), lambda i: (0, 0))],
        out_specs=(pl.BlockSpec((tm, d), lambda i: (i, 0)),
                   pl.BlockSpec((tm, ROUTE_LANES), lambda i: (i, 0)),
                   pl.BlockSpec((1, ROUTE_LANES), lambda i: (0, 0))),
        scratch_shapes=[pltpu.VMEM((1, ROUTE_LANES), F32)],
        compiler_params=_cparams(("arbitrary",)),
        name="moe_router",
    )(x, g.reshape(1, d), mods.mods, mods.mods, w_hi, w_lo, b_route)


def _dispatch_kernel(pcnt_ref, pend_ref, dest_ref, h_ref, xs_ref, zbuf, sem, zsem):
    tm = h_ref.shape[0]
    zrows = zbuf.shape[0]

    @pl.when(pl.program_id(0) == 0)
    def _():
        zbuf[...] = jnp.zeros_like(zbuf)

        def clear(e, _):
            @pl.when(pcnt_ref[e] > 0)
            def _():
                start = pl.multiple_of(pend_ref[e] - zrows, zrows)
                cp = pltpu.make_async_copy(zbuf, xs_ref.at[pl.ds(start, zrows)], zsem)
                cp.start()
                cp.wait()
            return 0

        lax.fori_loop(0, MOE_EXPERTS, clear, 0)

    def issue(r, _):
        for k in range(2):
            d = dest_ref[0, 0, 2 * r + k]
            pltpu.make_async_copy(h_ref.at[pl.ds(r, 1)], xs_ref.at[pl.ds(d, 1)], sem).start(priority=k)
        return 0

    lax.fori_loop(0, tm, issue, 0, unroll=ROW_ISSUE_UNROLL)
    for _ in range(2):
        pltpu.make_async_copy(h_ref, xs_ref.at[pl.ds(0, tm)], sem).wait()


def _dispatch(h, dest, pcnt, pends, n_pad, tm=ROW_TM):
    n, d = h.shape
    nblk = n // tm
    grid_spec = pltpu.PrefetchScalarGridSpec(
        num_scalar_prefetch=2,
        grid=(nblk,),
        in_specs=[pl.BlockSpec((1, 1, 2 * tm), lambda i, *_: (i, 0, 0), memory_space=pltpu.SMEM),
                  pl.BlockSpec((tm, d), lambda i, *_: (i, 0))],
        out_specs=pl.BlockSpec(memory_space=pl.ANY),
        scratch_shapes=[pltpu.VMEM((MOE_TM, d), F32), pltpu.SemaphoreType.DMA(()),
                        pltpu.SemaphoreType.DMA(())],
    )
    return pl.pallas_call(
        _dispatch_kernel,
        out_shape=jax.ShapeDtypeStruct((n_pad, d), F32),
        grid_spec=grid_spec,
        compiler_params=_cparams(("arbitrary",)),
        name="moe_dispatch",
    )(pcnt, pends, dest.reshape(nblk, 1, 2 * tm), h)


def _experts_kernel(be_ref, first_ref, nxt_ref, nu_ref, x_ref, wg_hbm, wu_hbm, wd_hbm, o_ref,
                    stg_g, stg_u, stg_d, wg_bf, wu_bf, wd_bf, sem, *, layer, cast_rows):
    i = pl.program_id(0)
    active = i < nu_ref[0]
    stages = ((wg_hbm, stg_g, wg_bf), (wu_hbm, stg_u, wu_bf), (wd_hbm, stg_d, wd_bf))

    def weight_copies(e):
        return [pltpu.make_async_copy(hbm.at[layer, e], stg, sem.at[k])
                for k, (hbm, stg, _) in enumerate(stages)]

    @pl.when(i == 0)
    def _():
        for cp in weight_copies(be_ref[0]):
            cp.start()

    @pl.when(active & (first_ref[i] == 1))
    def _():
        for cp in weight_copies(be_ref[i]):
            cp.wait()
        for _, stg, wbf in stages:
            def cast(r, _, stg=stg, wbf=wbf):
                rows = pl.ds(pl.multiple_of(r * cast_rows, cast_rows), cast_rows)
                wbf[rows, :] = stg[rows, :].astype(BF16)
                return 0

            lax.fori_loop(0, stg.shape[0] // cast_rows, cast, 0)

        @pl.when(nxt_ref[i] >= 0)
        def _():
            for cp in weight_copies(nxt_ref[i]):
                cp.start()

    @pl.when(active)
    def _():
        x = x_ref[...].astype(BF16)
        g = jnp.dot(x, wg_bf[...], preferred_element_type=F32)
        u = jnp.dot(x, wu_bf[...], preferred_element_type=F32)
        a = (_silu(g) * u).astype(BF16)
        o_ref[...] = jnp.dot(a, wd_bf[...], preferred_element_type=F32)

    @pl.when(jnp.logical_not(active))
    def _():
        o_ref[...] = jnp.zeros_like(o_ref)


def _experts(xs, block_expert, block_first, block_next, n_used, w_gate, w_up, w_down, layer, tm=MOE_TM):
    n_pad, d = xs.shape
    f = w_gate.shape[-1]
    nblk = n_pad // tm
    grid_spec = pltpu.PrefetchScalarGridSpec(
        num_scalar_prefetch=4,
        grid=(nblk,),
        in_specs=[pl.BlockSpec((tm, d), lambda i, be, fi, nx, nu: (jnp.minimum(i, nu[0] - 1), 0)),
                  pl.BlockSpec(memory_space=pl.ANY),
                  pl.BlockSpec(memory_space=pl.ANY),
                  pl.BlockSpec(memory_space=pl.ANY)],
        out_specs=pl.BlockSpec((tm, d), lambda i, *_: (i, 0)),
        scratch_shapes=[pltpu.VMEM((d, f), F32), pltpu.VMEM((d, f), F32), pltpu.VMEM((f, d), F32),
                        pltpu.VMEM((d, f), BF16), pltpu.VMEM((d, f), BF16), pltpu.VMEM((f, d), BF16),
                        pltpu.SemaphoreType.DMA((3,))],
    )
    return pl.pallas_call(
        functools.partial(_experts_kernel, layer=layer, cast_rows=256),
        out_shape=jax.ShapeDtypeStruct((n_pad, d), F32),
        grid_spec=grid_spec,
        compiler_params=_cparams(("arbitrary",), vmem=EXPERT_VMEM_LIMIT_BYTES),
        name="moe_experts",
    )(block_expert, block_first, block_next, n_used, xs, w_gate, w_up, w_down)


def _combine_kernel(dest_ref, dnext_ref, y_hbm, x_ref, route_ref, gate_ref, o_ref, buf, sem):
    i = pl.program_id(0)
    nblk = pl.num_programs(0)
    tm = x_ref.shape[0]

    def issue_block(idx_ref, slot):
        def issue(r, _):
            for k in range(2):
                d = idx_ref[0, 0, 2 * r + k]
                pltpu.make_async_copy(y_hbm.at[pl.ds(d, 1)], buf.at[slot, k, pl.ds(r, 1)],
                                      sem.at[slot]).start(priority=k)
            return 0

        lax.fori_loop(0, tm, issue, 0, unroll=ROW_ISSUE_UNROLL)

    @pl.when(i == 0)
    def _():
        issue_block(dest_ref, 0)

    @pl.when(i + 1 < nblk)
    def _():
        issue_block(dnext_ref, (i + 1) % 2)

    slot = i % 2
    for k in range(2):
        pltpu.make_async_copy(y_hbm.at[pl.ds(0, tm)], buf.at[slot, k], sem.at[slot]).wait()
    route = route_ref[...]
    w0 = route[:, _R_W:_R_W + 1]
    w1 = route[:, _R_W + 1:_R_W + 2]
    o_ref[...] = x_ref[...] + gate_ref[...] * (buf[slot, 0] * w0 + buf[slot, 1] * w1)


def _combine(x, y, dest, route, mods, layer, tm=ROW_TM):
    n, d = x.shape
    nblk = n // tm
    dest3 = dest.reshape(nblk, 1, 2 * tm)
    return pl.pallas_call(
        _combine_kernel,
        out_shape=jax.ShapeDtypeStruct((n, d), F32),
        grid=(nblk,),
        in_specs=[pl.BlockSpec((1, 1, 2 * tm), lambda i: (i, 0, 0), memory_space=pltpu.SMEM),
                  pl.BlockSpec((1, 1, 2 * tm), lambda i: (jnp.minimum(i + 1, nblk - 1), 0, 0),
                               memory_space=pltpu.SMEM),
                  pl.BlockSpec(memory_space=pl.ANY),
                  pl.BlockSpec((tm, d), lambda i: (i, 0)),
                  pl.BlockSpec((tm, ROUTE_LANES), lambda i: (i, 0)),
                  mods.spec(layer, 5, tm)],
        out_specs=pl.BlockSpec((tm, d), lambda i: (i, 0)),
        scratch_shapes=[pltpu.VMEM((2, 2, tm, d), F32), pltpu.SemaphoreType.DMA((2,))],
        compiler_params=_cparams(("arbitrary",)),
        name="moe_combine",
    )(dest3, dest3, y, x, route, mods.mods)


def _moe_layer(x, mods, layer, norm_g, w_group, b_group, w_expert, b_expert, w_gate, w_up, w_down):
    n, d = x.shape
    ne = MOE_EXPERTS
    pad = ROUTE_LANES - MOE_GROUPS - ne
    w_route = jnp.concatenate([w_group, w_expert, jnp.zeros((d, pad), F32)], axis=1)
    b_route = jnp.concatenate([b_group, b_expert, jnp.zeros((pad,), F32)]).reshape(1, ROUTE_LANES)
    h, route, counts = _router(x, norm_g, mods, layer, w_route, b_route)
    cnt = counts[0, _R_LOGIT0:_R_LOGIT0 + ne].astype(jnp.int32)
    pcnt = (cnt + MOE_TM - 1) // MOE_TM * MOE_TM
    pends = jnp.cumsum(pcnt)
    pstarts = pends - pcnt
    experts = jnp.arange(ne, dtype=jnp.int32)
    eid = route[:, _R_EID:_R_EID + 2].astype(jnp.int32)
    rank = route[:, _R_RANK:_R_RANK + 2].astype(jnp.int32)
    dest = (jnp.sum(jnp.where(eid[..., None] == experts, pstarts, 0), axis=-1) + rank).reshape(-1)
    n_blocks = (n * 2) // MOE_TM + ne
    n_pad = n_blocks * MOE_TM
    block_row = jnp.arange(n_blocks, dtype=jnp.int32) * MOE_TM
    block_expert = jnp.minimum(jnp.sum((pends[None, :] <= block_row[:, None]).astype(jnp.int32), axis=1), ne - 1)
    block_first = jnp.concatenate([jnp.ones((1,), jnp.int32),
                                   (block_expert[1:] != block_expert[:-1]).astype(jnp.int32)])
    later = (experts[None, :] > experts[:, None]) & (pcnt[None, :] > 0)
    next_expert = jnp.min(jnp.where(later, experts[None, :], ne), axis=1)
    next_expert = jnp.where(next_expert == ne, -1, next_expert)
    block_next = jnp.sum(jnp.where(block_expert[:, None] == experts, next_expert, 0), axis=1)
    n_used = (pends[-1:] // MOE_TM).astype(jnp.int32)
    xs = _dispatch(h, dest, pcnt, pends, n_pad)
    y = _experts(xs, block_expert, block_first, block_next, n_used, w_gate, w_up, w_down, layer)
    return _combine(x, y, dest, route, mods, layer)


class _Streams:
    def __init__(self, batch, seq, dec_batch, dec_seq):
        self.ctx = (0, batch, seq)
        self.lat = (batch * seq, dec_batch, dec_seq)


def _s5_layer(x, mods, layer, streams, norm_g, lam_re, lam_im, log_step, b_re, b_im, c_re, c_im, d_skip,
              w_a, b_a, w_b, b_b, s0_re, s0_im):
    n, d = x.shape
    g, p = lam_re.shape[1:]
    operands = _s5_params(lam_re, lam_im, log_step, b_re, b_im, c_re, c_im)
    h = _adanorm(x, norm_g, mods, layer, 0, 1)
    dvec = d_skip.reshape(1, d)
    row0, nseq, seq = streams.ctx
    zero = jnp.zeros((2, d // LANES, nseq, operands[3].shape[-1]), F32)
    z, sf = _s5_scan(h, operands, dvec, zero, row0, nseq, seq)
    row0, nseq, seq = streams.lat
    z, _ = _s5_scan(h, operands, dvec, _s5_state_to_lanes(s0_re, s0_im), row0, nseq, seq, z_prev=z)
    x = _mm_residual(z, [w_a.astype(BF16), w_b.astype(BF16)], [b_a, b_b], x, mods, layer, 2)
    new_re, new_im = _s5_state_from_lanes(sf, g, p)
    return x, new_re, new_im


def _mlstm_layer(x, mods, layer, streams, norm_g, w_in, b_gates, head_g, w_out, c0, n0, m0):
    n, d = x.shape
    hh = ML_HEADS
    dv = d // hh
    dk = dv // 2
    qk = hh * dk
    n_main = 2 * qk + 2 * d
    col_scale = jnp.concatenate([jnp.ones((qk,), F32), jnp.full((qk,), dk ** -0.5, F32),
                                 jnp.ones((2 * d,), F32)])
    w_main = (w_in[:, :n_main] * col_scale).astype(BF16)
    w_gates = jnp.pad(w_in[:, n_main:], ((0, 0), (0, LANES - 4 * hh)))
    proj = _mm_adanorm(x, norm_g, mods, layer, 0, 1, w_main, BF16)
    gates = _mm_adanorm(x, norm_g, mods, layer, 0, 1, w_gates, F32)
    k_t = jnp.transpose(proj[:, qk:2 * qk])
    g_c, w_c, e_c, d_r, w_r = _gate_prep(gates, b_gates, ML_T)
    fwd, bwd = slice(_G_FWD, _G_FWD + hh), slice(_G_BWD, _G_BWD + hh)
    cols = jnp.stack([g_c[:, fwd], w_c[:, fwd], e_c[:, fwd], g_c[:, bwd], w_c[:, bwd], e_c[:, bwd]], axis=-1)
    cols = jnp.pad(cols, ((0, 0), (0, 0), (0, 2)))
    rows = jnp.stack([d_r[fwd], w_r[fwd], d_r[bwd], w_r[bwd]], axis=1)

    def gate_views(row0, nseq, seq):
        gc = cols[row0:row0 + nseq * seq].reshape(nseq, seq, hh, 8)
        gr = rows[:, :, row0:row0 + nseq * seq].reshape(hh, 4, nseq, seq)
        return jnp.transpose(gc, (0, 2, 1, 3)), jnp.transpose(gr, (2, 0, 1, 3))

    def n_lanes(nv):
        return jnp.pad(jnp.transpose(nv, (0, 2, 1, 3))[..., None], ((0, 0),) * 4 + ((0, LANES - 1),))

    def m_lanes(mv):
        return jnp.broadcast_to(jnp.transpose(mv, (0, 2, 1))[..., None], mv.shape[:1] + (hh, 2, LANES))

    row0, nseq, seq = streams.ctx
    gcol, grow = gate_views(row0, nseq, seq)
    hn, c_f, n_f, m_f = _mlstm(
        proj, k_t, gcol, grow, head_g,
        jnp.zeros((nseq, 2, hh, dk, dv), F32), jnp.zeros((nseq, hh, 2, dk, LANES), F32),
        jnp.full((nseq, hh, 2, LANES), NEG_BIG, F32), row0, nseq, seq)
    row0, nseq, seq = streams.lat
    gcol, grow = gate_views(row0, nseq, seq)
    hn, _, _, _ = _mlstm(proj, k_t, gcol, grow, head_g, c0, n_lanes(n0), m_lanes(m0),
                         row0, nseq, seq, hn_prev=hn)
    x = _mm_residual(hn, [w_out.astype(BF16)], None, x, mods, layer, 2)
    return x, c_f, jnp.transpose(n_f[..., 0], (0, 2, 1, 3)), jnp.transpose(m_f[..., 0], (0, 2, 1))


def kernel(x_prompt, x_sample, state_s5_re, state_s5_im, state_mlstm_C, state_mlstm_n, state_mlstm_m, c, c_ctx, w_ada, b_ada, norm1_g, norm2_g, final_norm_g, s5_lambda_re, s5_lambda_im, s5_log_step, s5_b_re, s5_b_im, s5_c_re, s5_c_im, s5_d, s5_w_glu_a, s5_b_glu_a, s5_w_glu_b, s5_b_glu_b, ml_w_in, ml_b_gates, ml_head_norm_g, ml_w_out, moe_w_group, moe_b_group, moe_w_expert, moe_b_expert, moe_w_gate, moe_w_up, moe_w_down):
    batch, seq, d = x_prompt.shape
    dec_batch, dec_seq, _ = x_sample.shape
    depth = w_ada.shape[0]
    n_ctx = batch * seq
    streams = _Streams(batch, seq, dec_batch, dec_seq)

    cond = jnp.concatenate([c_ctx[None, :], c, jnp.zeros((MOD_ROWS - 1 - dec_batch, d), F32)], axis=0)
    mods = _modulation(cond, w_ada, b_ada).reshape(depth * MOD_ROWS * N_MOD, 1, d)
    mods = _Mods(mods, n_ctx, dec_seq)

    x = jnp.concatenate([x_prompt.reshape(n_ctx, d), _embed(x_sample).reshape(dec_batch * dec_seq, d)], axis=0)
    new_s5_re, new_s5_im, new_c, new_n, new_m = [], [], [], [], []
    for l in range(depth):
        j = l // N_MIXERS
        if l % N_MIXERS == 0:
            x, s_re, s_im = _s5_layer(
                x, mods, l, streams, norm1_g[l], s5_lambda_re[j], s5_lambda_im[j], s5_log_step[j],
                s5_b_re[j], s5_b_im[j], s5_c_re[j], s5_c_im[j], s5_d[j],
                s5_w_glu_a[j], s5_b_glu_a[j], s5_w_glu_b[j], s5_b_glu_b[j],
                state_s5_re[:, j], state_s5_im[:, j])
            new_s5_re.append(s_re)
            new_s5_im.append(s_im)
        else:
            x, c_f, n_f, m_f = _mlstm_layer(
                x, mods, l, streams, norm1_g[l], ml_w_in[j], ml_b_gates[j], ml_head_norm_g[j], ml_w_out[j],
                state_mlstm_C[:, j], state_mlstm_n[:, j], state_mlstm_m[:, j])
            new_c.append(c_f)
            new_n.append(n_f)
            new_m.append(m_f)
        x = _moe_layer(x, mods, l, norm2_g[l], moe_w_group[l], moe_b_group[l], moe_w_expert[l],
                       moe_b_expert[l], moe_w_gate, moe_w_up, moe_w_down)
    y_prompt = _final_norm(x, final_norm_g, 0, n_ctx).reshape(batch, seq, d)
    y_sample = _final_norm(x, final_norm_g, n_ctx, dec_batch * dec_seq).reshape(dec_batch, dec_seq, d)
    return (y_prompt, y_sample, jnp.stack(new_s5_re, axis=1), jnp.stack(new_s5_im, axis=1),
            jnp.stack(new_c, axis=1), jnp.stack(new_n, axis=1), jnp.stack(new_m, axis=1))
```

```python
import functools
import math

import jax
import jax.numpy as jnp
from jax import lax
from jax.experimental import pallas as pl
from jax.experimental.pallas import tpu as pltpu

F32 = jnp.float32
BF16 = jnp.bfloat16

S5_GROUP_CH = 16
ML_HEADS = 8
MOE_GROUPS = 4
MOE_EPG = 8
MOE_EXPERTS = MOE_GROUPS * MOE_EPG
GRID_W = 64
POS_BASE = 10000.0
RMS_EPS = 1e-6
NEG_BIG = -1e30
N_MIXERS = 2
N_MOD = 6

LANES = 128
SUBLANES = 8
VMEM_LIMIT_BYTES = 56 * 1024 * 1024
EXPERT_VMEM_LIMIT_BYTES = 60 * 1024 * 1024

MOD_ROWS = 16
S5_LANE_GROUPS = LANES // S5_GROUP_CH
S5_TC = 64
ML_T = 256
ML_HP = 2
ROUTE_LANES = LANES
MOE_TM = 256
ROW_TM = 256
ROW_ISSUE_UNROLL = 8


def _cparams(sem, vmem=VMEM_LIMIT_BYTES):
    return pltpu.CompilerParams(dimension_semantics=sem, vmem_limit_bytes=vmem)


def _silu(x):
    return x * jax.nn.sigmoid(x)


def _gelu_tanh(x):
    c = math.sqrt(2.0 / math.pi)
    return 0.5 * x * (1.0 + jnp.tanh(c * (x + 0.044715 * (x * x * x))))


def _ada_norm_tile(x, g, shift, scale):
    r = lax.rsqrt(jnp.mean(x * x, axis=-1, keepdims=True) + RMS_EPS)
    return (x * r * g) * (1.0 + scale) + shift


def _mod_kernel(c_ref, w_ref, b_ref, o_ref):
    o_ref[...] = _dot_3pass(_silu(c_ref[...]), w_ref[...]) + b_ref[...]


def _modulation(cond, w_ada, b_ada, tn=1024):
    depth, d, n = w_ada.shape
    return pl.pallas_call(
        _mod_kernel,
        out_shape=jax.ShapeDtypeStruct((depth, MOD_ROWS, n), F32),
        grid=(depth, n // tn),
        in_specs=[
            pl.BlockSpec((MOD_ROWS, d), lambda l, j: (0, 0)),
            pl.BlockSpec((None, d, tn), lambda l, j: (l, 0, j)),
            pl.BlockSpec((None, 1, tn), lambda l, j: (l, 0, j)),
        ],
        out_specs=pl.BlockSpec((None, MOD_ROWS, tn), lambda l, j: (l, 0, j)),
        compiler_params=_cparams(("parallel", "parallel")),
        name="modulation",
    )(cond, w_ada, b_ada.reshape(depth, 1, n))


class _Mods:
    def __init__(self, mods, n_ctx, dec_seq):
        self.mods = mods
        self.n_ctx = n_ctx
        self.dec_seq = dec_seq
        self.d = mods.shape[-1]

    def spec(self, layer, k, tm, row_offset=0):
        base = layer * MOD_ROWS * N_MOD + k
        n_ctx, dec_seq = self.n_ctx, self.dec_seq

        def index(i, *_):
            row0 = i * tm + row_offset
            r = jnp.where(row0 < n_ctx, 0, 1 + (row0 - n_ctx) // dec_seq)
            return (base + r * N_MOD, 0, 0)

        return pl.BlockSpec((None, 1, self.d), index)


def _embed_kernel(xc_ref, xl_ref, p_ref, o_ref, *, ctx_blocks):
    i = pl.program_id(0)

    @pl.when(i < ctx_blocks)
    def _():
        o_ref[...] = xc_ref[...]

    @pl.when(i >= ctx_blocks)
    def _():
        o_ref[...] = xl_ref[...] + p_ref[...]


def _grid_pos_embed(n_tokens, d):
    rows = n_tokens // GRID_W
    r, col = jnp.meshgrid(jnp.arange(rows, dtype=F32), jnp.arange(GRID_W, dtype=F32), indexing="ij")
    quarter = d // 4
    omega = 1.0 / (POS_BASE ** (jnp.arange(quarter, dtype=F32) / quarter))

    def axis_embed(pos):
        ang = pos.reshape(-1, 1) * omega[None, :]
        return jnp.concatenate([jnp.sin(ang), jnp.cos(ang)], axis=-1)

    return jnp.concatenate([axis_embed(r), axis_embed(col)], axis=-1)


def _embed(x_prompt, x_sample, tm=512):
    batch, seq, d = x_prompt.shape
    dec_batch, dec_seq, _ = x_sample.shape
    n_ctx, n_lat = batch * seq, dec_batch * dec_seq
    ctx_blocks, seq_blocks = n_ctx // tm, dec_seq // tm
    pos = _grid_pos_embed(dec_seq, d)
    return pl.pallas_call(
        functools.partial(_embed_kernel, ctx_blocks=ctx_blocks),
        out_shape=jax.ShapeDtypeStruct((n_ctx + n_lat, d), F32),
        grid=((n_ctx + n_lat) // tm,),
        in_specs=[pl.BlockSpec((tm, d), lambda i: (jnp.minimum(i, ctx_blocks - 1), 0)),
                  pl.BlockSpec((tm, d), lambda i: (jnp.maximum(i - ctx_blocks, 0), 0)),
                  pl.BlockSpec((tm, d), lambda i: (jnp.maximum(i - ctx_blocks, 0) % seq_blocks, 0))],
        out_specs=pl.BlockSpec((tm, d), lambda i: (i, 0)),
        compiler_params=_cparams(("parallel",)),
        name="embed",
    )(x_prompt.reshape(n_ctx, d), x_sample.reshape(n_lat, d), pos)


def _adanorm_kernel(x_ref, g_ref, sh_ref, sc_ref, o_ref):
    o_ref[...] = _ada_norm_tile(x_ref[...], g_ref[...], sh_ref[...], sc_ref[...]).astype(o_ref.dtype)


def _adanorm(x, g, mods, layer, k_shift, k_scale, tm=512):
    n, d = x.shape
    return pl.pallas_call(
        _adanorm_kernel,
        out_shape=jax.ShapeDtypeStruct((n, d), F32),
        grid=(n // tm,),
        in_specs=[pl.BlockSpec((tm, d), lambda i: (i, 0)),
                  pl.BlockSpec((1, d), lambda i: (0, 0)),
                  mods.spec(layer, k_shift, tm),
                  mods.spec(layer, k_scale, tm)],
        out_specs=pl.BlockSpec((tm, d), lambda i: (i, 0)),
        compiler_params=_cparams(("parallel",)),
        name="adanorm",
    )(x, g.reshape(1, d), mods.mods, mods.mods)


def _final_norm_kernel(x_ref, g_ref, o_ref):
    x = x_ref[...]
    r = lax.rsqrt(jnp.mean(x * x, axis=-1, keepdims=True) + RMS_EPS)
    o_ref[...] = x * r * g_ref[...]


def _final_norm(x, g, row0, nrows, tm=512):
    d = x.shape[1]
    off = row0 // tm
    return pl.pallas_call(
        _final_norm_kernel,
        out_shape=jax.ShapeDtypeStruct((nrows, d), F32),
        grid=(nrows // tm,),
        in_specs=[pl.BlockSpec((tm, d), lambda i: (i + off, 0)),
                  pl.BlockSpec((1, d), lambda i: (0, 0))],
        out_specs=pl.BlockSpec((tm, d), lambda i: (i, 0)),
        compiler_params=_cparams(("parallel",)),
        name="final_norm",
    )(x, g.reshape(1, d))


def _cmul(ar, ai, br, bi):
    return ar * br - ai * bi, ar * bi + ai * br


def _s5_param_kernel(lre_ref, lim_ref, ls_ref, bre_ref, bim_ref, cre_ref, cim_ref,
                     b2_ref, c2_ref, d2_ref, a2_ref):
    h, s = bre_ref.shape
    gl = LANES // h
    p = s // gl
    lr = lre_ref[...]
    li = lim_ref[...]
    dt = jnp.exp(ls_ref[...])
    mag = jnp.exp(lr * dt)
    ar = mag * jnp.cos(li * dt)
    ai = mag * jnp.sin(li * dt)
    den = lr * lr + li * li
    zr = ((ar - 1.0) * lr + ai * li) / den
    zi = (ai * lr - (ar - 1.0) * li) / den
    a2 = _cmul(ar, ai, ar, ai)
    bb = _cmul(zr, zi, bre_ref[...], bim_ref[...])
    abb = _cmul(ar, ai, *bb)
    cc = (cre_ref[...], cim_ref[...])
    ca = _cmul(*cc, ar, ai)
    ca2 = _cmul(*cc, *a2)

    same_b = (lax.broadcasted_iota(jnp.int32, (LANES, s), 0) // h
              == lax.broadcasted_iota(jnp.int32, (LANES, s), 1) // p)
    same_c = (lax.broadcasted_iota(jnp.int32, (s, LANES), 0) // p
              == lax.broadcasted_iota(jnp.int32, (s, LANES), 1) // h)
    spread = (lax.broadcasted_iota(jnp.int32, (h, LANES), 1) % h
              == lax.broadcasted_iota(jnp.int32, (h, LANES), 0)).astype(BF16)

    def bblock(re, im):
        def one(x):
            return jnp.where(same_b, jnp.concatenate([x] * gl, axis=0), 0.0)
        return jnp.concatenate([one(re), one(im)], axis=1)

    def cblock(re, im):
        def one(x):
            t = sum(lax.dot_general(part, spread, (((0,), (0,)), ((), ())), preferred_element_type=F32)
                    for part in _split_bf16(x))
            return jnp.where(same_c, t, 0.0)
        return jnp.concatenate([one(re), one(-im)], axis=0)

    b_blk = bblock(*bb)
    ab_blk = bblock(*abb)
    c_blk = cblock(*cc)
    cb = _dot_3pass(b_blk, c_blk)
    cab = _dot_3pass(ab_blk, c_blk)
    b2_ref[...] = jnp.concatenate([ab_blk, b_blk], axis=0).astype(BF16)
    c2_ref[...] = jnp.concatenate([cblock(*ca), cblock(*ca2)], axis=1).astype(BF16)
    d2_ref[...] = jnp.concatenate([jnp.concatenate([cb, cab], axis=1),
                                   jnp.concatenate([jnp.zeros_like(cb), cb], axis=1)], axis=0).astype(BF16)
    a2_ref[...] = jnp.concatenate(a2, axis=1)


def _s5_params(lam_re, lam_im, log_step, b_re, b_im, c_re, c_im):
    _, g, p = lam_re.shape
    h = b_re.shape[-1]
    gl = S5_LANE_GROUPS
    nlc = g // gl
    s = gl * p

    def lanes(v):
        return v.reshape(2, nlc, 1, s)

    def rows(m):
        return jnp.transpose(m.reshape(2, h, nlc, s), (0, 2, 1, 3))

    step = jnp.broadcast_to(log_step[:, :, None], (2, g, p))
    vec = pl.BlockSpec((None, None, 1, s), lambda d, j: (d, j, 0, 0))
    mat = pl.BlockSpec((None, None, h, s), lambda d, j: (d, j, 0, 0))

    def out(r, c, dt):
        return (jax.ShapeDtypeStruct((2, nlc, r, c), dt),
                pl.BlockSpec((None, None, r, c), lambda d, j: (d, j, 0, 0)))

    outs = [out(2 * LANES, 2 * s, BF16), out(2 * s, 2 * LANES, BF16), out(2 * LANES, 2 * LANES, BF16),
            out(1, 2 * s, F32)]
    return pl.pallas_call(
        _s5_param_kernel,
        out_shape=tuple(o[0] for o in outs),
        grid=(2, nlc),
        in_specs=[vec, vec, vec, mat, mat, mat, mat],
        out_specs=tuple(o[1] for o in outs),
        compiler_params=_cparams(("parallel", "parallel")),
        name="s5_params",
    )(lanes(lam_re), lanes(lam_im), lanes(step),
      rows(jnp.transpose(b_re, (0, 3, 1, 2))), rows(jnp.transpose(b_im, (0, 3, 1, 2))),
      rows(jnp.transpose(c_re, (0, 2, 1, 3))), rows(jnp.transpose(c_im, (0, 2, 1, 3))))


def _s5_state_to_lanes(s_re, s_im):
    b, _, g, p = s_re.shape
    gl = S5_LANE_GROUPS
    nlc = g // gl

    def lay(s):
        return jnp.transpose(s.reshape(b, 2, nlc, gl * p), (1, 2, 0, 3))

    return jnp.concatenate([lay(s_re), lay(s_im)], axis=-1)


def _s5_state_from_lanes(s, g, p):
    _, nlc, b, s2 = s.shape
    half = s2 // 2

    def unlay(t):
        return jnp.transpose(t, (2, 0, 1, 3)).reshape(b, 2, g, p)

    return unlay(s[..., :half]), unlay(s[..., half:])


def _s5_scan_kernel(*refs, seq, tc, aliased):
    if aliased:
        h_ref, b2_ref, c2_ref, d2_ref, a_ref, d_ref, s0_ref, _, z_ref, sf_ref = refs[:10]
    else:
        h_ref, b2_ref, c2_ref, d2_ref, a_ref, d_ref, s0_ref, z_ref, sf_ref = refs[:9]
    scr = (refs[-10:-5], refs[-5:])
    nb = SUBLANES
    half = a_ref.shape[-1] // 2
    nc = seq // tc
    npair = tc // 2
    a_re = [jnp.broadcast_to(a_ref[dr][:, :half], (nb, half)) for dr in range(2)]
    a_im = [jnp.broadcast_to(a_ref[dr][:, half:], (nb, half)) for dr in range(2)]

    def chunk(c, carry):
        t0s = (c * tc, (nc - 1 - c) * tc)

        def pair_rows(dr, p):
            if dr == 0:
                first = t0s[0] + 2 * p
                return pl.ds(first, nb, stride=seq), pl.ds(first + 1, nb, stride=seq)
            first = t0s[1] + tc - 1 - 2 * p
            return pl.ds(first, nb, stride=seq), pl.ds(first - 1, nb, stride=seq)

        carry = list(carry)
        for dr in range(2):
            u_scr, bu_scr, _, _, _ = scr[dr]
            for p in range(npair):
                r1, r2 = pair_rows(dr, p)
                u_scr[p * nb:(p + 1) * nb, :LANES] = h_ref[r1, :]
                u_scr[p * nb:(p + 1) * nb, LANES:] = h_ref[r2, :]
            bu_scr[...] = jnp.dot(u_scr[...].astype(BF16), b2_ref[dr], preferred_element_type=F32)
        for dr in range(2):
            _, bu_scr, x_scr, _, _ = scr[dr]
            xr, xi = carry[2 * dr], carry[2 * dr + 1]
            for p in range(npair):
                x_scr[p * nb:(p + 1) * nb, :half] = xr
                x_scr[p * nb:(p + 1) * nb, half:] = xi
                bu = bu_scr[p * nb:(p + 1) * nb, :]
                xr, xi = (a_re[dr] * xr - a_im[dr] * xi + bu[:, :half],
                          a_re[dr] * xi + a_im[dr] * xr + bu[:, half:])
            carry[2 * dr], carry[2 * dr + 1] = xr, xi
        for dr in range(2):
            u_scr, _, x_scr, yo_scr, y_scr = scr[dr]
            yo_scr[...] = jnp.dot(x_scr[...].astype(BF16), c2_ref[dr], preferred_element_type=F32) \
                + jnp.dot(u_scr[...].astype(BF16), d2_ref[dr], preferred_element_type=F32)
            for p in range(npair):
                r1, r2 = pair_rows(dr, p)
                y_scr[r1, :] = yo_scr[p * nb:(p + 1) * nb, :LANES]
                y_scr[r2, :] = yo_scr[p * nb:(p + 1) * nb, LANES:]
        return tuple(carry)

    s0f = s0_ref[0]
    s0b = s0_ref[1]
    init = (s0f[:, :half], s0f[:, half:], s0b[:, :half], s0b[:, half:])
    xr_f, xi_f, xr_b, xi_b = lax.fori_loop(0, nc, chunk, init)
    sf_ref[0, :, :half] = xr_f
    sf_ref[0, :, half:] = xi_f
    sf_ref[1, :, :half] = xr_b
    sf_ref[1, :, half:] = xi_b
    y = h_ref[...] * d_ref[...] + scr[0][4][...] + scr[1][4][...]
    z_ref[...] = _gelu_tanh(y).astype(z_ref.dtype)


def _s5_scan(h, operands, dvec, s0, row0, nseq, seq, z_prev=None):
    b2, c2, d2, avec = operands
    n, d = h.shape
    nlc = d // LANES
    s2 = avec.shape[-1]
    nb = SUBLANES
    rows = nb * seq
    off = row0 // rows
    aliased = z_prev is not None
    in_specs = [
        pl.BlockSpec((rows, LANES), lambda i, j: (i + off, j)),
        pl.BlockSpec((2, None, 2 * LANES, s2), lambda i, j: (0, j, 0, 0)),
        pl.BlockSpec((2, None, s2, 2 * LANES), lambda i, j: (0, j, 0, 0)),
        pl.BlockSpec((2, None, 2 * LANES, 2 * LANES), lambda i, j: (0, j, 0, 0)),
        pl.BlockSpec((2, None, 1, s2), lambda i, j: (0, j, 0, 0)),
        pl.BlockSpec((1, LANES), lambda i, j: (0, j)),
        pl.BlockSpec((2, None, nb, s2), lambda i, j: (0, j, i, 0)),
    ]
    args = [h, b2, c2, d2, avec, dvec, s0]
    io_alias = {}
    if aliased:
        in_specs.append(pl.BlockSpec(memory_space=pl.ANY))
        args.append(z_prev)
        io_alias = {len(args) - 1: 0}
    npair_rows = S5_TC // 2 * nb
    return pl.pallas_call(
        functools.partial(_s5_scan_kernel, seq=seq, tc=S5_TC, aliased=aliased),
        out_shape=(jax.ShapeDtypeStruct((n, d), BF16), jax.ShapeDtypeStruct((2, nlc, nseq, s2), F32)),
        grid=(nseq // nb, nlc),
        in_specs=in_specs,
        out_specs=(pl.BlockSpec((rows, LANES), lambda i, j: (i + off, j)),
                   pl.BlockSpec((2, None, nb, s2), lambda i, j: (0, j, i, 0))),
        scratch_shapes=2 * [pltpu.VMEM((npair_rows, 2 * LANES), F32),
                            pltpu.VMEM((npair_rows, s2), F32),
                            pltpu.VMEM((npair_rows, s2), F32),
                            pltpu.VMEM((npair_rows, 2 * LANES), F32),
                            pltpu.VMEM((rows, LANES), F32)],
        input_output_aliases=io_alias,
        compiler_params=_cparams(("parallel", "parallel")),
        name="s5_scan",
    )(*args)


def _mm_res_kernel(*refs, n_w, has_bias):
    z_ref = refs[0]
    w_refs = refs[1:1 + n_w]
    pos = 1 + n_w
    b_refs = refs[pos:pos + n_w] if has_bias else ()
    pos += n_w if has_bias else 0
    x_ref, gate_ref, o_ref = refs[pos:pos + 3]
    z = z_ref[...]
    acc = [jnp.dot(z, w[...], preferred_element_type=F32) for w in w_refs]
    if has_bias:
        acc = [a + b[...] for a, b in zip(acc, b_refs)]
    y = acc[0] if n_w == 1 else acc[0] * jax.nn.sigmoid(acc[1])
    o_ref[...] = x_ref[...] + gate_ref[...] * y


def _mm_residual(z, ws, bs, x, mods, layer, k_gate, tm=1024, tn=512):
    n, k = z.shape
    n_out = ws[0].shape[1]
    n_w = len(ws)
    has_bias = bs is not None
    in_specs = [pl.BlockSpec((tm, k), lambda i, j: (i, 0))]
    in_specs += [pl.BlockSpec((k, tn), lambda i, j: (0, j)) for _ in ws]
    args = [z, *ws]
    if has_bias:
        in_specs += [pl.BlockSpec((1, tn), lambda i, j: (0, j)) for _ in bs]
        args += [b.reshape(1, n_out) for b in bs]
    gate_spec = mods.spec(layer, k_gate, tm)
    gate_spec = pl.BlockSpec((None, 1, tn), lambda i, j, f=gate_spec.index_map: (f(i)[0], 0, j))
    in_specs += [pl.BlockSpec((tm, tn), lambda i, j: (i, j)), gate_spec]
    args += [x, mods.mods]
    return pl.pallas_call(
        functools.partial(_mm_res_kernel, n_w=n_w, has_bias=has_bias),
        out_shape=jax.ShapeDtypeStruct((n, n_out), F32),
        grid=(n // tm, n_out // tn),
        in_specs=in_specs,
        out_specs=pl.BlockSpec((tm, tn), lambda i, j: (i, j)),
        compiler_params=_cparams(("parallel", "parallel")),
        name=f"proj_residual_{n_w}w",
    )(*args)


def _split_bf16(a):
    hi = a.astype(BF16)
    return hi, (a - hi.astype(F32)).astype(BF16)


def _dot_3pass(a, b):
    a_hi, a_lo = _split_bf16(a)
    b_hi, b_lo = _split_bf16(b)
    return (jnp.dot(a_hi, b_hi, preferred_element_type=F32) + jnp.dot(a_hi, b_lo, preferred_element_type=F32)
            + jnp.dot(a_lo, b_hi, preferred_element_type=F32))


def _mm_norm_kernel(x_ref, g_ref, sh_ref, sc_ref, w_ref, ws_ref, o_ref, os_ref, h_scr):
    @pl.when(pl.program_id(1) == 0)
    def _():
        h = _ada_norm_tile(x_ref[...], g_ref[...], sh_ref[...], sc_ref[...])
        h_scr[...] = h.astype(h_scr.dtype)
        os_ref[...] = _dot_3pass(h, ws_ref[...])

    o_ref[...] = jnp.dot(h_scr[...], w_ref[...], preferred_element_type=F32).astype(o_ref.dtype)


def _mm_adanorm(x, g, mods, layer, k_shift, k_scale, w, w_side, tm=1024, tn=1024):
    n, d = x.shape
    n_out = w.shape[1]
    n_side = w_side.shape[1]
    return pl.pallas_call(
        _mm_norm_kernel,
        out_shape=(jax.ShapeDtypeStruct((n, n_out), BF16), jax.ShapeDtypeStruct((n, n_side), F32)),
        grid=(n // tm, n_out // tn),
        in_specs=[pl.BlockSpec((tm, d), lambda i, j: (i, 0)),
                  pl.BlockSpec((1, d), lambda i, j: (0, 0)),
                  mods.spec(layer, k_shift, tm),
                  mods.spec(layer, k_scale, tm),
                  pl.BlockSpec((d, tn), lambda i, j: (0, j)),
                  pl.BlockSpec((d, n_side), lambda i, j: (0, 0))],
        out_specs=(pl.BlockSpec((tm, tn), lambda i, j: (i, j)),
                   pl.BlockSpec((tm, n_side), lambda i, j: (i, 0))),
        scratch_shapes=[pltpu.VMEM((tm, d), BF16)],
        compiler_params=_cparams(("parallel", "arbitrary")),
        name="adanorm_proj",
    )(x, g.reshape(1, d), mods.mods, mods.mods, w, w_side)


_G_FWD = ML_HEADS
_G_BWD = 3 * ML_HEADS


def _gate_kernel(xc_ref, xr_ref, bc_ref, br_ref, g_ref, w_ref, e_ref, dr_ref, wr_ref):
    t = xc_ref.shape[0]
    r_i = lax.broadcasted_iota(jnp.int32, (t, t), 0)
    c_i = lax.broadcasted_iota(jnp.int32, (t, t), 1)
    lower = (c_i <= r_i).astype(F32)
    upper = (c_i >= r_i).astype(F32)
    hi = lax.Precision.HIGHEST
    xc = xc_ref[...] + bc_ref[...]
    fc = jax.nn.log_sigmoid(xc)
    lane = lax.broadcasted_iota(jnp.int32, xc.shape, 1)
    g_c = jnp.where(lane < 2 * ML_HEADS,
                    jnp.dot(lower, fc, preferred_element_type=F32, precision=hi),
                    jnp.dot(upper, fc, preferred_element_type=F32, precision=hi))
    e_c = jnp.broadcast_to(jnp.sum(fc, axis=0, keepdims=True), xc.shape)
    g_ref[...] = g_c
    e_ref[...] = e_c
    w_ref[...] = e_c - g_c + pltpu.roll(xc, ML_HEADS, 1)
    xr = xr_ref[...] + br_ref[...]
    fr = jax.nn.log_sigmoid(xr)
    row = lax.broadcasted_iota(jnp.int32, xr.shape, 0)
    g_r = jnp.where(row < 2 * ML_HEADS,
                    jnp.dot(fr, upper, preferred_element_type=F32, precision=hi),
                    jnp.dot(fr, lower, preferred_element_type=F32, precision=hi))
    i_r = pltpu.roll(xr, ML_HEADS, 0)
    dr_ref[...] = i_r - g_r
    wr_ref[...] = jnp.sum(fr, axis=1, keepdims=True) - g_r + i_r


def _gate_prep(gates, b_gates, tch):
    n = gates.shape[0]
    ng = 4 * ML_HEADS
    bias_c = jnp.pad(b_gates, (0, LANES - ng)).reshape(1, LANES)
    bias_r = b_gates.reshape(ng, 1)
    gates_r = jnp.transpose(gates[:, :ng])
    col = pl.BlockSpec((tch, LANES), lambda i: (i, 0))
    rowb = pl.BlockSpec((ng, tch), lambda i: (0, i))
    return pl.pallas_call(
        _gate_kernel,
        out_shape=(jax.ShapeDtypeStruct((n, LANES), F32),) * 3 + (jax.ShapeDtypeStruct((ng, n), F32),) * 2,
        grid=(n // tch,),
        in_specs=[col, rowb, pl.BlockSpec((1, LANES), lambda i: (0, 0)),
                  pl.BlockSpec((ng, 1), lambda i: (0, 0))],
        out_specs=(col, col, col, rowb, rowb),
        compiler_params=_cparams(("parallel",)),
        name="mlstm_gates",
    )(gates, gates_r, bias_c, bias_r)


def _mlstm_kernel(q_ref, kt_ref, v_ref, o_ref, gcol_ref, grow_ref, hg_ref,
                  c0_ref, n0_ref, m0_ref, hn_ref, cf_ref, nf_ref, mf_ref, *scratch, seq, tch, hp):
    nc = seq // tch
    dv = v_ref.shape[-1] // hp
    dk = q_ref.shape[-1] // hp
    tt = lax.broadcasted_iota(jnp.int32, (tch, tch), 0)
    ss = lax.broadcasted_iota(jnp.int32, (tch, tch), 1)
    ones_col = (lax.broadcasted_iota(jnp.int32, (tch, LANES), 1) == 0).astype(BF16)
    scr = {(hd, dr): scratch[3 * (2 * hd + dr):3 * (2 * hd + dr) + 3] for hd in range(hp) for dr in range(2)}
    for (hd, dr), (_, c_scr, vx) in scr.items():
        c_scr[:, :dv] = c0_ref[dr, hd]
        c_scr[:, dv:] = n0_ref[hd, dr]
        vx[:, dv:] = ones_col

    def chunk_dir(hd, dr, r0, m):
        hacc, c_scr, vx = scr[hd, dr]
        rows = pl.ds(pl.multiple_of(r0, tch), tch)
        q = q_ref[rows, hd * dk:(hd + 1) * dk]
        kt = kt_ref[hd * dk:(hd + 1) * dk, rows]
        vx[:, :dv] = v_ref[rows, hd * dv:(hd + 1) * dv]
        gc = gcol_ref[hd, rows, :]
        gr = grow_ref[hd, :, rows]
        g_col = gc[:, 3 * dr:3 * dr + 1]
        w_col = gc[:, 3 * dr + 1:3 * dr + 2]
        e_col = gc[:, 3 * dr + 2:3 * dr + 3]
        d_row = gr[2 * dr:2 * dr + 1, :]
        w_row = gr[2 * dr + 1:2 * dr + 2, :]
        mask = (ss <= tt) if dr == 0 else (ss >= tt)
        a_col = g_col + m
        dmat = jnp.where(mask, g_col + d_row, -jnp.inf)
        mt = jnp.maximum(a_col, jnp.max(dmat, axis=1, keepdims=True))
        qk = jnp.dot(q, kt, preferred_element_type=F32)
        s = (qk * jnp.exp(dmat - mt)).astype(BF16)
        inter = jnp.exp(a_col - mt)
        vext = vx[...]
        tot = jnp.dot(s, vext, preferred_element_type=F32) \
            + inter * jnp.dot(q, c_scr[...].astype(BF16), preferred_element_type=F32)
        den = tot[:, dv:dv + 1]
        hacc[rows, :] = tot[:, :dv] / jnp.maximum(jnp.abs(den), jnp.exp(-mt))
        g_end = jnp.max(e_col, axis=0, keepdims=True)
        m_new = jnp.maximum(g_end + m, jnp.max(w_col, axis=0, keepdims=True))
        decay = jnp.exp(g_end + m - m_new)
        kw = (kt.astype(F32) * jnp.exp(w_row - m_new)).astype(BF16)
        c_scr[...] = decay * c_scr[...] + jnp.dot(kw, vext, preferred_element_type=F32)
        return m_new

    keys = list(scr)

    def body(c, carry):
        return tuple(chunk_dir(hd, dr, (c if dr == 0 else nc - 1 - c) * tch, m)
                     for (hd, dr), m in zip(keys, carry))

    m_fin = lax.fori_loop(0, nc, body, tuple(m0_ref[hd, dr:dr + 1, 0:1] for hd, dr in keys))
    for (hd, dr), m in zip(keys, m_fin):
        c_scr = scr[hd, dr][1]
        cf_ref[dr, hd] = c_scr[:, :dv]
        nf_ref[hd, dr] = c_scr[:, dv:]
        mf_ref[hd, dr:dr + 1, :] = jnp.broadcast_to(m, (1, LANES))
    for hd in range(hp):
        cols = slice(hd * dv, (hd + 1) * dv)
        hs = scr[hd, 0][0][...] + scr[hd, 1][0][...]
        hn = hs * lax.rsqrt(jnp.mean(hs * hs, axis=-1, keepdims=True) + RMS_EPS)
        hn = hn * hg_ref[:, cols] * jax.nn.sigmoid(o_ref[:, cols].astype(F32))
        hn_ref[:, cols] = hn.astype(hn_ref.dtype)


def _mlstm(proj, k_t, gcol, grow, head_g, c0, n0, m0, row0, nseq, seq, hn_prev=None, c_slot=None, hp=ML_HP):
    n = proj.shape[0]
    h = ML_HEADS
    dk = c0.shape[-2]
    dv = c0.shape[-1]
    d = h * dv
    qk = h * dk
    off = row0 // seq
    tch = min(ML_T, seq)
    aliased = hn_prev is not None
    c_spec = pl.BlockSpec((None, 2, hp, dk, dv), lambda b, j: (b, 0, j, 0, 0))
    n_spec = pl.BlockSpec((None, hp, 2, dk, LANES), lambda b, j: (b, j, 0, 0, 0))
    m_spec = pl.BlockSpec((None, hp, 2, LANES), lambda b, j: (b, j, 0, 0))
    wk, wv = hp * dk, hp * dv
    in_specs = [
        pl.BlockSpec((seq, wk), lambda b, j: (b + off, j)),
        pl.BlockSpec((wk, seq), lambda b, j: (j, b + off)),
        pl.BlockSpec((seq, wv), lambda b, j: (b + off, 2 * qk // wv + j)),
        pl.BlockSpec((seq, wv), lambda b, j: (b + off, (2 * qk + d) // wv + j)),
        pl.BlockSpec((None, hp, seq, 8), lambda b, j: (b, j, 0, 0)),
        pl.BlockSpec((None, hp, 4, seq), lambda b, j: (b, j, 0, 0)),
        pl.BlockSpec((1, wv), lambda b, j: (0, j)),
        c_spec, n_spec, m_spec,
    ]
    args = [proj, k_t, proj, proj, gcol, grow, head_g.reshape(1, d), c0, n0, m0]
    n_in = len(args)
    io_alias = {}
    if aliased:
        in_specs.append(pl.BlockSpec(memory_space=pl.ANY))
        args.append(hn_prev)
        io_alias[len(args) - 1] = 0
    c_type = jax.ShapeDtypeStruct((nseq, 2, h, dk, dv), F32)
    c_out_spec = c_spec
    if c_slot is not None:
        slot, n_slots, c_prev = c_slot
        c_type = jax.ShapeDtypeStruct((nseq, n_slots, 2, h, dk, dv), F32)
        c_out_spec = pl.BlockSpec((None, None, 2, hp, dk, dv), lambda b, j: (b, slot, 0, j, 0, 0))
        if c_prev is not None:
            in_specs.append(pl.BlockSpec(memory_space=pl.ANY))
            args.append(c_prev)
            io_alias[len(args) - 1] = 1

    def kern(*refs):
        _mlstm_kernel(*refs[:n_in], *refs[len(args):], seq=seq, tch=tch, hp=hp)

    return pl.pallas_call(
        kern,
        out_shape=(jax.ShapeDtypeStruct((n, d), BF16), c_type,
                   jax.ShapeDtypeStruct((nseq, h, 2, dk, LANES), F32),
                   jax.ShapeDtypeStruct((nseq, h, 2, LANES), F32)),
        grid=(nseq, h // hp),
        in_specs=in_specs,
        out_specs=(pl.BlockSpec((seq, wv), lambda b, j: (b + off, j)), c_out_spec, n_spec, m_spec),
        scratch_shapes=2 * hp * [pltpu.VMEM((seq, dv), F32), pltpu.VMEM((dk, dv + LANES), F32),
                                 pltpu.VMEM((tch, dv + LANES), BF16)],
        input_output_aliases=io_alias,
        compiler_params=_cparams(("parallel", "parallel")),
        name="mlstm",
    )(*args)


_R_EID, _R_W, _R_RANK = 0, 2, 4
_R_LOGIT0 = MOE_GROUPS


def _router_kernel(x_ref, g_ref, sh_ref, sc_ref, whi_ref, wlo_ref, b_ref, h_ref, route_ref, cnt_ref, carry):
    i = pl.program_id(0)

    @pl.when(i == 0)
    def _():
        carry[...] = jnp.zeros_like(carry)

    h = _ada_norm_tile(x_ref[...], g_ref[...], sh_ref[...], sc_ref[...])
    h_ref[...] = h
    tm = h.shape[0]
    h_hi = h.astype(BF16)
    h_lo = (h - h_hi.astype(F32)).astype(BF16)
    logits = (jnp.dot(h_hi, whi_ref[...], preferred_element_type=F32)
              + jnp.dot(h_hi, wlo_ref[...], preferred_element_type=F32)
              + jnp.dot(h_lo, whi_ref[...], preferred_element_type=F32)) + b_ref[...]
    lane = lax.broadcasted_iota(jnp.int32, logits.shape, 1)
    big = jnp.int32(ROUTE_LANES)

    def first_lane(cond):
        return jnp.min(jnp.where(cond, lane, big), axis=1, keepdims=True)

    glog = jnp.where(lane < MOE_GROUPS, logits, -jnp.inf)
    ge = jnp.exp(glog - jnp.max(glog, axis=1, keepdims=True))
    pgrp = ge / jnp.sum(ge, axis=1, keepdims=True)
    pg = jnp.max(pgrp, axis=1, keepdims=True)
    grp = first_lane(pgrp == pg)
    e_lane = lane - _R_LOGIT0
    in_grp = (e_lane >= 0) & (e_lane < MOE_EXPERTS) & ((e_lane // MOE_EPG) == grp)
    elog = jnp.where(in_grp, logits, -jnp.inf)
    ee = jnp.exp(elog - jnp.max(elog, axis=1, keepdims=True))
    pe = jnp.where(in_grp, ee / jnp.sum(ee, axis=1, keepdims=True), -1.0)
    p0 = jnp.max(pe, axis=1, keepdims=True)
    l0 = first_lane(pe == p0)
    pe1 = jnp.where(lane == l0, -1.0, pe)
    p1 = jnp.max(pe1, axis=1, keepdims=True)
    l1 = first_lane(pe1 == p1)
    psum = p0 + p1
    w0 = pg * p0 / psum
    w1 = pg * p1 / psum
    onehot = ((lane == l0) | (lane == l1)).astype(BF16)
    r_i = lax.broadcasted_iota(jnp.int32, (tm, tm), 0)
    c_i = lax.broadcasted_iota(jnp.int32, (tm, tm), 1)
    tri = (c_i < r_i).astype(BF16)
    before = jnp.dot(tri, onehot, preferred_element_type=F32) + carry[...]
    rank0 = jnp.sum(jnp.where(lane == l0, before, 0.0), axis=1, keepdims=True)
    rank1 = jnp.sum(jnp.where(lane == l1, before, 0.0), axis=1, keepdims=True)
    carry[...] = carry[...] + jnp.sum(onehot.astype(F32), axis=0, keepdims=True)
    cnt_ref[...] = carry[...]
    cols = [(l0 - _R_LOGIT0).astype(F32), (l1 - _R_LOGIT0).astype(F32), w0, w1, rank0, rank1]
    route = jnp.zeros(logits.shape, F32)
    for c, val in enumerate(cols):
        route = jnp.where(lane == c, val, route)
    route_ref[...] = route


def _router(x, g, mods, layer, w_route, b_route, tm=512):
    n, d = x.shape
    w_hi = w_route.astype(BF16)
    w_lo = (w_route - w_hi.astype(F32)).astype(BF16)
    return pl.pallas_call(
        _router_kernel,
        out_shape=(jax.ShapeDtypeStruct((n, d), F32),
                   jax.ShapeDtypeStruct((n, ROUTE_LANES), F32),
                   jax.ShapeDtypeStruct((1, ROUTE_LANES), F32)),
        grid=(n // tm,),
        in_specs=[pl.BlockSpec((tm, d), lambda i: (i, 0)),
                  pl.BlockSpec((1, d), lambda i: (0, 0)),
                  mods.spec(layer, 3, tm),
                  mods.spec(layer, 4, tm),
                  pl.BlockSpec((d, ROUTE_LANES), lambda i: (0, 0)),
                  pl.BlockSpec((d, ROUTE_LANES), lambda i: (0, 0)),
                  pl.BlockSpec((1, ROUTE_LANES), lambda i: (0, 0))],
        out_specs=(pl.BlockSpec((tm, d), lambda i: (i, 0)),
                   pl.BlockSpec((tm, ROUTE_LANES), lambda i: (i, 0)),
                   pl.BlockSpec((1, ROUTE_LANES), lambda i: (0, 0))),
        scratch_shapes=[pltpu.VMEM((1, ROUTE_LANES), F32)],
        compiler_params=_cparams(("arbitrary",)),
        name="moe_router",
    )(x, g.reshape(1, d), mods.mods, mods.mods, w_hi, w_lo, b_route)


def _dispatch_kernel(pcnt_ref, pend_ref, dest_ref, h_ref, xs_ref, zbuf, sem, zsem):
    tm = h_ref.shape[0]
    zrows = zbuf.shape[0]

    @pl.when(pl.program_id(0) == 0)
    def _():
        zbuf[...] = jnp.zeros_like(zbuf)

        def clear(e, _):
            @pl.when(pcnt_ref[e] > 0)
            def _():
                start = pl.multiple_of(pend_ref[e] - zrows, zrows)
                cp = pltpu.make_async_copy(zbuf, xs_ref.at[pl.ds(start, zrows)], zsem)
                cp.start()
                cp.wait()
            return 0

        lax.fori_loop(0, MOE_EXPERTS, clear, 0)

    def issue(r, _):
        for k in range(2):
            d = dest_ref[0, 0, 2 * r + k]
            pltpu.make_async_copy(h_ref.at[pl.ds(r, 1)], xs_ref.at[pl.ds(d, 1)], sem).start(priority=k)
        return 0

    lax.fori_loop(0, tm, issue, 0, unroll=ROW_ISSUE_UNROLL)
    for _ in range(2):
        pltpu.make_async_copy(h_ref, xs_ref.at[pl.ds(0, tm)], sem).wait()


def _dispatch(h, dest, pcnt, pends, n_pad, tm=ROW_TM):
    n, d = h.shape
    nblk = n // tm
    grid_spec = pltpu.PrefetchScalarGridSpec(
        num_scalar_prefetch=2,
        grid=(nblk,),
        in_specs=[pl.BlockSpec((1, 1, 2 * tm), lambda i, *_: (i, 0, 0), memory_space=pltpu.SMEM),
                  pl.BlockSpec((tm, d), lambda i, *_: (i, 0))],
        out_specs=pl.BlockSpec(memory_space=pl.ANY),
        scratch_shapes=[pltpu.VMEM((MOE_TM, d), F32), pltpu.SemaphoreType.DMA(()),
                        pltpu.SemaphoreType.DMA(())],
    )
    return pl.pallas_call(
        _dispatch_kernel,
        out_shape=jax.ShapeDtypeStruct((n_pad, d), F32),
        grid_spec=grid_spec,
        compiler_params=_cparams(("arbitrary",)),
        name="moe_dispatch",
    )(pcnt, pends, dest.reshape(nblk, 1, 2 * tm), h)


def _experts_kernel(be_ref, first_ref, nxt_ref, nu_ref, x_ref, wg_hbm, wu_hbm, wd_hbm, o_ref,
                    stg_g, stg_u, stg_d, wg_bf, wu_bf, wd_bf, sem, *, layer, cast_rows):
    i = pl.program_id(0)
    active = i < nu_ref[0]
    stages = ((wg_hbm, stg_g, wg_bf), (wu_hbm, stg_u, wu_bf), (wd_hbm, stg_d, wd_bf))

    def weight_copies(e):
        return [pltpu.make_async_copy(hbm.at[layer, e], stg, sem.at[k])
                for k, (hbm, stg, _) in enumerate(stages)]

    @pl.when(i == 0)
    def _():
        for cp in weight_copies(be_ref[0]):
            cp.start()

    @pl.when(active & (first_ref[i] == 1))
    def _():
        for cp in weight_copies(be_ref[i]):
            cp.wait()
        for _, stg, wbf in stages:
            def cast(r, _, stg=stg, wbf=wbf):
                rows = pl.ds(pl.multiple_of(r * cast_rows, cast_rows), cast_rows)
                wbf[rows, :] = stg[rows, :].astype(BF16)
                return 0

            lax.fori_loop(0, stg.shape[0] // cast_rows, cast, 0)

        @pl.when(nxt_ref[i] >= 0)
        def _():
            for cp in weight_copies(nxt_ref[i]):
                cp.start()

    @pl.when(active)
    def _():
        x = x_ref[...].astype(BF16)
        g = jnp.dot(x, wg_bf[...], preferred_element_type=F32)
        u = jnp.dot(x, wu_bf[...], preferred_element_type=F32)
        a = (_silu(g) * u).astype(BF16)
        o_ref[...] = jnp.dot(a, wd_bf[...], preferred_element_type=F32)

    @pl.when(jnp.logical_not(active))
    def _():
        o_ref[...] = jnp.zeros_like(o_ref)


def _experts(xs, block_expert, block_first, block_next, n_used, w_gate, w_up, w_down, layer, tm=MOE_TM):
    n_pad, d = xs.shape
    f = w_gate.shape[-1]
    nblk = n_pad // tm
    grid_spec = pltpu.PrefetchScalarGridSpec(
        num_scalar_prefetch=4,
        grid=(nblk,),
        in_specs=[pl.BlockSpec((tm, d), lambda i, be, fi, nx, nu: (jnp.minimum(i, nu[0] - 1), 0)),
                  pl.BlockSpec(memory_space=pl.ANY),
                  pl.BlockSpec(memory_space=pl.ANY),
                  pl.BlockSpec(memory_space=pl.ANY)],
        out_specs=pl.BlockSpec((tm, d), lambda i, *_: (i, 0)),
        scratch_shapes=[pltpu.VMEM((d, f), F32), pltpu.VMEM((d, f), F32), pltpu.VMEM((f, d), F32),
                        pltpu.VMEM((d, f), BF16), pltpu.VMEM((d, f), BF16), pltpu.VMEM((f, d), BF16),
                        pltpu.SemaphoreType.DMA((3,))],
    )
    return pl.pallas_call(
        functools.partial(_experts_kernel, layer=layer, cast_rows=256),
        out_shape=jax.ShapeDtypeStruct((n_pad, d), F32),
        grid_spec=grid_spec,
        compiler_params=_cparams(("arbitrary",), vmem=EXPERT_VMEM_LIMIT_BYTES),
        name="moe_experts",
    )(block_expert, block_first, block_next, n_used, xs, w_gate, w_up, w_down)


def _combine_kernel(dest_ref, dnext_ref, y_hbm, x_ref, route_ref, gate_ref, o_ref, buf, sem):
    i = pl.program_id(0)
    nblk = pl.num_programs(0)
    tm = x_ref.shape[0]

    def issue_block(idx_ref, slot):
        def issue(r, _):
            for k in range(2):
                d = idx_ref[0, 0, 2 * r + k]
                pltpu.make_async_copy(y_hbm.at[pl.ds(d, 1)], buf.at[slot, k, pl.ds(r, 1)],
                                      sem.at[slot]).start(priority=k)
            return 0

        lax.fori_loop(0, tm, issue, 0, unroll=ROW_ISSUE_UNROLL)

    @pl.when(i == 0)
    def _():
        issue_block(dest_ref, 0)

    @pl.when(i + 1 < nblk)
    def _():
        issue_block(dnext_ref, (i + 1) % 2)

    slot = i % 2
    for k in range(2):
        pltpu.make_async_copy(y_hbm.at[pl.ds(0, tm)], buf.at[slot, k], sem.at[slot]).wait()
    route = route_ref[...]
    w0 = route[:, _R_W:_R_W + 1]
    w1 = route[:, _R_W + 1:_R_W + 2]
    o_ref[...] = x_ref[...] + gate_ref[...] * (buf[slot, 0] * w0 + buf[slot, 1] * w1)


def _combine(x, y, dest, route, mods, layer, tm=ROW_TM):
    n, d = x.shape
    nblk = n // tm
    dest3 = dest.reshape(nblk, 1, 2 * tm)
    return pl.pallas_call(
        _combine_kernel,
        out_shape=jax.ShapeDtypeStruct((n, d), F32),
        grid=(nblk,),
        in_specs=[pl.BlockSpec((1, 1, 2 * tm), lambda i: (i, 0, 0), memory_space=pltpu.SMEM),
                  pl.BlockSpec((1, 1, 2 * tm), lambda i: (jnp.minimum(i + 1, nblk - 1), 0, 0),
                               memory_space=pltpu.SMEM),
                  pl.BlockSpec(memory_space=pl.ANY),
                  pl.BlockSpec((tm, d), lambda i: (i, 0)),
                  pl.BlockSpec((tm, ROUTE_LANES), lambda i: (i, 0)),
                  mods.spec(layer, 5, tm)],
        out_specs=pl.BlockSpec((tm, d), lambda i: (i, 0)),
        scratch_shapes=[pltpu.VMEM((2, 2, tm, d), F32), pltpu.SemaphoreType.DMA((2,))],
        compiler_params=_cparams(("arbitrary",)),
        name="moe_combine",
    )(dest3, dest3, y, x, route, mods.mods)


def _moe_layer(x, mods, layer, norm_g, w_group, b_group, w_expert, b_expert, w_gate, w_up, w_down):
    n, d = x.shape
    ne = MOE_EXPERTS
    pad = ROUTE_LANES - MOE_GROUPS - ne
    w_route = jnp.concatenate([w_group, w_expert, jnp.zeros((d, pad), F32)], axis=1)
    b_route = jnp.concatenate([b_group, b_expert, jnp.zeros((pad,), F32)]).reshape(1, ROUTE_LANES)
    h, route, counts = _router(x, norm_g, mods, layer, w_route, b_route)
    cnt = counts[0, _R_LOGIT0:_R_LOGIT0 + ne].astype(jnp.int32)
    pcnt = (cnt + MOE_TM - 1) // MOE_TM * MOE_TM
    pends = jnp.cumsum(pcnt)
    pstarts = pends - pcnt
    experts = jnp.arange(ne, dtype=jnp.int32)
    eid = route[:, _R_EID:_R_EID + 2].astype(jnp.int32)
    rank = route[:, _R_RANK:_R_RANK + 2].astype(jnp.int32)
    dest = (jnp.sum(jnp.where(eid[..., None] == experts, pstarts, 0), axis=-1) + rank).reshape(-1)
    n_blocks = (n * 2) // MOE_TM + ne
    n_pad = n_blocks * MOE_TM
    block_row = jnp.arange(n_blocks, dtype=jnp.int32) * MOE_TM
    block_expert = jnp.minimum(jnp.sum((pends[None, :] <= block_row[:, None]).astype(jnp.int32), axis=1), ne - 1)
    block_first = jnp.concatenate([jnp.ones((1,), jnp.int32),
                                   (block_expert[1:] != block_expert[:-1]).astype(jnp.int32)])
    later = (experts[None, :] > experts[:, None]) & (pcnt[None, :] > 0)
    next_expert = jnp.min(jnp.where(later, experts[None, :], ne), axis=1)
    next_expert = jnp.where(next_expert == ne, -1, next_expert)
    block_next = jnp.sum(jnp.where(block_expert[:, None] == experts, next_expert, 0), axis=1)
    n_used = (pends[-1:] // MOE_TM).astype(jnp.int32)
    xs = _dispatch(h, dest, pcnt, pends, n_pad)
    y = _experts(xs, block_expert, block_first, block_next, n_used, w_gate, w_up, w_down, layer)
    return _combine(x, y, dest, route, mods, layer)


class _Streams:
    def __init__(self, batch, seq, dec_batch, dec_seq):
        self.ctx = (0, batch, seq)
        self.lat = (batch * seq, dec_batch, dec_seq)


def _s5_layer(x, mods, layer, streams, norm_g, lam_re, lam_im, log_step, b_re, b_im, c_re, c_im, d_skip,
              w_a, b_a, w_b, b_b, s0_re, s0_im):
    n, d = x.shape
    g, p = lam_re.shape[1:]
    operands = _s5_params(lam_re, lam_im, log_step, b_re, b_im, c_re, c_im)
    h = _adanorm(x, norm_g, mods, layer, 0, 1)
    dvec = d_skip.reshape(1, d)
    row0, nseq, seq = streams.ctx
    zero = jnp.zeros((2, d // LANES, nseq, operands[3].shape[-1]), F32)
    z, sf = _s5_scan(h, operands, dvec, zero, row0, nseq, seq)
    row0, nseq, seq = streams.lat
    z, _ = _s5_scan(h, operands, dvec, _s5_state_to_lanes(s0_re, s0_im), row0, nseq, seq, z_prev=z)
    x = _mm_residual(z, [w_a.astype(BF16), w_b.astype(BF16)], [b_a, b_b], x, mods, layer, 2)
    new_re, new_im = _s5_state_from_lanes(sf, g, p)
    return x, new_re, new_im


def _mlstm_layer(x, mods, layer, streams, norm_g, w_in, b_gates, head_g, w_out, c0, n0, m0, c_slot):
    n, d = x.shape
    hh = ML_HEADS
    dv = d // hh
    dk = dv // 2
    qk = hh * dk
    n_main = 2 * qk + 2 * d
    col_scale = jnp.concatenate([jnp.ones((qk,), F32), jnp.full((qk,), dk ** -0.5, F32),
                                 jnp.ones((2 * d,), F32)])
    w_main = (w_in[:, :n_main] * col_scale).astype(BF16)
    w_gates = jnp.pad(w_in[:, n_main:], ((0, 0), (0, LANES - 4 * hh)))
    proj, gates = _mm_adanorm(x, norm_g, mods, layer, 0, 1, w_main, w_gates)
    k_t = jnp.transpose(proj[:, qk:2 * qk])
    g_c, w_c, e_c, d_r, w_r = _gate_prep(gates, b_gates, ML_T)
    fwd, bwd = slice(_G_FWD, _G_FWD + hh), slice(_G_BWD, _G_BWD + hh)
    cols = jnp.stack([g_c[:, fwd], w_c[:, fwd], e_c[:, fwd], g_c[:, bwd], w_c[:, bwd], e_c[:, bwd]], axis=-1)
    cols = jnp.pad(cols, ((0, 0), (0, 0), (0, 2)))
    rows = jnp.stack([d_r[fwd], w_r[fwd], d_r[bwd], w_r[bwd]], axis=1)

    def gate_views(row0, nseq, seq):
        gc = cols[row0:row0 + nseq * seq].reshape(nseq, seq, hh, 8)
        gr = rows[:, :, row0:row0 + nseq * seq].reshape(hh, 4, nseq, seq)
        return jnp.transpose(gc, (0, 2, 1, 3)), jnp.transpose(gr, (2, 0, 1, 3))

    def n_lanes(nv):
        return jnp.pad(jnp.transpose(nv, (0, 2, 1, 3))[..., None], ((0, 0),) * 4 + ((0, LANES - 1),))

    def m_lanes(mv):
        return jnp.broadcast_to(jnp.transpose(mv, (0, 2, 1))[..., None], mv.shape[:1] + (hh, 2, LANES))

    row0, nseq, seq = streams.ctx
    gcol, grow = gate_views(row0, nseq, seq)
    hn, c_f, n_f, m_f = _mlstm(
        proj, k_t, gcol, grow, head_g,
        jnp.zeros((nseq, 2, hh, dk, dv), F32), jnp.zeros((nseq, hh, 2, dk, LANES), F32),
        jnp.full((nseq, hh, 2, LANES), NEG_BIG, F32), row0, nseq, seq, c_slot=c_slot)
    row0, nseq, seq = streams.lat
    gcol, grow = gate_views(row0, nseq, seq)
    hn, _, _, _ = _mlstm(proj, k_t, gcol, grow, head_g, c0, n_lanes(n0), m_lanes(m0),
                         row0, nseq, seq, hn_prev=hn)
    x = _mm_residual(hn, [w_out.astype(BF16)], None, x, mods, layer, 2)
    return x, c_f, jnp.transpose(n_f[..., 0], (0, 2, 1, 3)), jnp.transpose(m_f[..., 0], (0, 2, 1))


def kernel(x_prompt, x_sample, state_s5_re, state_s5_im, state_mlstm_C, state_mlstm_n, state_mlstm_m, c, c_ctx, w_ada, b_ada, norm1_g, norm2_g, final_norm_g, s5_lambda_re, s5_lambda_im, s5_log_step, s5_b_re, s5_b_im, s5_c_re, s5_c_im, s5_d, s5_w_glu_a, s5_b_glu_a, s5_w_glu_b, s5_b_glu_b, ml_w_in, ml_b_gates, ml_head_norm_g, ml_w_out, moe_w_group, moe_b_group, moe_w_expert, moe_b_expert, moe_w_gate, moe_w_up, moe_w_down):
    batch, seq, d = x_prompt.shape
    dec_batch, dec_seq, _ = x_sample.shape
    depth = w_ada.shape[0]
    n_ctx = batch * seq
    streams = _Streams(batch, seq, dec_batch, dec_seq)

    cond = jnp.concatenate([c_ctx[None, :], c, jnp.zeros((MOD_ROWS - 1 - dec_batch, d), F32)], axis=0)
    mods = _modulation(cond, w_ada, b_ada).reshape(depth * MOD_ROWS * N_MOD, 1, d)
    mods = _Mods(mods, n_ctx, dec_seq)

    x = _embed(x_prompt, x_sample)
    new_s5_re, new_s5_im, new_n, new_m = [], [], [], []
    new_c = None
    n_ml_layers = depth // N_MIXERS
    for l in range(depth):
        j = l // N_MIXERS
        if l % N_MIXERS == 0:
            x, s_re, s_im = _s5_layer(
                x, mods, l, streams, norm1_g[l], s5_lambda_re[j], s5_lambda_im[j], s5_log_step[j],
                s5_b_re[j], s5_b_im[j], s5_c_re[j], s5_c_im[j], s5_d[j],
                s5_w_glu_a[j], s5_b_glu_a[j], s5_w_glu_b[j], s5_b_glu_b[j],
                state_s5_re[:, j], state_s5_im[:, j])
            new_s5_re.append(s_re)
            new_s5_im.append(s_im)
        else:
            x, new_c, n_f, m_f = _mlstm_layer(
                x, mods, l, streams, norm1_g[l], ml_w_in[j], ml_b_gates[j], ml_head_norm_g[j], ml_w_out[j],
                state_mlstm_C[:, j], state_mlstm_n[:, j], state_mlstm_m[:, j], (j, n_ml_layers, new_c))
            new_n.append(n_f)
            new_m.append(m_f)
        x = _moe_layer(x, mods, l, norm2_g[l], moe_w_group[l], moe_b_group[l], moe_w_expert[l],
                       moe_b_expert[l], moe_w_gate, moe_w_up, moe_w_down)
    y_prompt = _final_norm(x, final_norm_g, 0, n_ctx).reshape(batch, seq, d)
    y_sample = _final_norm(x, final_norm_g, n_ctx, dec_batch * dec_seq).reshape(dec_batch, dec_seq, d)
    return (y_prompt, y_sample, jnp.stack(new_s5_re, axis=1), jnp.stack(new_s5_im, axis=1),
            new_c, jnp.stack(new_n, axis=1), jnp.stack(new_m, axis=1))
```

```python
import functools
import math

import jax
import jax.numpy as jnp
from jax import lax
from jax.experimental import pallas as pl
from jax.experimental.pallas import tpu as pltpu

F32 = jnp.float32
BF16 = jnp.bfloat16

S5_GROUP_CH = 16
ML_HEADS = 8
MOE_GROUPS = 4
MOE_EPG = 8
MOE_EXPERTS = MOE_GROUPS * MOE_EPG
GRID_W = 64
POS_BASE = 10000.0
RMS_EPS = 1e-6
NEG_BIG = -1e30
N_MIXERS = 2
N_MOD = 6

LANES = 128
SUBLANES = 8
VMEM_LIMIT_BYTES = 56 * 1024 * 1024
EXPERT_VMEM_LIMIT_BYTES = 60 * 1024 * 1024

MOD_ROWS = 16
S5_LANE_GROUPS = LANES // S5_GROUP_CH
S5_TC = 64
ML_T = 256
ML_HP = 2
ROUTE_LANES = LANES
MOE_TM = 256
ROW_TM = 256
ROW_ISSUE_UNROLL = 8


def _cparams(sem, vmem=VMEM_LIMIT_BYTES):
    return pltpu.CompilerParams(dimension_semantics=sem, vmem_limit_bytes=vmem)


def _silu(x):
    return x * jax.nn.sigmoid(x)


def _gelu_tanh(x):
    c = math.sqrt(2.0 / math.pi)
    return 0.5 * x * (1.0 + jnp.tanh(c * (x + 0.044715 * (x * x * x))))


def _ada_norm_tile(x, g, shift, scale):
    r = lax.rsqrt(jnp.mean(x * x, axis=-1, keepdims=True) + RMS_EPS)
    return (x * r * g) * (1.0 + scale) + shift


def _mod_kernel(c_ref, w_ref, b_ref, o_ref):
    o_ref[...] = _dot_3pass(_silu(c_ref[...]), w_ref[...]) + b_ref[...]


def _modulation(cond, w_ada, b_ada, tn=1024):
    depth, d, n = w_ada.shape
    return pl.pallas_call(
        _mod_kernel,
        out_shape=jax.ShapeDtypeStruct((depth, MOD_ROWS, n), F32),
        grid=(depth, n // tn),
        in_specs=[
            pl.BlockSpec((MOD_ROWS, d), lambda l, j: (0, 0)),
            pl.BlockSpec((None, d, tn), lambda l, j: (l, 0, j)),
            pl.BlockSpec((None, 1, tn), lambda l, j: (l, 0, j)),
        ],
        out_specs=pl.BlockSpec((None, MOD_ROWS, tn), lambda l, j: (l, 0, j)),
        compiler_params=_cparams(("parallel", "parallel")),
        name="modulation",
    )(cond, w_ada, b_ada.reshape(depth, 1, n))


class _Mods:
    def __init__(self, mods, n_ctx, dec_seq):
        self.mods = mods
        self.n_ctx = n_ctx
        self.dec_seq = dec_seq
        self.d = mods.shape[-1]

    def spec(self, layer, k, tm, row_offset=0):
        base = layer * MOD_ROWS * N_MOD + k
        n_ctx, dec_seq = self.n_ctx, self.dec_seq

        def index(i, *_):
            row0 = i * tm + row_offset
            r = jnp.where(row0 < n_ctx, 0, 1 + (row0 - n_ctx) // dec_seq)
            return (base + r * N_MOD, 0, 0)

        return pl.BlockSpec((None, 1, self.d), index)


def _embed_kernel(xc_ref, xl_ref, p_ref, o_ref, *, ctx_blocks):
    i = pl.program_id(0)

    @pl.when(i < ctx_blocks)
    def _():
        o_ref[...] = xc_ref[...]

    @pl.when(i >= ctx_blocks)
    def _():
        o_ref[...] = xl_ref[...] + p_ref[...]


def _grid_pos_embed(n_tokens, d):
    rows = n_tokens // GRID_W
    r, col = jnp.meshgrid(jnp.arange(rows, dtype=F32), jnp.arange(GRID_W, dtype=F32), indexing="ij")
    quarter = d // 4
    omega = 1.0 / (POS_BASE ** (jnp.arange(quarter, dtype=F32) / quarter))

    def axis_embed(pos):
        ang = pos.reshape(-1, 1) * omega[None, :]
        return jnp.concatenate([jnp.sin(ang), jnp.cos(ang)], axis=-1)

    return jnp.concatenate([axis_embed(r), axis_embed(col)], axis=-1)


def _embed(x_prompt, x_sample, tm=512):
    batch, seq, d = x_prompt.shape
    dec_batch, dec_seq, _ = x_sample.shape
    n_ctx, n_lat = batch * seq, dec_batch * dec_seq
    ctx_blocks, seq_blocks = n_ctx // tm, dec_seq // tm
    pos = _grid_pos_embed(dec_seq, d)
    return pl.pallas_call(
        functools.partial(_embed_kernel, ctx_blocks=ctx_blocks),
        out_shape=jax.ShapeDtypeStruct((n_ctx + n_lat, d), F32),
        grid=((n_ctx + n_lat) // tm,),
        in_specs=[pl.BlockSpec((tm, d), lambda i: (jnp.minimum(i, ctx_blocks - 1), 0)),
                  pl.BlockSpec((tm, d), lambda i: (jnp.maximum(i - ctx_blocks, 0), 0)),
                  pl.BlockSpec((tm, d), lambda i: (jnp.maximum(i - ctx_blocks, 0) % seq_blocks, 0))],
        out_specs=pl.BlockSpec((tm, d), lambda i: (i, 0)),
        compiler_params=_cparams(("parallel",)),
        name="embed",
    )(x_prompt.reshape(n_ctx, d), x_sample.reshape(n_lat, d), pos)


def _adanorm_kernel(x_ref, g_ref, sh_ref, sc_ref, o_ref):
    o_ref[...] = _ada_norm_tile(x_ref[...], g_ref[...], sh_ref[...], sc_ref[...]).astype(o_ref.dtype)


def _adanorm(x, g, mods, layer, k_shift, k_scale, tm=512):
    n, d = x.shape
    return pl.pallas_call(
        _adanorm_kernel,
        out_shape=jax.ShapeDtypeStruct((n, d), F32),
        grid=(n // tm,),
        in_specs=[pl.BlockSpec((tm, d), lambda i: (i, 0)),
                  pl.BlockSpec((1, d), lambda i: (0, 0)),
                  mods.spec(layer, k_shift, tm),
                  mods.spec(layer, k_scale, tm)],
        out_specs=pl.BlockSpec((tm, d), lambda i: (i, 0)),
        compiler_params=_cparams(("parallel",)),
        name="adanorm",
    )(x, g.reshape(1, d), mods.mods, mods.mods)


def _final_norm_kernel(x_ref, g_ref, o_ref):
    x = x_ref[...]
    r = lax.rsqrt(jnp.mean(x * x, axis=-1, keepdims=True) + RMS_EPS)
    o_ref[...] = x * r * g_ref[...]


def _final_norm(x, g, row0, nrows, tm=512):
    d = x.shape[1]
    off = row0 // tm
    return pl.pallas_call(
        _final_norm_kernel,
        out_shape=jax.ShapeDtypeStruct((nrows, d), F32),
        grid=(nrows // tm,),
        in_specs=[pl.BlockSpec((tm, d), lambda i: (i + off, 0)),
                  pl.BlockSpec((1, d), lambda i: (0, 0))],
        out_specs=pl.BlockSpec((tm, d), lambda i: (i, 0)),
        compiler_params=_cparams(("parallel",)),
        name="final_norm",
    )(x, g.reshape(1, d))


def _cmul(ar, ai, br, bi):
    return ar * br - ai * bi, ar * bi + ai * br


def _s5_param_kernel(lre_ref, lim_ref, ls_ref, bre_ref, bim_ref, cre_ref, cim_ref,
                     b2_ref, c2_ref, d2_ref, a2_ref):
    h, s = bre_ref.shape
    gl = LANES // h
    p = s // gl
    lr = lre_ref[...]
    li = lim_ref[...]
    dt = jnp.exp(ls_ref[...])
    mag = jnp.exp(lr * dt)
    ar = mag * jnp.cos(li * dt)
    ai = mag * jnp.sin(li * dt)
    den = lr * lr + li * li
    zr = ((ar - 1.0) * lr + ai * li) / den
    zi = (ai * lr - (ar - 1.0) * li) / den
    a2 = _cmul(ar, ai, ar, ai)
    bb = _cmul(zr, zi, bre_ref[...], bim_ref[...])
    abb = _cmul(ar, ai, *bb)
    cc = (cre_ref[...], cim_ref[...])
    ca = _cmul(*cc, ar, ai)
    ca2 = _cmul(*cc, *a2)

    same_b = (lax.broadcasted_iota(jnp.int32, (LANES, s), 0) // h
              == lax.broadcasted_iota(jnp.int32, (LANES, s), 1) // p)
    same_c = (lax.broadcasted_iota(jnp.int32, (s, LANES), 0) // p
              == lax.broadcasted_iota(jnp.int32, (s, LANES), 1) // h)
    spread = (lax.broadcasted_iota(jnp.int32, (h, LANES), 1) % h
              == lax.broadcasted_iota(jnp.int32, (h, LANES), 0)).astype(BF16)

    def bblock(re, im):
        def one(x):
            return jnp.where(same_b, jnp.concatenate([x] * gl, axis=0), 0.0)
        return jnp.concatenate([one(re), one(im)], axis=1)

    def cblock(re, im):
        def one(x):
            t = sum(lax.dot_general(part, spread, (((0,), (0,)), ((), ())), preferred_element_type=F32)
                    for part in _split_bf16(x))
            return jnp.where(same_c, t, 0.0)
        return jnp.concatenate([one(re), one(-im)], axis=0)

    b_blk = bblock(*bb)
    ab_blk = bblock(*abb)
    c_blk = cblock(*cc)
    cb = _dot_3pass(b_blk, c_blk)
    cab = _dot_3pass(ab_blk, c_blk)
    b2_ref[...] = jnp.concatenate([ab_blk, b_blk], axis=0).astype(BF16)
    c2_ref[...] = jnp.concatenate([cblock(*ca), cblock(*ca2)], axis=1).astype(BF16)
    d2_ref[...] = jnp.concatenate([jnp.concatenate([cb, cab], axis=1),
                                   jnp.concatenate([jnp.zeros_like(cb), cb], axis=1)], axis=0).astype(BF16)
    a2_ref[...] = jnp.concatenate(a2, axis=1)


def _s5_params(lam_re, lam_im, log_step, b_re, b_im, c_re, c_im):
    _, g, p = lam_re.shape
    h = b_re.shape[-1]
    gl = S5_LANE_GROUPS
    nlc = g // gl
    s = gl * p

    def lanes(v):
        return v.reshape(2, nlc, 1, s)

    def rows(m):
        return jnp.transpose(m.reshape(2, h, nlc, s), (0, 2, 1, 3))

    step = jnp.broadcast_to(log_step[:, :, None], (2, g, p))
    vec = pl.BlockSpec((None, None, 1, s), lambda d, j: (d, j, 0, 0))
    mat = pl.BlockSpec((None, None, h, s), lambda d, j: (d, j, 0, 0))

    def out(r, c, dt):
        return (jax.ShapeDtypeStruct((2, nlc, r, c), dt),
                pl.BlockSpec((None, None, r, c), lambda d, j: (d, j, 0, 0)))

    outs = [out(2 * LANES, 2 * s, BF16), out(2 * s, 2 * LANES, BF16), out(2 * LANES, 2 * LANES, BF16),
            out(1, 2 * s, F32)]
    return pl.pallas_call(
        _s5_param_kernel,
        out_shape=tuple(o[0] for o in outs),
        grid=(2, nlc),
        in_specs=[vec, vec, vec, mat, mat, mat, mat],
        out_specs=tuple(o[1] for o in outs),
        compiler_params=_cparams(("parallel", "parallel")),
        name="s5_params",
    )(lanes(lam_re), lanes(lam_im), lanes(step),
      rows(jnp.transpose(b_re, (0, 3, 1, 2))), rows(jnp.transpose(b_im, (0, 3, 1, 2))),
      rows(jnp.transpose(c_re, (0, 2, 1, 3))), rows(jnp.transpose(c_im, (0, 2, 1, 3))))


def _s5_state_to_lanes(s_re, s_im):
    b, _, g, p = s_re.shape
    gl = S5_LANE_GROUPS
    nlc = g // gl

    def lay(s):
        return jnp.transpose(s.reshape(b, 2, nlc, gl * p), (1, 2, 0, 3))

    return jnp.concatenate([lay(s_re), lay(s_im)], axis=-1)


def _s5_state_from_lanes(s, g, p):
    _, nlc, b, s2 = s.shape
    half = s2 // 2

    def unlay(t):
        return jnp.transpose(t, (2, 0, 1, 3)).reshape(b, 2, g, p)

    return unlay(s[..., :half]), unlay(s[..., half:])


def _s5_scan_kernel(*refs, seq, tc, aliased):
    if aliased:
        h_ref, b2_ref, c2_ref, d2_ref, a_ref, d_ref, s0_ref, _, z_ref, sf_ref = refs[:10]
    else:
        h_ref, b2_ref, c2_ref, d2_ref, a_ref, d_ref, s0_ref, z_ref, sf_ref = refs[:9]
    scr = (refs[-10:-5], refs[-5:])
    nb = SUBLANES
    half = a_ref.shape[-1] // 2
    nc = seq // tc
    npair = tc // 2
    a_re = [jnp.broadcast_to(a_ref[dr][:, :half], (nb, half)) for dr in range(2)]
    a_im = [jnp.broadcast_to(a_ref[dr][:, half:], (nb, half)) for dr in range(2)]

    def chunk(c, carry):
        t0s = (c * tc, (nc - 1 - c) * tc)

        def pair_rows(dr, p):
            if dr == 0:
                first = t0s[0] + 2 * p
                return pl.ds(first, nb, stride=seq), pl.ds(first + 1, nb, stride=seq)
            first = t0s[1] + tc - 1 - 2 * p
            return pl.ds(first, nb, stride=seq), pl.ds(first - 1, nb, stride=seq)

        carry = list(carry)
        for dr in range(2):
            u_scr, bu_scr, _, _, _ = scr[dr]
            for p in range(npair):
                r1, r2 = pair_rows(dr, p)
                u_scr[p * nb:(p + 1) * nb, :LANES] = h_ref[r1, :]
                u_scr[p * nb:(p + 1) * nb, LANES:] = h_ref[r2, :]
            bu_scr[...] = jnp.dot(u_scr[...].astype(BF16), b2_ref[dr], preferred_element_type=F32)
        for dr in range(2):
            _, bu_scr, x_scr, _, _ = scr[dr]
            xr, xi = carry[2 * dr], carry[2 * dr + 1]
            for p in range(npair):
                x_scr[p * nb:(p + 1) * nb, :half] = xr
                x_scr[p * nb:(p + 1) * nb, half:] = xi
                bu = bu_scr[p * nb:(p + 1) * nb, :]
                xr, xi = (a_re[dr] * xr - a_im[dr] * xi + bu[:, :half],
                          a_re[dr] * xi + a_im[dr] * xr + bu[:, half:])
            carry[2 * dr], carry[2 * dr + 1] = xr, xi
        for dr in range(2):
            u_scr, _, x_scr, yo_scr, y_scr = scr[dr]
            yo_scr[...] = jnp.dot(x_scr[...].astype(BF16), c2_ref[dr], preferred_element_type=F32) \
                + jnp.dot(u_scr[...].astype(BF16), d2_ref[dr], preferred_element_type=F32)
            for p in range(npair):
                r1, r2 = pair_rows(dr, p)
                y_scr[r1, :] = yo_scr[p * nb:(p + 1) * nb, :LANES]
                y_scr[r2, :] = yo_scr[p * nb:(p + 1) * nb, LANES:]
        return tuple(carry)

    s0f = s0_ref[0]
    s0b = s0_ref[1]
    init = (s0f[:, :half], s0f[:, half:], s0b[:, :half], s0b[:, half:])
    xr_f, xi_f, xr_b, xi_b = lax.fori_loop(0, nc, chunk, init)
    sf_ref[0, :, :half] = xr_f
    sf_ref[0, :, half:] = xi_f
    sf_ref[1, :, :half] = xr_b
    sf_ref[1, :, half:] = xi_b
    y = h_ref[...] * d_ref[...] + scr[0][4][...] + scr[1][4][...]
    z_ref[...] = _gelu_tanh(y).astype(z_ref.dtype)


def _s5_scan(h, operands, dvec, s0, row0, nseq, seq, z_prev=None):
    b2, c2, d2, avec = operands
    n, d = h.shape
    nlc = d // LANES
    s2 = avec.shape[-1]
    nb = SUBLANES
    rows = nb * seq
    off = row0 // rows
    aliased = z_prev is not None
    in_specs = [
        pl.BlockSpec((rows, LANES), lambda i, j: (i + off, j)),
        pl.BlockSpec((2, None, 2 * LANES, s2), lambda i, j: (0, j, 0, 0)),
        pl.BlockSpec((2, None, s2, 2 * LANES), lambda i, j: (0, j, 0, 0)),
        pl.BlockSpec((2, None, 2 * LANES, 2 * LANES), lambda i, j: (0, j, 0, 0)),
        pl.BlockSpec((2, None, 1, s2), lambda i, j: (0, j, 0, 0)),
        pl.BlockSpec((1, LANES), lambda i, j: (0, j)),
        pl.BlockSpec((2, None, nb, s2), lambda i, j: (0, j, i, 0)),
    ]
    args = [h, b2, c2, d2, avec, dvec, s0]
    io_alias = {}
    if aliased:
        in_specs.append(pl.BlockSpec(memory_space=pl.ANY))
        args.append(z_prev)
        io_alias = {len(args) - 1: 0}
    npair_rows = S5_TC // 2 * nb
    return pl.pallas_call(
        functools.partial(_s5_scan_kernel, seq=seq, tc=S5_TC, aliased=aliased),
        out_shape=(jax.ShapeDtypeStruct((n, d), BF16), jax.ShapeDtypeStruct((2, nlc, nseq, s2), F32)),
        grid=(nseq // nb, nlc),
        in_specs=in_specs,
        out_specs=(pl.BlockSpec((rows, LANES), lambda i, j: (i + off, j)),
                   pl.BlockSpec((2, None, nb, s2), lambda i, j: (0, j, i, 0))),
        scratch_shapes=2 * [pltpu.VMEM((npair_rows, 2 * LANES), F32),
                            pltpu.VMEM((npair_rows, s2), F32),
                            pltpu.VMEM((npair_rows, s2), F32),
                            pltpu.VMEM((npair_rows, 2 * LANES), F32),
                            pltpu.VMEM((rows, LANES), F32)],
        input_output_aliases=io_alias,
        compiler_params=_cparams(("parallel", "parallel")),
        name="s5_scan",
    )(*args)


def _mm_res_kernel(*refs, n_w, has_bias):
    z_ref = refs[0]
    w_refs = refs[1:1 + n_w]
    pos = 1 + n_w
    b_refs = refs[pos:pos + n_w] if has_bias else ()
    pos += n_w if has_bias else 0
    x_ref, gate_ref, o_ref = refs[pos:pos + 3]
    z = z_ref[...]
    acc = [jnp.dot(z, w[...], preferred_element_type=F32) for w in w_refs]
    if has_bias:
        acc = [a + b[...] for a, b in zip(acc, b_refs)]
    y = acc[0] if n_w == 1 else acc[0] * jax.nn.sigmoid(acc[1])
    o_ref[...] = x_ref[...] + gate_ref[...] * y


def _mm_residual(z, ws, bs, x, mods, layer, k_gate, tm=1024, tn=512):
    n, k = z.shape
    n_out = ws[0].shape[1]
    n_w = len(ws)
    has_bias = bs is not None
    in_specs = [pl.BlockSpec((tm, k), lambda i, j: (i, 0))]
    in_specs += [pl.BlockSpec((k, tn), lambda i, j: (0, j)) for _ in ws]
    args = [z, *ws]
    if has_bias:
        in_specs += [pl.BlockSpec((1, tn), lambda i, j: (0, j)) for _ in bs]
        args += [b.reshape(1, n_out) for b in bs]
    gate_spec = mods.spec(layer, k_gate, tm)
    gate_spec = pl.BlockSpec((None, 1, tn), lambda i, j, f=gate_spec.index_map: (f(i)[0], 0, j))
    in_specs += [pl.BlockSpec((tm, tn), lambda i, j: (i, j)), gate_spec]
    args += [x, mods.mods]
    return pl.pallas_call(
        functools.partial(_mm_res_kernel, n_w=n_w, has_bias=has_bias),
        out_shape=jax.ShapeDtypeStruct((n, n_out), F32),
        grid=(n // tm, n_out // tn),
        in_specs=in_specs,
        out_specs=pl.BlockSpec((tm, tn), lambda i, j: (i, j)),
        compiler_params=_cparams(("parallel", "parallel")),
        name=f"proj_residual_{n_w}w",
    )(*args)


def _split_bf16(a):
    hi = a.astype(BF16)
    return hi, (a - hi.astype(F32)).astype(BF16)


def _dot_3pass(a, b):
    a_hi, a_lo = _split_bf16(a)
    b_hi, b_lo = _split_bf16(b)
    return (jnp.dot(a_hi, b_hi, preferred_element_type=F32) + jnp.dot(a_hi, b_lo, preferred_element_type=F32)
            + jnp.dot(a_lo, b_hi, preferred_element_type=F32))


def _mm_norm_kernel(x_ref, g_ref, sh_ref, sc_ref, w_ref, ws_ref, o_ref, os_ref, h_scr):
    @pl.when(pl.program_id(1) == 0)
    def _():
        h = _ada_norm_tile(x_ref[...], g_ref[...], sh_ref[...], sc_ref[...])
        h_scr[...] = h.astype(h_scr.dtype)
        os_ref[...] = _dot_3pass(h, ws_ref[...])

    o_ref[...] = jnp.dot(h_scr[...], w_ref[...], preferred_element_type=F32).astype(o_ref.dtype)


def _mm_adanorm(x, g, mods, layer, k_shift, k_scale, w, w_side, tm=1024, tn=1024):
    n, d = x.shape
    n_out = w.shape[1]
    n_side = w_side.shape[1]
    return pl.pallas_call(
        _mm_norm_kernel,
        out_shape=(jax.ShapeDtypeStruct((n, n_out), BF16), jax.ShapeDtypeStruct((n, n_side), F32)),
        grid=(n // tm, n_out // tn),
        in_specs=[pl.BlockSpec((tm, d), lambda i, j: (i, 0)),
                  pl.BlockSpec((1, d), lambda i, j: (0, 0)),
                  mods.spec(layer, k_shift, tm),
                  mods.spec(layer, k_scale, tm),
                  pl.BlockSpec((d, tn), lambda i, j: (0, j)),
                  pl.BlockSpec((d, n_side), lambda i, j: (0, 0))],
        out_specs=(pl.BlockSpec((tm, tn), lambda i, j: (i, j)),
                   pl.BlockSpec((tm, n_side), lambda i, j: (i, 0))),
        scratch_shapes=[pltpu.VMEM((tm, d), BF16)],
        compiler_params=_cparams(("parallel", "arbitrary")),
        name="adanorm_proj",
    )(x, g.reshape(1, d), mods.mods, mods.mods, w, w_side)


_G_FWD = ML_HEADS
_G_BWD = 3 * ML_HEADS


def _gate_kernel(xc_ref, xr_ref, bc_ref, br_ref, g_ref, w_ref, e_ref, dr_ref, wr_ref):
    t = xc_ref.shape[0]
    r_i = lax.broadcasted_iota(jnp.int32, (t, t), 0)
    c_i = lax.broadcasted_iota(jnp.int32, (t, t), 1)
    lower = (c_i <= r_i).astype(F32)
    upper = (c_i >= r_i).astype(F32)
    hi = lax.Precision.HIGHEST
    xc = xc_ref[...] + bc_ref[...]
    fc = jax.nn.log_sigmoid(xc)
    lane = lax.broadcasted_iota(jnp.int32, xc.shape, 1)
    g_c = jnp.where(lane < 2 * ML_HEADS,
                    jnp.dot(lower, fc, preferred_element_type=F32, precision=hi),
                    jnp.dot(upper, fc, preferred_element_type=F32, precision=hi))
    e_c = jnp.broadcast_to(jnp.sum(fc, axis=0, keepdims=True), xc.shape)
    g_ref[...] = g_c
    e_ref[...] = e_c
    w_ref[...] = e_c - g_c + pltpu.roll(xc, ML_HEADS, 1)
    xr = xr_ref[...] + br_ref[...]
    fr = jax.nn.log_sigmoid(xr)
    row = lax.broadcasted_iota(jnp.int32, xr.shape, 0)
    g_r = jnp.where(row < 2 * ML_HEADS,
                    jnp.dot(fr, upper, preferred_element_type=F32, precision=hi),
                    jnp.dot(fr, lower, preferred_element_type=F32, precision=hi))
    i_r = pltpu.roll(xr, ML_HEADS, 0)
    dr_ref[...] = i_r - g_r
    wr_ref[...] = jnp.sum(fr, axis=1, keepdims=True) - g_r + i_r


def _gate_prep(gates, b_gates, tch):
    n = gates.shape[0]
    ng = 4 * ML_HEADS
    bias_c = jnp.pad(b_gates, (0, LANES - ng)).reshape(1, LANES)
    bias_r = b_gates.reshape(ng, 1)
    gates_r = jnp.transpose(gates[:, :ng])
    col = pl.BlockSpec((tch, LANES), lambda i: (i, 0))
    rowb = pl.BlockSpec((ng, tch), lambda i: (0, i))
    return pl.pallas_call(
        _gate_kernel,
        out_shape=(jax.ShapeDtypeStruct((n, LANES), F32),) * 3 + (jax.ShapeDtypeStruct((ng, n), F32),) * 2,
        grid=(n // tch,),
        in_specs=[col, rowb, pl.BlockSpec((1, LANES), lambda i: (0, 0)),
                  pl.BlockSpec((ng, 1), lambda i: (0, 0))],
        out_specs=(col, col, col, rowb, rowb),
        compiler_params=_cparams(("parallel",)),
        name="mlstm_gates",
    )(gates, gates_r, bias_c, bias_r)


def _mlstm_kernel(q_ref, kt_ref, v_ref, o_ref, gcol_ref, grow_ref, hg_ref, *rest, seq, tch, hp, fresh):
    if fresh:
        c0_ref = n0_ref = m0_ref = None
    else:
        c0_ref, n0_ref, m0_ref = rest[:3]
        rest = rest[3:]
    hn_ref, cf_ref, nf_ref, mf_ref = rest[:4]
    scratch = rest[4:]
    nc = seq // tch
    dv = v_ref.shape[-1] // hp
    dk = q_ref.shape[-1] // hp
    tt = lax.broadcasted_iota(jnp.int32, (tch, tch), 0)
    ss = lax.broadcasted_iota(jnp.int32, (tch, tch), 1)
    ones_col = (lax.broadcasted_iota(jnp.int32, (tch, LANES), 1) == 0).astype(BF16)
    scr = {(hd, dr): scratch[3 * (2 * hd + dr):3 * (2 * hd + dr) + 3] for hd in range(hp) for dr in range(2)}
    for (hd, dr), (_, c_scr, vx) in scr.items():
        if fresh:
            c_scr[...] = jnp.zeros_like(c_scr)
        else:
            c_scr[:, :dv] = c0_ref[dr, hd]
            c_scr[:, dv:] = n0_ref[hd, dr]
        vx[:, dv:] = ones_col

    def chunk_dir(hd, dr, r0, m):
        hacc, c_scr, vx = scr[hd, dr]
        rows = pl.ds(pl.multiple_of(r0, tch), tch)
        q = q_ref[rows, hd * dk:(hd + 1) * dk]
        kt = kt_ref[hd * dk:(hd + 1) * dk, rows]
        vx[:, :dv] = v_ref[rows, hd * dv:(hd + 1) * dv]
        gc = gcol_ref[hd, rows, :]
        gr = grow_ref[hd, :, rows]
        g_col = gc[:, 3 * dr:3 * dr + 1]
        w_col = gc[:, 3 * dr + 1:3 * dr + 2]
        e_col = gc[:, 3 * dr + 2:3 * dr + 3]
        d_row = gr[2 * dr:2 * dr + 1, :]
        w_row = gr[2 * dr + 1:2 * dr + 2, :]
        mask = (ss <= tt) if dr == 0 else (ss >= tt)
        a_col = g_col + m
        dmat = jnp.where(mask, g_col + d_row, -jnp.inf)
        mt = jnp.maximum(a_col, jnp.max(dmat, axis=1, keepdims=True))
        qk = jnp.dot(q, kt, preferred_element_type=F32)
        s = (qk * jnp.exp(dmat - mt)).astype(BF16)
        inter = jnp.exp(a_col - mt)
        vext = vx[...]
        tot = jnp.dot(s, vext, preferred_element_type=F32) \
            + inter * jnp.dot(q, c_scr[...].astype(BF16), preferred_element_type=F32)
        den = tot[:, dv:dv + 1]
        hacc[rows, :] = tot[:, :dv] / jnp.maximum(jnp.abs(den), jnp.exp(-mt))
        g_end = jnp.max(e_col, axis=0, keepdims=True)
        m_new = jnp.maximum(g_end + m, jnp.max(w_col, axis=0, keepdims=True))
        decay = jnp.exp(g_end + m - m_new)
        kw = (kt.astype(F32) * jnp.exp(w_row - m_new)).astype(BF16)
        c_scr[...] = decay * c_scr[...] + jnp.dot(kw, vext, preferred_element_type=F32)
        return m_new

    keys = list(scr)

    def body(c, carry):
        return tuple(chunk_dir(hd, dr, (c if dr == 0 else nc - 1 - c) * tch, m)
                     for (hd, dr), m in zip(keys, carry))

    m_init = tuple(jnp.full((1, 1), NEG_BIG, F32) if fresh else m0_ref[hd, dr:dr + 1, 0:1] for hd, dr in keys)
    m_fin = lax.fori_loop(0, nc, body, m_init)
    for (hd, dr), m in zip(keys, m_fin):
        c_scr = scr[hd, dr][1]
        cf_ref[dr, hd] = c_scr[:, :dv]
        nf_ref[hd, dr] = c_scr[:, dv:]
        mf_ref[hd, dr:dr + 1, :] = jnp.broadcast_to(m, (1, LANES))
    for hd in range(hp):
        cols = slice(hd * dv, (hd + 1) * dv)
        hs = scr[hd, 0][0][...] + scr[hd, 1][0][...]
        hn = hs * lax.rsqrt(jnp.mean(hs * hs, axis=-1, keepdims=True) + RMS_EPS)
        hn = hn * hg_ref[:, cols] * jax.nn.sigmoid(o_ref[:, cols].astype(F32))
        hn_ref[:, cols] = hn.astype(hn_ref.dtype)


def _mlstm(proj, k_t, gcol, grow, head_g, c0, n0, m0, row0, nseq, seq, hn_prev=None, c_slot=None, hp=ML_HP):
    n = proj.shape[0]
    h = ML_HEADS
    d = proj.shape[1] // 3
    dv = d // h
    dk = dv // 2
    qk = h * dk
    fresh = c0 is None
    off = row0 // seq
    tch = min(ML_T, seq)
    aliased = hn_prev is not None
    c_spec = pl.BlockSpec((None, 2, hp, dk, dv), lambda b, j: (b, 0, j, 0, 0))
    n_spec = pl.BlockSpec((None, hp, 2, dk, LANES), lambda b, j: (b, j, 0, 0, 0))
    m_spec = pl.BlockSpec((None, hp, 2, LANES), lambda b, j: (b, j, 0, 0))
    wk, wv = hp * dk, hp * dv
    in_specs = [
        pl.BlockSpec((seq, wk), lambda b, j: (b + off, j)),
        pl.BlockSpec((wk, seq), lambda b, j: (j, b + off)),
        pl.BlockSpec((seq, wv), lambda b, j: (b + off, 2 * qk // wv + j)),
        pl.BlockSpec((seq, wv), lambda b, j: (b + off, (2 * qk + d) // wv + j)),
        pl.BlockSpec((None, hp, seq, 8), lambda b, j: (b, j, 0, 0)),
        pl.BlockSpec((None, hp, 4, seq), lambda b, j: (b, j, 0, 0)),
        pl.BlockSpec((1, wv), lambda b, j: (0, j)),
    ]
    args = [proj, k_t, proj, proj, gcol, grow, head_g.reshape(1, d)]
    if not fresh:
        in_specs += [c_spec, n_spec, m_spec]
        args += [c0, n0, m0]
    n_in = len(args)
    io_alias = {}
    if aliased:
        in_specs.append(pl.BlockSpec(memory_space=pl.ANY))
        args.append(hn_prev)
        io_alias[len(args) - 1] = 0
    c_type = jax.ShapeDtypeStruct((nseq, 2, h, dk, dv), F32)
    c_out_spec = c_spec
    if c_slot is not None:
        slot, n_slots, c_prev = c_slot
        c_type = jax.ShapeDtypeStruct((nseq, n_slots, 2, h, dk, dv), F32)
        c_out_spec = pl.BlockSpec((None, None, 2, hp, dk, dv), lambda b, j: (b, slot, 0, j, 0, 0))
        if c_prev is not None:
            in_specs.append(pl.BlockSpec(memory_space=pl.ANY))
            args.append(c_prev)
            io_alias[len(args) - 1] = 1

    def kern(*refs):
        _mlstm_kernel(*refs[:n_in], *refs[len(args):], seq=seq, tch=tch, hp=hp, fresh=fresh)

    return pl.pallas_call(
        kern,
        out_shape=(jax.ShapeDtypeStruct((n, d), BF16), c_type,
                   jax.ShapeDtypeStruct((nseq, h, 2, dk, LANES), F32),
                   jax.ShapeDtypeStruct((nseq, h, 2, LANES), F32)),
        grid=(nseq, h // hp),
        in_specs=in_specs,
        out_specs=(pl.BlockSpec((seq, wv), lambda b, j: (b + off, j)), c_out_spec, n_spec, m_spec),
        scratch_shapes=2 * hp * [pltpu.VMEM((seq, dv), F32), pltpu.VMEM((dk, dv + LANES), F32),
                                 pltpu.VMEM((tch, dv + LANES), BF16)],
        input_output_aliases=io_alias,
        compiler_params=_cparams(("parallel", "parallel")),
        name="mlstm",
    )(*args)


_R_EID, _R_W, _R_RANK = 0, 2, 4
_R_LOGIT0 = MOE_GROUPS


def _router_kernel(x_ref, g_ref, sh_ref, sc_ref, whi_ref, wlo_ref, b_ref, h_ref, route_ref, cnt_ref, carry):
    i = pl.program_id(0)

    @pl.when(i == 0)
    def _():
        carry[...] = jnp.zeros_like(carry)

    h = _ada_norm_tile(x_ref[...], g_ref[...], sh_ref[...], sc_ref[...])
    h_ref[...] = h
    tm = h.shape[0]
    h_hi = h.astype(BF16)
    h_lo = (h - h_hi.astype(F32)).astype(BF16)
    logits = (jnp.dot(h_hi, whi_ref[...], preferred_element_type=F32)
              + jnp.dot(h_hi, wlo_ref[...], preferred_element_type=F32)
              + jnp.dot(h_lo, whi_ref[...], preferred_element_type=F32)) + b_ref[...]
    lane = lax.broadcasted_iota(jnp.int32, logits.shape, 1)
    big = jnp.int32(ROUTE_LANES)

    def first_lane(cond):
        return jnp.min(jnp.where(cond, lane, big), axis=1, keepdims=True)

    glog = jnp.where(lane < MOE_GROUPS, logits, -jnp.inf)
    ge = jnp.exp(glog - jnp.max(glog, axis=1, keepdims=True))
    pgrp = ge / jnp.sum(ge, axis=1, keepdims=True)
    pg = jnp.max(pgrp, axis=1, keepdims=True)
    grp = first_lane(pgrp == pg)
    e_lane = lane - _R_LOGIT0
    in_grp = (e_lane >= 0) & (e_lane < MOE_EXPERTS) & ((e_lane // MOE_EPG) == grp)
    elog = jnp.where(in_grp, logits, -jnp.inf)
    ee = jnp.exp(elog - jnp.max(elog, axis=1, keepdims=True))
    pe = jnp.where(in_grp, ee / jnp.sum(ee, axis=1, keepdims=True), -1.0)
    p0 = jnp.max(pe, axis=1, keepdims=True)
    l0 = first_lane(pe == p0)
    pe1 = jnp.where(lane == l0, -1.0, pe)
    p1 = jnp.max(pe1, axis=1, keepdims=True)
    l1 = first_lane(pe1 == p1)
    psum = p0 + p1
    w0 = pg * p0 / psum
    w1 = pg * p1 / psum
    onehot = ((lane == l0) | (lane == l1)).astype(BF16)
    r_i = lax.broadcasted_iota(jnp.int32, (tm, tm), 0)
    c_i = lax.broadcasted_iota(jnp.int32, (tm, tm), 1)
    tri = (c_i < r_i).astype(BF16)
    before = jnp.dot(tri, onehot, preferred_element_type=F32) + carry[...]
    rank0 = jnp.sum(jnp.where(lane == l0, before, 0.0), axis=1, keepdims=True)
    rank1 = jnp.sum(jnp.where(lane == l1, before, 0.0), axis=1, keepdims=True)
    carry[...] = carry[...] + jnp.sum(onehot.astype(F32), axis=0, keepdims=True)
    cnt_ref[...] = carry[...]
    cols = [(l0 - _R_LOGIT0).astype(F32), (l1 - _R_LOGIT0).astype(F32), w0, w1, rank0, rank1]
    route = jnp.zeros(logits.shape, F32)
    for c, val in enumerate(cols):
        route = jnp.where(lane == c, val, route)
    route_ref[...] = route


def _router(x, g, mods, layer, w_route, b_route, tm=512):
    n, d = x.shape
    w_hi = w_route.astype(BF16)
    w_lo = (w_route - w_hi.astype(F32)).astype(BF16)
    return pl.pallas_call(
        _router_kernel,
        out_shape=(jax.ShapeDtypeStruct((n, d), F32),
                   jax.ShapeDtypeStruct((n, ROUTE_LANES), F32),
                   jax.ShapeDtypeStruct((1, ROUTE_LANES), F32)),
        grid=(n // tm,),
        in_specs=[pl.BlockSpec((tm, d), lambda i: (i, 0)),
                  pl.BlockSpec((1, d), lambda i: (0, 0)),
                  mods.spec(layer, 3, tm),
                  mods.spec(layer, 4, tm),
                  pl.BlockSpec((d, ROUTE_LANES), lambda i: (0, 0)),
                  pl.BlockSpec((d, ROUTE_LANES), lambda i: (0, 0)),
                  pl.BlockSpec((1, ROUTE_LANES), lambda i: (0, 0))],
        out_specs=(pl.BlockSpec((tm, d), lambda i: (i, 0)),
                   pl.BlockSpec((tm, ROUTE_LANES), lambda i: (i, 0)),
                   pl.BlockSpec((1, ROUTE_LANES), lambda i: (0, 0))),
        scratch_shapes=[pltpu.VMEM((1, ROUTE_LANES), F32)],
        compiler_params=_cparams(("arbitrary",)),
        name="moe_router",
    )(x, g.reshape(1, d), mods.mods, mods.mods, w_hi, w_lo, b_route)


def _dispatch_kernel(pcnt_ref, pend_ref, dest_ref, h_ref, xs_ref, zbuf, sem, zsem):
    tm = h_ref.shape[0]
    zrows = zbuf.shape[0]

    @pl.when(pl.program_id(0) == 0)
    def _():
        zbuf[...] = jnp.zeros_like(zbuf)

        def clear(e, _):
            @pl.when(pcnt_ref[e] > 0)
            def _():
                start = pl.multiple_of(pend_ref[e] - zrows, zrows)
                cp = pltpu.make_async_copy(zbuf, xs_ref.at[pl.ds(start, zrows)], zsem)
                cp.start()
                cp.wait()
            return 0

        lax.fori_loop(0, MOE_EXPERTS, clear, 0)

    def issue(r, _):
        for k in range(2):
            d = dest_ref[0, 0, 2 * r + k]
            pltpu.make_async_copy(h_ref.at[pl.ds(r, 1)], xs_ref.at[pl.ds(d, 1)], sem).start(priority=k)
        return 0

    lax.fori_loop(0, tm, issue, 0, unroll=ROW_ISSUE_UNROLL)
    for _ in range(2):
        pltpu.make_async_copy(h_ref, xs_ref.at[pl.ds(0, tm)], sem).wait()


def _dispatch(h, dest, pcnt, pends, n_pad, tm=ROW_TM):
    n, d = h.shape
    nblk = n // tm
    grid_spec = pltpu.PrefetchScalarGridSpec(
        num_scalar_prefetch=2,
        grid=(nblk,),
        in_specs=[pl.BlockSpec((1, 1, 2 * tm), lambda i, *_: (i, 0, 0), memory_space=pltpu.SMEM),
                  pl.BlockSpec((tm, d), lambda i, *_: (i, 0))],
        out_specs=pl.BlockSpec(memory_space=pl.ANY),
        scratch_shapes=[pltpu.VMEM((MOE_TM, d), F32), pltpu.SemaphoreType.DMA(()),
                        pltpu.SemaphoreType.DMA(())],
    )
    return pl.pallas_call(
        _dispatch_kernel,
        out_shape=jax.ShapeDtypeStruct((n_pad, d), F32),
        grid_spec=grid_spec,
        compiler_params=_cparams(("arbitrary",)),
        name="moe_dispatch",
    )(pcnt, pends, dest.reshape(nblk, 1, 2 * tm), h)


def _experts_kernel(be_ref, first_ref, nxt_ref, nu_ref, x_ref, wg_hbm, wu_hbm, wd_hbm, o_ref,
                    stg_g, stg_u, stg_d, wg_bf, wu_bf, wd_bf, sem, *, layer, cast_rows):
    i = pl.program_id(0)
    active = i < nu_ref[0]
    stages = ((wg_hbm, stg_g, wg_bf), (wu_hbm, stg_u, wu_bf), (wd_hbm, stg_d, wd_bf))

    def weight_copies(e):
        return [pltpu.make_async_copy(hbm.at[layer, e], stg, sem.at[k])
                for k, (hbm, stg, _) in enumerate(stages)]

    @pl.when(i == 0)
    def _():
        for cp in weight_copies(be_ref[0]):
            cp.start()

    @pl.when(active & (first_ref[i] == 1))
    def _():
        for cp in weight_copies(be_ref[i]):
            cp.wait()
        for _, stg, wbf in stages:
            def cast(r, _, stg=stg, wbf=wbf):
                rows = pl.ds(pl.multiple_of(r * cast_rows, cast_rows), cast_rows)
                wbf[rows, :] = stg[rows, :].astype(BF16)
                return 0

            lax.fori_loop(0, stg.shape[0] // cast_rows, cast, 0)

        @pl.when(nxt_ref[i] >= 0)
        def _():
            for cp in weight_copies(nxt_ref[i]):
                cp.start()

    @pl.when(active)
    def _():
        x = x_ref[...].astype(BF16)
        g = jnp.dot(x, wg_bf[...], preferred_element_type=F32)
        u = jnp.dot(x, wu_bf[...], preferred_element_type=F32)
        a = (_silu(g) * u).astype(BF16)
        o_ref[...] = jnp.dot(a, wd_bf[...], preferred_element_type=F32)

    @pl.when(jnp.logical_not(active))
    def _():
        o_ref[...] = jnp.zeros_like(o_ref)


def _experts(xs, block_expert, block_first, block_next, n_used, w_gate, w_up, w_down, layer, tm=MOE_TM):
    n_pad, d = xs.shape
    f = w_gate.shape[-1]
    nblk = n_pad // tm
    grid_spec = pltpu.PrefetchScalarGridSpec(
        num_scalar_prefetch=4,
        grid=(nblk,),
        in_specs=[pl.BlockSpec((tm, d), lambda i, be, fi, nx, nu: (jnp.minimum(i, nu[0] - 1), 0)),
                  pl.BlockSpec(memory_space=pl.ANY),
                  pl.BlockSpec(memory_space=pl.ANY),
                  pl.BlockSpec(memory_space=pl.ANY)],
        out_specs=pl.BlockSpec((tm, d), lambda i, *_: (i, 0)),
        scratch_shapes=[pltpu.VMEM((d, f), F32), pltpu.VMEM((d, f), F32), pltpu.VMEM((f, d), F32),
                        pltpu.VMEM((d, f), BF16), pltpu.VMEM((d, f), BF16), pltpu.VMEM((f, d), BF16),
                        pltpu.SemaphoreType.DMA((3,))],
    )
    return pl.pallas_call(
        functools.partial(_experts_kernel, layer=layer, cast_rows=256),
        out_shape=jax.ShapeDtypeStruct((n_pad, d), F32),
        grid_spec=grid_spec,
        compiler_params=_cparams(("arbitrary",), vmem=EXPERT_VMEM_LIMIT_BYTES),
        name="moe_experts",
    )(block_expert, block_first, block_next, n_used, xs, w_gate, w_up, w_down)


def _combine_kernel(dest_ref, dnext_ref, y_hbm, x_ref, route_ref, gate_ref, o_ref, buf, sem):
    i = pl.program_id(0)
    nblk = pl.num_programs(0)
    tm = x_ref.shape[0]

    def issue_block(idx_ref, slot):
        def issue(r, _):
            for k in range(2):
                d = idx_ref[0, 0, 2 * r + k]
                pltpu.make_async_copy(y_hbm.at[pl.ds(d, 1)], buf.at[slot, k, pl.ds(r, 1)],
                                      sem.at[slot]).start(priority=k)
            return 0

        lax.fori_loop(0, tm, issue, 0, unroll=ROW_ISSUE_UNROLL)

    @pl.when(i == 0)
    def _():
        issue_block(dest_ref, 0)

    @pl.when(i + 1 < nblk)
    def _():
        issue_block(dnext_ref, (i + 1) % 2)

    slot = i % 2
    for k in range(2):
        pltpu.make_async_copy(y_hbm.at[pl.ds(0, tm)], buf.at[slot, k], sem.at[slot]).wait()
    route = route_ref[...]
    w0 = route[:, _R_W:_R_W + 1]
    w1 = route[:, _R_W + 1:_R_W + 2]
    o_ref[...] = x_ref[...] + gate_ref[...] * (buf[slot, 0] * w0 + buf[slot, 1] * w1)


def _combine(x, y, dest, route, mods, layer, tm=ROW_TM):
    n, d = x.shape
    nblk = n // tm
    dest3 = dest.reshape(nblk, 1, 2 * tm)
    return pl.pallas_call(
        _combine_kernel,
        out_shape=jax.ShapeDtypeStruct((n, d), F32),
        grid=(nblk,),
        in_specs=[pl.BlockSpec((1, 1, 2 * tm), lambda i: (i, 0, 0), memory_space=pltpu.SMEM),
                  pl.BlockSpec((1, 1, 2 * tm), lambda i: (jnp.minimum(i + 1, nblk - 1), 0, 0),
                               memory_space=pltpu.SMEM),
                  pl.BlockSpec(memory_space=pl.ANY),
                  pl.BlockSpec((tm, d), lambda i: (i, 0)),
                  pl.BlockSpec((tm, ROUTE_LANES), lambda i: (i, 0)),
                  mods.spec(layer, 5, tm)],
        out_specs=pl.BlockSpec((tm, d), lambda i: (i, 0)),
        scratch_shapes=[pltpu.VMEM((2, 2, tm, d), F32), pltpu.SemaphoreType.DMA((2,))],
        compiler_params=_cparams(("arbitrary",)),
        name="moe_combine",
    )(dest3, dest3, y, x, route, mods.mods)


def _moe_layer(x, mods, layer, norm_g, w_group, b_group, w_expert, b_expert, w_gate, w_up, w_down):
    n, d = x.shape
    ne = MOE_EXPERTS
    pad = ROUTE_LANES - MOE_GROUPS - ne
    w_route = jnp.concatenate([w_group, w_expert, jnp.zeros((d, pad), F32)], axis=1)
    b_route = jnp.concatenate([b_group, b_expert, jnp.zeros((pad,), F32)]).reshape(1, ROUTE_LANES)
    h, route, counts = _router(x, norm_g, mods, layer, w_route, b_route)
    cnt = counts[0, _R_LOGIT0:_R_LOGIT0 + ne].astype(jnp.int32)
    pcnt = (cnt + MOE_TM - 1) // MOE_TM * MOE_TM
    pends = jnp.cumsum(pcnt)
    pstarts = pends - pcnt
    experts = jnp.arange(ne, dtype=jnp.int32)
    eid = route[:, _R_EID:_R_EID + 2].astype(jnp.int32)
    rank = route[:, _R_RANK:_R_RANK + 2].astype(jnp.int32)
    dest = (jnp.sum(jnp.where(eid[..., None] == experts, pstarts, 0), axis=-1) + rank).reshape(-1)
    n_blocks = (n * 2) // MOE_TM + ne
    n_pad = n_blocks * MOE_TM
    block_row = jnp.arange(n_blocks, dtype=jnp.int32) * MOE_TM
    block_expert = jnp.minimum(jnp.sum((pends[None, :] <= block_row[:, None]).astype(jnp.int32), axis=1), ne - 1)
    block_first = jnp.concatenate([jnp.ones((1,), jnp.int32),
                                   (block_expert[1:] != block_expert[:-1]).astype(jnp.int32)])
    later = (experts[None, :] > experts[:, None]) & (pcnt[None, :] > 0)
    next_expert = jnp.min(jnp.where(later, experts[None, :], ne), axis=1)
    next_expert = jnp.where(next_expert == ne, -1, next_expert)
    block_next = jnp.sum(jnp.where(block_expert[:, None] == experts, next_expert, 0), axis=1)
    n_used = (pends[-1:] // MOE_TM).astype(jnp.int32)
    xs = _dispatch(h, dest, pcnt, pends, n_pad)
    y = _experts(xs, block_expert, block_first, block_next, n_used, w_gate, w_up, w_down, layer)
    return _combine(x, y, dest, route, mods, layer)


class _Streams:
    def __init__(self, batch, seq, dec_batch, dec_seq):
        self.ctx = (0, batch, seq)
        self.lat = (batch * seq, dec_batch, dec_seq)


def _s5_layer(x, mods, layer, streams, norm_g, lam_re, lam_im, log_step, b_re, b_im, c_re, c_im, d_skip,
              w_a, b_a, w_b, b_b, s0_re, s0_im):
    n, d = x.shape
    g, p = lam_re.shape[1:]
    operands = _s5_params(lam_re, lam_im, log_step, b_re, b_im, c_re, c_im)
    h = _adanorm(x, norm_g, mods, layer, 0, 1)
    dvec = d_skip.reshape(1, d)
    row0, nseq, seq = streams.ctx
    zero = jnp.zeros((2, d // LANES, nseq, operands[3].shape[-1]), F32)
    z, sf = _s5_scan(h, operands, dvec, zero, row0, nseq, seq)
    row0, nseq, seq = streams.lat
    z, _ = _s5_scan(h, operands, dvec, _s5_state_to_lanes(s0_re, s0_im), row0, nseq, seq, z_prev=z)
    x = _mm_residual(z, [w_a.astype(BF16), w_b.astype(BF16)], [b_a, b_b], x, mods, layer, 2)
    new_re, new_im = _s5_state_from_lanes(sf, g, p)
    return x, new_re, new_im


def _mlstm_layer(x, mods, layer, streams, norm_g, w_in, b_gates, head_g, w_out, c0, n0, m0, c_slot):
    n, d = x.shape
    hh = ML_HEADS
    dv = d // hh
    dk = dv // 2
    qk = hh * dk
    n_main = 2 * qk + 2 * d
    col_scale = jnp.concatenate([jnp.ones((qk,), F32), jnp.full((qk,), dk ** -0.5, F32),
                                 jnp.ones((2 * d,), F32)])
    w_main = (w_in[:, :n_main] * col_scale).astype(BF16)
    w_gates = jnp.pad(w_in[:, n_main:], ((0, 0), (0, LANES - 4 * hh)))
    proj, gates = _mm_adanorm(x, norm_g, mods, layer, 0, 1, w_main, w_gates)
    k_t = jnp.transpose(proj[:, qk:2 * qk])
    g_c, w_c, e_c, d_r, w_r = _gate_prep(gates, b_gates, ML_T)
    fwd, bwd = slice(_G_FWD, _G_FWD + hh), slice(_G_BWD, _G_BWD + hh)
    cols = jnp.stack([g_c[:, fwd], w_c[:, fwd], e_c[:, fwd], g_c[:, bwd], w_c[:, bwd], e_c[:, bwd]], axis=-1)
    cols = jnp.pad(cols, ((0, 0), (0, 0), (0, 2)))
    rows = jnp.stack([d_r[fwd], w_r[fwd], d_r[bwd], w_r[bwd]], axis=1)

    def gate_views(row0, nseq, seq):
        gc = cols[row0:row0 + nseq * seq].reshape(nseq, seq, hh, 8)
        gr = rows[:, :, row0:row0 + nseq * seq].reshape(hh, 4, nseq, seq)
        return jnp.transpose(gc, (0, 2, 1, 3)), jnp.transpose(gr, (2, 0, 1, 3))

    def n_lanes(nv):
        return jnp.pad(jnp.transpose(nv, (0, 2, 1, 3))[..., None], ((0, 0),) * 4 + ((0, LANES - 1),))

    def m_lanes(mv):
        return jnp.broadcast_to(jnp.transpose(mv, (0, 2, 1))[..., None], mv.shape[:1] + (hh, 2, LANES))

    row0, nseq, seq = streams.ctx
    gcol, grow = gate_views(row0, nseq, seq)
    hn, c_f, n_f, m_f = _mlstm(proj, k_t, gcol, grow, head_g, None, None, None, row0, nseq, seq, c_slot=c_slot)
    row0, nseq, seq = streams.lat
    gcol, grow = gate_views(row0, nseq, seq)
    hn, _, _, _ = _mlstm(proj, k_t, gcol, grow, head_g, c0, n_lanes(n0), m_lanes(m0),
                         row0, nseq, seq, hn_prev=hn)
    x = _mm_residual(hn, [w_out.astype(BF16)], None, x, mods, layer, 2)
    return x, c_f, jnp.transpose(n_f[..., 0], (0, 2, 1, 3)), jnp.transpose(m_f[..., 0], (0, 2, 1))


def kernel(x_prompt, x_sample, state_s5_re, state_s5_im, state_mlstm_C, state_mlstm_n, state_mlstm_m, c, c_ctx, w_ada, b_ada, norm1_g, norm2_g, final_norm_g, s5_lambda_re, s5_lambda_im, s5_log_step, s5_b_re, s5_b_im, s5_c_re, s5_c_im, s5_d, s5_w_glu_a, s5_b_glu_a, s5_w_glu_b, s5_b_glu_b, ml_w_in, ml_b_gates, ml_head_norm_g, ml_w_out, moe_w_group, moe_b_group, moe_w_expert, moe_b_expert, moe_w_gate, moe_w_up, moe_w_down):
    batch, seq, d = x_prompt.shape
    dec_batch, dec_seq, _ = x_sample.shape
    depth = w_ada.shape[0]
    n_ctx = batch * seq
    streams = _Streams(batch, seq, dec_batch, dec_seq)

    cond = jnp.concatenate([c_ctx[None, :], c, jnp.zeros((MOD_ROWS - 1 - dec_batch, d), F32)], axis=0)
    mods = _modulation(cond, w_ada, b_ada).reshape(depth * MOD_ROWS * N_MOD, 1, d)
    mods = _Mods(mods, n_ctx, dec_seq)

    x = _embed(x_prompt, x_sample)
    new_s5_re, new_s5_im, new_n, new_m = [], [], [], []
    new_c = None
    n_ml_layers = depth // N_MIXERS
    for l in range(depth):
        j = l // N_MIXERS
        if l % N_MIXERS == 0:
            x, s_re, s_im = _s5_layer(
                x, mods, l, streams, norm1_g[l], s5_lambda_re[j], s5_lambda_im[j], s5_log_step[j],
                s5_b_re[j], s5_b_im[j], s5_c_re[j], s5_c_im[j], s5_d[j],
                s5_w_glu_a[j], s5_b_glu_a[j], s5_w_glu_b[j], s5_b_glu_b[j],
                state_s5_re[:, j], state_s5_im[:, j])
            new_s5_re.append(s_re)
            new_s5_im.append(s_im)
        else:
            x, new_c, n_f, m_f = _mlstm_layer(
                x, mods, l, streams, norm1_g[l], ml_w_in[j], ml_b_gates[j], ml_head_norm_g[j], ml_w_out[j],
                state_mlstm_C[:, j], state_mlstm_n[:, j], state_mlstm_m[:, j], (j, n_ml_layers, new_c))
            new_n.append(n_f)
            new_m.append(m_f)
        x = _moe_layer(x, mods, l, norm2_g[l], moe_w_group[l], moe_b_group[l], moe_w_expert[l],
                       moe_b_expert[l], moe_w_gate, moe_w_up, moe_w_down)
    y_prompt = _final_norm(x, final_norm_g, 0, n_ctx).reshape(batch, seq, d)
    y_sample = _final_norm(x, final_norm_g, n_ctx, dec_batch * dec_seq).reshape(dec_batch, dec_seq, d)
    return (y_prompt, y_sample, jnp.stack(new_s5_re, axis=1), jnp.stack(new_s5_im, axis=1),
            new_c, jnp.stack(new_n, axis=1), jnp.stack(new_m, axis=1))
```

```python
import functools
import math

import jax
import jax.numpy as jnp
from jax import lax
from jax.experimental import pallas as pl
from jax.experimental.pallas import tpu as pltpu

F32 = jnp.float32
BF16 = jnp.bfloat16

S5_GROUP_CH = 16
ML_HEADS = 8
MOE_GROUPS = 4
MOE_EPG = 8
MOE_EXPERTS = MOE_GROUPS * MOE_EPG
GRID_W = 64
POS_BASE = 10000.0
RMS_EPS = 1e-6
NEG_BIG = -1e30
N_MIXERS = 2
N_MOD = 6

LANES = 128
SUBLANES = 8
VMEM_LIMIT_BYTES = 56 * 1024 * 1024
EXPERT_VMEM_LIMIT_BYTES = 60 * 1024 * 1024

MOD_ROWS = 16
S5_LANE_GROUPS = LANES // S5_GROUP_CH
S5_TC = 64
ML_T = 256
ML_HP = 2
ROUTE_LANES = LANES
MOE_TM = 256
ROW_TM = 256
DISPATCH_TM = 512
ROW_ISSUE_UNROLL = 8


def _cparams(sem, vmem=VMEM_LIMIT_BYTES):
    return pltpu.CompilerParams(dimension_semantics=sem, vmem_limit_bytes=vmem)


def _silu(x):
    return x * jax.nn.sigmoid(x)


def _gelu_tanh(x):
    c = math.sqrt(2.0 / math.pi)
    return 0.5 * x * (1.0 + jnp.tanh(c * (x + 0.044715 * (x * x * x))))


def _ada_norm_tile(x, g, shift, scale):
    r = lax.rsqrt(jnp.mean(x * x, axis=-1, keepdims=True) + RMS_EPS)
    return (x * r * g) * (1.0 + scale) + shift


def _mod_kernel(c_ref, w_ref, b_ref, o_ref):
    o_ref[...] = _dot_3pass(_silu(c_ref[...]), w_ref[...]) + b_ref[...]


def _modulation(cond, w_ada, b_ada, tn=1024):
    depth, d, n = w_ada.shape
    return pl.pallas_call(
        _mod_kernel,
        out_shape=jax.ShapeDtypeStruct((depth, MOD_ROWS, n), F32),
        grid=(depth, n // tn),
        in_specs=[
            pl.BlockSpec((MOD_ROWS, d), lambda l, j: (0, 0)),
            pl.BlockSpec((None, d, tn), lambda l, j: (l, 0, j)),
            pl.BlockSpec((None, 1, tn), lambda l, j: (l, 0, j)),
        ],
        out_specs=pl.BlockSpec((None, MOD_ROWS, tn), lambda l, j: (l, 0, j)),
        compiler_params=_cparams(("parallel", "parallel")),
        name="modulation",
    )(cond, w_ada, b_ada.reshape(depth, 1, n))


class _Mods:
    def __init__(self, mods, n_ctx, dec_seq):
        self.mods = mods
        self.n_ctx = n_ctx
        self.dec_seq = dec_seq
        self.d = mods.shape[-1]

    def spec(self, layer, k, tm, row_offset=0):
        base = layer * MOD_ROWS * N_MOD + k
        n_ctx, dec_seq = self.n_ctx, self.dec_seq

        def index(i, *_):
            row0 = i * tm + row_offset
            r = jnp.where(row0 < n_ctx, 0, 1 + (row0 - n_ctx) // dec_seq)
            return (base + r * N_MOD, 0, 0)

        return pl.BlockSpec((None, 1, self.d), index)


def _embed_kernel(xc_ref, xl_ref, p_ref, o_ref, *, ctx_blocks):
    i = pl.program_id(0)

    @pl.when(i < ctx_blocks)
    def _():
        o_ref[...] = xc_ref[...]

    @pl.when(i >= ctx_blocks)
    def _():
        o_ref[...] = xl_ref[...] + p_ref[...]


def _grid_pos_embed(n_tokens, d):
    rows = n_tokens // GRID_W
    r, col = jnp.meshgrid(jnp.arange(rows, dtype=F32), jnp.arange(GRID_W, dtype=F32), indexing="ij")
    quarter = d // 4
    omega = 1.0 / (POS_BASE ** (jnp.arange(quarter, dtype=F32) / quarter))

    def axis_embed(pos):
        ang = pos.reshape(-1, 1) * omega[None, :]
        return jnp.concatenate([jnp.sin(ang), jnp.cos(ang)], axis=-1)

    return jnp.concatenate([axis_embed(r), axis_embed(col)], axis=-1)


def _embed(x_prompt, x_sample, tm=512):
    batch, seq, d = x_prompt.shape
    dec_batch, dec_seq, _ = x_sample.shape
    n_ctx, n_lat = batch * seq, dec_batch * dec_seq
    ctx_blocks, seq_blocks = n_ctx // tm, dec_seq // tm
    pos = _grid_pos_embed(dec_seq, d)
    return pl.pallas_call(
        functools.partial(_embed_kernel, ctx_blocks=ctx_blocks),
        out_shape=jax.ShapeDtypeStruct((n_ctx + n_lat, d), F32),
        grid=((n_ctx + n_lat) // tm,),
        in_specs=[pl.BlockSpec((tm, d), lambda i: (jnp.minimum(i, ctx_blocks - 1), 0)),
                  pl.BlockSpec((tm, d), lambda i: (jnp.maximum(i - ctx_blocks, 0), 0)),
                  pl.BlockSpec((tm, d), lambda i: (jnp.maximum(i - ctx_blocks, 0) % seq_blocks, 0))],
        out_specs=pl.BlockSpec((tm, d), lambda i: (i, 0)),
        compiler_params=_cparams(("parallel",)),
        name="embed",
    )(x_prompt.reshape(n_ctx, d), x_sample.reshape(n_lat, d), pos)


def _adanorm_kernel(x_ref, g_ref, sh_ref, sc_ref, o_ref):
    o_ref[...] = _ada_norm_tile(x_ref[...], g_ref[...], sh_ref[...], sc_ref[...]).astype(o_ref.dtype)


def _adanorm(x, g, mods, layer, k_shift, k_scale, tm=512):
    n, d = x.shape
    return pl.pallas_call(
        _adanorm_kernel,
        out_shape=jax.ShapeDtypeStruct((n, d), F32),
        grid=(n // tm,),
        in_specs=[pl.BlockSpec((tm, d), lambda i: (i, 0)),
                  pl.BlockSpec((1, d), lambda i: (0, 0)),
                  mods.spec(layer, k_shift, tm),
                  mods.spec(layer, k_scale, tm)],
        out_specs=pl.BlockSpec((tm, d), lambda i: (i, 0)),
        compiler_params=_cparams(("parallel",)),
        name="adanorm",
    )(x, g.reshape(1, d), mods.mods, mods.mods)


def _final_norm_kernel(x_ref, g_ref, o_ref):
    x = x_ref[...]
    r = lax.rsqrt(jnp.mean(x * x, axis=-1, keepdims=True) + RMS_EPS)
    o_ref[...] = x * r * g_ref[...]


def _final_norm(x, g, row0, nrows, tm=512):
    d = x.shape[1]
    off = row0 // tm
    return pl.pallas_call(
        _final_norm_kernel,
        out_shape=jax.ShapeDtypeStruct((nrows, d), F32),
        grid=(nrows // tm,),
        in_specs=[pl.BlockSpec((tm, d), lambda i: (i + off, 0)),
                  pl.BlockSpec((1, d), lambda i: (0, 0))],
        out_specs=pl.BlockSpec((tm, d), lambda i: (i, 0)),
        compiler_params=_cparams(("parallel",)),
        name="final_norm",
    )(x, g.reshape(1, d))


def _cmul(ar, ai, br, bi):
    return ar * br - ai * bi, ar * bi + ai * br


def _s5_param_kernel(lre_ref, lim_ref, ls_ref, bre_ref, bim_ref, cre_ref, cim_ref,
                     b2_ref, c2_ref, d2_ref, a2_ref):
    h, s = bre_ref.shape
    gl = LANES // h
    p = s // gl
    lr = lre_ref[...]
    li = lim_ref[...]
    dt = jnp.exp(ls_ref[...])
    mag = jnp.exp(lr * dt)
    ar = mag * jnp.cos(li * dt)
    ai = mag * jnp.sin(li * dt)
    den = lr * lr + li * li
    zr = ((ar - 1.0) * lr + ai * li) / den
    zi = (ai * lr - (ar - 1.0) * li) / den
    a2 = _cmul(ar, ai, ar, ai)
    bb = _cmul(zr, zi, bre_ref[...], bim_ref[...])
    abb = _cmul(ar, ai, *bb)
    cc = (cre_ref[...], cim_ref[...])
    ca = _cmul(*cc, ar, ai)
    ca2 = _cmul(*cc, *a2)

    same_b = (lax.broadcasted_iota(jnp.int32, (LANES, s), 0) // h
              == lax.broadcasted_iota(jnp.int32, (LANES, s), 1) // p)
    same_c = (lax.broadcasted_iota(jnp.int32, (s, LANES), 0) // p
              == lax.broadcasted_iota(jnp.int32, (s, LANES), 1) // h)
    spread = (lax.broadcasted_iota(jnp.int32, (h, LANES), 1) % h
              == lax.broadcasted_iota(jnp.int32, (h, LANES), 0)).astype(BF16)

    def bblock(re, im):
        def one(x):
            return jnp.where(same_b, jnp.concatenate([x] * gl, axis=0), 0.0)
        return jnp.concatenate([one(re), one(im)], axis=1)

    def cblock(re, im):
        def one(x):
            t = sum(lax.dot_general(part, spread, (((0,), (0,)), ((), ())), preferred_element_type=F32)
                    for part in _split_bf16(x))
            return jnp.where(same_c, t, 0.0)
        return jnp.concatenate([one(re), one(-im)], axis=0)

    b_blk = bblock(*bb)
    ab_blk = bblock(*abb)
    c_blk = cblock(*cc)
    cb = _dot_3pass(b_blk, c_blk)
    cab = _dot_3pass(ab_blk, c_blk)
    b2_ref[...] = jnp.concatenate([ab_blk, b_blk], axis=0).astype(BF16)
    c2_ref[...] = jnp.concatenate([cblock(*ca), cblock(*ca2)], axis=1).astype(BF16)
    d2_ref[...] = jnp.concatenate([jnp.concatenate([cb, cab], axis=1),
                                   jnp.concatenate([jnp.zeros_like(cb), cb], axis=1)], axis=0).astype(BF16)
    a2_ref[...] = jnp.concatenate(a2, axis=1)


def _s5_params(lam_re, lam_im, log_step, b_re, b_im, c_re, c_im):
    _, g, p = lam_re.shape
    h = b_re.shape[-1]
    gl = S5_LANE_GROUPS
    nlc = g // gl
    s = gl * p

    def lanes(v):
        return v.reshape(2, nlc, 1, s)

    def rows(m):
        return jnp.transpose(m.reshape(2, h, nlc, s), (0, 2, 1, 3))

    step = jnp.broadcast_to(log_step[:, :, None], (2, g, p))
    vec = pl.BlockSpec((None, None, 1, s), lambda d, j: (d, j, 0, 0))
    mat = pl.BlockSpec((None, None, h, s), lambda d, j: (d, j, 0, 0))

    def out(r, c, dt):
        return (jax.ShapeDtypeStruct((2, nlc, r, c), dt),
                pl.BlockSpec((None, None, r, c), lambda d, j: (d, j, 0, 0)))

    outs = [out(2 * LANES, 2 * s, BF16), out(2 * s, 2 * LANES, BF16), out(2 * LANES, 2 * LANES, BF16),
            out(1, 2 * s, F32)]
    return pl.pallas_call(
        _s5_param_kernel,
        out_shape=tuple(o[0] for o in outs),
        grid=(2, nlc),
        in_specs=[vec, vec, vec, mat, mat, mat, mat],
        out_specs=tuple(o[1] for o in outs),
        compiler_params=_cparams(("parallel", "parallel")),
        name="s5_params",
    )(lanes(lam_re), lanes(lam_im), lanes(step),
      rows(jnp.transpose(b_re, (0, 3, 1, 2))), rows(jnp.transpose(b_im, (0, 3, 1, 2))),
      rows(jnp.transpose(c_re, (0, 2, 1, 3))), rows(jnp.transpose(c_im, (0, 2, 1, 3))))


def _s5_state_to_lanes(s_re, s_im):
    b, _, g, p = s_re.shape
    gl = S5_LANE_GROUPS
    nlc = g // gl

    def lay(s):
        return jnp.transpose(s.reshape(b, 2, nlc, gl * p), (1, 2, 0, 3))

    return jnp.concatenate([lay(s_re), lay(s_im)], axis=-1)


def _s5_state_from_lanes(s, g, p):
    _, nlc, b, s2 = s.shape
    half = s2 // 2

    def unlay(t):
        return jnp.transpose(t, (2, 0, 1, 3)).reshape(b, 2, g, p)

    return unlay(s[..., :half]), unlay(s[..., half:])


def _s5_scan_kernel(*refs, seq, tc, aliased):
    if aliased:
        h_ref, b2_ref, c2_ref, d2_ref, a_ref, d_ref, s0_ref, _, z_ref, sf_ref = refs[:10]
    else:
        h_ref, b2_ref, c2_ref, d2_ref, a_ref, d_ref, s0_ref, z_ref, sf_ref = refs[:9]
    scr = (refs[-10:-5], refs[-5:])
    nb = SUBLANES
    half = a_ref.shape[-1] // 2
    nc = seq // tc
    npair = tc // 2
    a_re = [jnp.broadcast_to(a_ref[dr][:, :half], (nb, half)) for dr in range(2)]
    a_im = [jnp.broadcast_to(a_ref[dr][:, half:], (nb, half)) for dr in range(2)]

    def chunk(c, carry):
        t0s = (c * tc, (nc - 1 - c) * tc)

        def pair_rows(dr, p):
            if dr == 0:
                first = t0s[0] + 2 * p
                return pl.ds(first, nb, stride=seq), pl.ds(first + 1, nb, stride=seq)
            first = t0s[1] + tc - 1 - 2 * p
            return pl.ds(first, nb, stride=seq), pl.ds(first - 1, nb, stride=seq)

        carry = list(carry)
        for dr in range(2):
            u_scr, bu_scr, _, _, _ = scr[dr]
            for p in range(npair):
                r1, r2 = pair_rows(dr, p)
                u_scr[p * nb:(p + 1) * nb, :LANES] = h_ref[r1, :]
                u_scr[p * nb:(p + 1) * nb, LANES:] = h_ref[r2, :]
            bu_scr[...] = jnp.dot(u_scr[...].astype(BF16), b2_ref[dr], preferred_element_type=F32)
        for dr in range(2):
            _, bu_scr, x_scr, _, _ = scr[dr]
            xr, xi = carry[2 * dr], carry[2 * dr + 1]
            for p in range(npair):
                x_scr[p * nb:(p + 1) * nb, :half] = xr
                x_scr[p * nb:(p + 1) * nb, half:] = xi
                bu = bu_scr[p * nb:(p + 1) * nb, :]
                xr, xi = (a_re[dr] * xr - a_im[dr] * xi + bu[:, :half],
                          a_re[dr] * xi + a_im[dr] * xr + bu[:, half:])
            carry[2 * dr], carry[2 * dr + 1] = xr, xi
        for dr in range(2):
            u_scr, _, x_scr, yo_scr, y_scr = scr[dr]
            yo_scr[...] = jnp.dot(x_scr[...].astype(BF16), c2_ref[dr], preferred_element_type=F32) \
                + jnp.dot(u_scr[...].astype(BF16), d2_ref[dr], preferred_element_type=F32)
            for p in range(npair):
                r1, r2 = pair_rows(dr, p)
                y_scr[r1, :] = yo_scr[p * nb:(p + 1) * nb, :LANES]
                y_scr[r2, :] = yo_scr[p * nb:(p + 1) * nb, LANES:]
        return tuple(carry)

    s0f = s0_ref[0]
    s0b = s0_ref[1]
    init = (s0f[:, :half], s0f[:, half:], s0b[:, :half], s0b[:, half:])
    xr_f, xi_f, xr_b, xi_b = lax.fori_loop(0, nc, chunk, init)
    sf_ref[0, :, :half] = xr_f
    sf_ref[0, :, half:] = xi_f
    sf_ref[1, :, :half] = xr_b
    sf_ref[1, :, half:] = xi_b
    y = h_ref[...] * d_ref[...] + scr[0][4][...] + scr[1][4][...]
    z_ref[...] = _gelu_tanh(y).astype(z_ref.dtype)


def _s5_scan(h, operands, dvec, s0, row0, nseq, seq, z_prev=None):
    b2, c2, d2, avec = operands
    n, d = h.shape
    nlc = d // LANES
    s2 = avec.shape[-1]
    nb = SUBLANES
    rows = nb * seq
    off = row0 // rows
    aliased = z_prev is not None
    in_specs = [
        pl.BlockSpec((rows, LANES), lambda i, j: (i + off, j)),
        pl.BlockSpec((2, None, 2 * LANES, s2), lambda i, j: (0, j, 0, 0)),
        pl.BlockSpec((2, None, s2, 2 * LANES), lambda i, j: (0, j, 0, 0)),
        pl.BlockSpec((2, None, 2 * LANES, 2 * LANES), lambda i, j: (0, j, 0, 0)),
        pl.BlockSpec((2, None, 1, s2), lambda i, j: (0, j, 0, 0)),
        pl.BlockSpec((1, LANES), lambda i, j: (0, j)),
        pl.BlockSpec((2, None, nb, s2), lambda i, j: (0, j, i, 0)),
    ]
    args = [h, b2, c2, d2, avec, dvec, s0]
    io_alias = {}
    if aliased:
        in_specs.append(pl.BlockSpec(memory_space=pl.ANY))
        args.append(z_prev)
        io_alias = {len(args) - 1: 0}
    npair_rows = S5_TC // 2 * nb
    return pl.pallas_call(
        functools.partial(_s5_scan_kernel, seq=seq, tc=S5_TC, aliased=aliased),
        out_shape=(jax.ShapeDtypeStruct((n, d), BF16), jax.ShapeDtypeStruct((2, nlc, nseq, s2), F32)),
        grid=(nseq // nb, nlc),
        in_specs=in_specs,
        out_specs=(pl.BlockSpec((rows, LANES), lambda i, j: (i + off, j)),
                   pl.BlockSpec((2, None, nb, s2), lambda i, j: (0, j, i, 0))),
        scratch_shapes=2 * [pltpu.VMEM((npair_rows, 2 * LANES), F32),
                            pltpu.VMEM((npair_rows, s2), F32),
                            pltpu.VMEM((npair_rows, s2), F32),
                            pltpu.VMEM((npair_rows, 2 * LANES), F32),
                            pltpu.VMEM((rows, LANES), F32)],
        input_output_aliases=io_alias,
        compiler_params=_cparams(("parallel", "parallel")),
        name="s5_scan",
    )(*args)


def _mm_res_kernel(*refs, n_w, has_bias):
    z_ref = refs[0]
    w_refs = refs[1:1 + n_w]
    pos = 1 + n_w
    b_refs = refs[pos:pos + n_w] if has_bias else ()
    pos += n_w if has_bias else 0
    x_ref, gate_ref, o_ref = refs[pos:pos + 3]
    z = z_ref[...]
    acc = [jnp.dot(z, w[...], preferred_element_type=F32) for w in w_refs]
    if has_bias:
        acc = [a + b[...] for a, b in zip(acc, b_refs)]
    y = acc[0] if n_w == 1 else acc[0] * jax.nn.sigmoid(acc[1])
    o_ref[...] = x_ref[...] + gate_ref[...] * y


def _mm_residual(z, ws, bs, x, mods, layer, k_gate, tm=1024, tn=1024):
    n, k = z.shape
    n_out = ws[0].shape[1]
    n_w = len(ws)
    has_bias = bs is not None
    in_specs = [pl.BlockSpec((tm, k), lambda i, j: (i, 0))]
    in_specs += [pl.BlockSpec((k, tn), lambda i, j: (0, j)) for _ in ws]
    args = [z, *ws]
    if has_bias:
        in_specs += [pl.BlockSpec((1, tn), lambda i, j: (0, j)) for _ in bs]
        args += [b.reshape(1, n_out) for b in bs]
    gate_spec = mods.spec(layer, k_gate, tm)
    gate_spec = pl.BlockSpec((None, 1, tn), lambda i, j, f=gate_spec.index_map: (f(i)[0], 0, j))
    in_specs += [pl.BlockSpec((tm, tn), lambda i, j: (i, j)), gate_spec]
    args += [x, mods.mods]
    return pl.pallas_call(
        functools.partial(_mm_res_kernel, n_w=n_w, has_bias=has_bias),
        out_shape=jax.ShapeDtypeStruct((n, n_out), F32),
        grid=(n // tm, n_out // tn),
        in_specs=in_specs,
        out_specs=pl.BlockSpec((tm, tn), lambda i, j: (i, j)),
        compiler_params=_cparams(("parallel", "parallel")),
        name=f"proj_residual_{n_w}w",
    )(*args)


def _split_bf16(a):
    hi = a.astype(BF16)
    return hi, (a - hi.astype(F32)).astype(BF16)


def _dot_3pass(a, b):
    a_hi, a_lo = _split_bf16(a)
    b_hi, b_lo = _split_bf16(b)
    return (jnp.dot(a_hi, b_hi, preferred_element_type=F32) + jnp.dot(a_hi, b_lo, preferred_element_type=F32)
            + jnp.dot(a_lo, b_hi, preferred_element_type=F32))


def _mm_norm_kernel(x_ref, g_ref, sh_ref, sc_ref, w_ref, ws_ref, o_ref, os_ref, h_scr):
    @pl.when(pl.program_id(1) == 0)
    def _():
        h = _ada_norm_tile(x_ref[...], g_ref[...], sh_ref[...], sc_ref[...])
        h_scr[...] = h.astype(h_scr.dtype)
        os_ref[...] = _dot_3pass(h, ws_ref[...])

    o_ref[...] = jnp.dot(h_scr[...], w_ref[...], preferred_element_type=F32).astype(o_ref.dtype)


def _mm_adanorm(x, g, mods, layer, k_shift, k_scale, w, w_side, tm=1024, tn=1024):
    n, d = x.shape
    n_out = w.shape[1]
    n_side = w_side.shape[1]
    return pl.pallas_call(
        _mm_norm_kernel,
        out_shape=(jax.ShapeDtypeStruct((n, n_out), BF16), jax.ShapeDtypeStruct((n, n_side), F32)),
        grid=(n // tm, n_out // tn),
        in_specs=[pl.BlockSpec((tm, d), lambda i, j: (i, 0)),
                  pl.BlockSpec((1, d), lambda i, j: (0, 0)),
                  mods.spec(layer, k_shift, tm),
                  mods.spec(layer, k_scale, tm),
                  pl.BlockSpec((d, tn), lambda i, j: (0, j)),
                  pl.BlockSpec((d, n_side), lambda i, j: (0, 0))],
        out_specs=(pl.BlockSpec((tm, tn), lambda i, j: (i, j)),
                   pl.BlockSpec((tm, n_side), lambda i, j: (i, 0))),
        scratch_shapes=[pltpu.VMEM((tm, d), BF16)],
        compiler_params=_cparams(("parallel", "arbitrary")),
        name="adanorm_proj",
    )(x, g.reshape(1, d), mods.mods, mods.mods, w, w_side)


_G_FWD = ML_HEADS
_G_BWD = 3 * ML_HEADS


def _gate_kernel(xc_ref, xr_ref, bc_ref, br_ref, g_ref, w_ref, e_ref, dr_ref, wr_ref):
    t = xc_ref.shape[0]
    r_i = lax.broadcasted_iota(jnp.int32, (t, t), 0)
    c_i = lax.broadcasted_iota(jnp.int32, (t, t), 1)
    lower = (c_i <= r_i).astype(F32)
    upper = (c_i >= r_i).astype(F32)
    hi = lax.Precision.HIGHEST
    xc = xc_ref[...] + bc_ref[...]
    fc = jax.nn.log_sigmoid(xc)
    lane = lax.broadcasted_iota(jnp.int32, xc.shape, 1)
    g_c = jnp.where(lane < 2 * ML_HEADS,
                    jnp.dot(lower, fc, preferred_element_type=F32, precision=hi),
                    jnp.dot(upper, fc, preferred_element_type=F32, precision=hi))
    e_c = jnp.broadcast_to(jnp.sum(fc, axis=0, keepdims=True), xc.shape)
    g_ref[...] = g_c
    e_ref[...] = e_c
    w_ref[...] = e_c - g_c + pltpu.roll(xc, ML_HEADS, 1)
    xr = xr_ref[...] + br_ref[...]
    fr = jax.nn.log_sigmoid(xr)
    row = lax.broadcasted_iota(jnp.int32, xr.shape, 0)
    g_r = jnp.where(row < 2 * ML_HEADS,
                    jnp.dot(fr, upper, preferred_element_type=F32, precision=hi),
                    jnp.dot(fr, lower, preferred_element_type=F32, precision=hi))
    i_r = pltpu.roll(xr, ML_HEADS, 0)
    dr_ref[...] = i_r - g_r
    wr_ref[...] = jnp.sum(fr, axis=1, keepdims=True) - g_r + i_r


def _gate_prep(gates, b_gates, tch):
    n = gates.shape[0]
    ng = 4 * ML_HEADS
    bias_c = jnp.pad(b_gates, (0, LANES - ng)).reshape(1, LANES)
    bias_r = b_gates.reshape(ng, 1)
    gates_r = jnp.transpose(gates[:, :ng])
    col = pl.BlockSpec((tch, LANES), lambda i: (i, 0))
    rowb = pl.BlockSpec((ng, tch), lambda i: (0, i))
    return pl.pallas_call(
        _gate_kernel,
        out_shape=(jax.ShapeDtypeStruct((n, LANES), F32),) * 3 + (jax.ShapeDtypeStruct((ng, n), F32),) * 2,
        grid=(n // tch,),
        in_specs=[col, rowb, pl.BlockSpec((1, LANES), lambda i: (0, 0)),
                  pl.BlockSpec((ng, 1), lambda i: (0, 0))],
        out_specs=(col, col, col, rowb, rowb),
        compiler_params=_cparams(("parallel",)),
        name="mlstm_gates",
    )(gates, gates_r, bias_c, bias_r)


def _mlstm_kernel(q_ref, kt_ref, v_ref, o_ref, gcol_ref, grow_ref, hg_ref,
                  c0_ref, n0_ref, m0_ref, hn_ref, cf_ref, nf_ref, mf_ref, *scratch, seq, tch, hp):
    nc = seq // tch
    dv = v_ref.shape[-1] // hp
    dk = q_ref.shape[-1] // hp
    tt = lax.broadcasted_iota(jnp.int32, (tch, tch), 0)
    ss = lax.broadcasted_iota(jnp.int32, (tch, tch), 1)
    ones_col = (lax.broadcasted_iota(jnp.int32, (tch, LANES), 1) == 0).astype(BF16)
    scr = {(hd, dr): scratch[3 * (2 * hd + dr):3 * (2 * hd + dr) + 3] for hd in range(hp) for dr in range(2)}
    for (hd, dr), (_, c_scr, vx) in scr.items():
        c_scr[:, :dv] = c0_ref[dr, hd]
        c_scr[:, dv:] = n0_ref[hd, dr]
        vx[:, dv:] = ones_col

    def chunk_dir(hd, dr, r0, m):
        hacc, c_scr, vx = scr[hd, dr]
        rows = pl.ds(pl.multiple_of(r0, tch), tch)
        q = q_ref[rows, hd * dk:(hd + 1) * dk]
        kt = kt_ref[hd * dk:(hd + 1) * dk, rows]
        vx[:, :dv] = v_ref[rows, hd * dv:(hd + 1) * dv]
        gc = gcol_ref[hd, rows, :]
        gr = grow_ref[hd, :, rows]
        g_col = gc[:, 3 * dr:3 * dr + 1]
        w_col = gc[:, 3 * dr + 1:3 * dr + 2]
        e_col = gc[:, 3 * dr + 2:3 * dr + 3]
        d_row = gr[2 * dr:2 * dr + 1, :]
        w_row = gr[2 * dr + 1:2 * dr + 2, :]
        mask = (ss <= tt) if dr == 0 else (ss >= tt)
        a_col = g_col + m
        dmat = jnp.where(mask, g_col + d_row, -jnp.inf)
        mt = jnp.maximum(a_col, jnp.max(dmat, axis=1, keepdims=True))
        qk = jnp.dot(q, kt, preferred_element_type=F32)
        s = (qk * jnp.exp(dmat - mt)).astype(BF16)
        inter = jnp.exp(a_col - mt)
        vext = vx[...]
        tot = jnp.dot(s, vext, preferred_element_type=F32) \
            + inter * jnp.dot(q, c_scr[...].astype(BF16), preferred_element_type=F32)
        den = tot[:, dv:dv + 1]
        hacc[rows, :] = tot[:, :dv] / jnp.maximum(jnp.abs(den), jnp.exp(-mt))
        g_end = jnp.max(e_col, axis=0, keepdims=True)
        m_new = jnp.maximum(g_end + m, jnp.max(w_col, axis=0, keepdims=True))
        decay = jnp.exp(g_end + m - m_new)
        kw = (kt.astype(F32) * jnp.exp(w_row - m_new)).astype(BF16)
        c_scr[...] = decay * c_scr[...] + jnp.dot(kw, vext, preferred_element_type=F32)
        return m_new

    keys = list(scr)

    def body(c, carry):
        return tuple(chunk_dir(hd, dr, (c if dr == 0 else nc - 1 - c) * tch, m)
                     for (hd, dr), m in zip(keys, carry))

    m_fin = lax.fori_loop(0, nc, body, tuple(m0_ref[hd, dr:dr + 1, 0:1] for hd, dr in keys))
    for (hd, dr), m in zip(keys, m_fin):
        c_scr = scr[hd, dr][1]
        cf_ref[dr, hd] = c_scr[:, :dv]
        nf_ref[hd, dr] = c_scr[:, dv:]
        mf_ref[hd, dr:dr + 1, :] = jnp.broadcast_to(m, (1, LANES))
    for hd in range(hp):
        cols = slice(hd * dv, (hd + 1) * dv)
        hs = scr[hd, 0][0][...] + scr[hd, 1][0][...]
        hn = hs * lax.rsqrt(jnp.mean(hs * hs, axis=-1, keepdims=True) + RMS_EPS)
        hn = hn * hg_ref[:, cols] * jax.nn.sigmoid(o_ref[:, cols].astype(F32))
        hn_ref[:, cols] = hn.astype(hn_ref.dtype)


def _mlstm(proj, k_t, gcol, grow, head_g, c0, n0, m0, row0, nseq, seq, hn_prev=None, c_slot=None, hp=ML_HP):
    n = proj.shape[0]
    h = ML_HEADS
    dk = c0.shape[-2]
    dv = c0.shape[-1]
    d = h * dv
    qk = h * dk
    off = row0 // seq
    tch = min(ML_T, seq)
    aliased = hn_prev is not None
    c_spec = pl.BlockSpec((None, 2, hp, dk, dv), lambda b, j: (b, 0, j, 0, 0))
    n_spec = pl.BlockSpec((None, hp, 2, dk, LANES), lambda b, j: (b, j, 0, 0, 0))
    m_spec = pl.BlockSpec((None, hp, 2, LANES), lambda b, j: (b, j, 0, 0))
    wk, wv = hp * dk, hp * dv
    in_specs = [
        pl.BlockSpec((seq, wk), lambda b, j: (b + off, j)),
        pl.BlockSpec((wk, seq), lambda b, j: (j, b + off)),
        pl.BlockSpec((seq, wv), lambda b, j: (b + off, 2 * qk // wv + j)),
        pl.BlockSpec((seq, wv), lambda b, j: (b + off, (2 * qk + d) // wv + j)),
        pl.BlockSpec((None, hp, seq, 8), lambda b, j: (b, j, 0, 0)),
        pl.BlockSpec((None, hp, 4, seq), lambda b, j: (b, j, 0, 0)),
        pl.BlockSpec((1, wv), lambda b, j: (0, j)),
        c_spec, n_spec, m_spec,
    ]
    args = [proj, k_t, proj, proj, gcol, grow, head_g.reshape(1, d), c0, n0, m0]
    n_in = len(args)
    io_alias = {}
    if aliased:
        in_specs.append(pl.BlockSpec(memory_space=pl.ANY))
        args.append(hn_prev)
        io_alias[len(args) - 1] = 0
    c_type = jax.ShapeDtypeStruct((nseq, 2, h, dk, dv), F32)
    c_out_spec = c_spec
    if c_slot is not None:
        slot, n_slots, c_prev = c_slot
        c_type = jax.ShapeDtypeStruct((nseq, n_slots, 2, h, dk, dv), F32)
        c_out_spec = pl.BlockSpec((None, None, 2, hp, dk, dv), lambda b, j: (b, slot, 0, j, 0, 0))
        if c_prev is not None:
            in_specs.append(pl.BlockSpec(memory_space=pl.ANY))
            args.append(c_prev)
            io_alias[len(args) - 1] = 1

    def kern(*refs):
        _mlstm_kernel(*refs[:n_in], *refs[len(args):], seq=seq, tch=tch, hp=hp)

    return pl.pallas_call(
        kern,
        out_shape=(jax.ShapeDtypeStruct((n, d), BF16), c_type,
                   jax.ShapeDtypeStruct((nseq, h, 2, dk, LANES), F32),
                   jax.ShapeDtypeStruct((nseq, h, 2, LANES), F32)),
        grid=(nseq, h // hp),
        in_specs=in_specs,
        out_specs=(pl.BlockSpec((seq, wv), lambda b, j: (b + off, j)), c_out_spec, n_spec, m_spec),
        scratch_shapes=2 * hp * [pltpu.VMEM((seq, dv), F32), pltpu.VMEM((dk, dv + LANES), F32),
                                 pltpu.VMEM((tch, dv + LANES), BF16)],
        input_output_aliases=io_alias,
        compiler_params=_cparams(("parallel", "parallel")),
        name="mlstm",
    )(*args)


_R_EID, _R_W, _R_RANK = 0, 2, 4
_R_LOGIT0 = MOE_GROUPS


def _router_kernel(x_ref, g_ref, sh_ref, sc_ref, whi_ref, wlo_ref, b_ref, h_ref, route_ref, cnt_ref, carry):
    i = pl.program_id(0)

    @pl.when(i == 0)
    def _():
        carry[...] = jnp.zeros_like(carry)

    h = _ada_norm_tile(x_ref[...], g_ref[...], sh_ref[...], sc_ref[...])
    h_ref[...] = h
    tm = h.shape[0]
    h_hi = h.astype(BF16)
    h_lo = (h - h_hi.astype(F32)).astype(BF16)
    logits = (jnp.dot(h_hi, whi_ref[...], preferred_element_type=F32)
              + jnp.dot(h_hi, wlo_ref[...], preferred_element_type=F32)
              + jnp.dot(h_lo, whi_ref[...], preferred_element_type=F32)) + b_ref[...]
    lane = lax.broadcasted_iota(jnp.int32, logits.shape, 1)
    big = jnp.int32(ROUTE_LANES)

    def first_lane(cond):
        return jnp.min(jnp.where(cond, lane, big), axis=1, keepdims=True)

    glog = jnp.where(lane < MOE_GROUPS, logits, -jnp.inf)
    ge = jnp.exp(glog - jnp.max(glog, axis=1, keepdims=True))
    pgrp = ge / jnp.sum(ge, axis=1, keepdims=True)
    pg = jnp.max(pgrp, axis=1, keepdims=True)
    grp = first_lane(pgrp == pg)
    e_lane = lane - _R_LOGIT0
    in_grp = (e_lane >= 0) & (e_lane < MOE_EXPERTS) & ((e_lane // MOE_EPG) == grp)
    elog = jnp.where(in_grp, logits, -jnp.inf)
    ee = jnp.exp(elog - jnp.max(elog, axis=1, keepdims=True))
    pe = jnp.where(in_grp, ee / jnp.sum(ee, axis=1, keepdims=True), -1.0)
    p0 = jnp.max(pe, axis=1, keepdims=True)
    l0 = first_lane(pe == p0)
    pe1 = jnp.where(lane == l0, -1.0, pe)
    p1 = jnp.max(pe1, axis=1, keepdims=True)
    l1 = first_lane(pe1 == p1)
    psum = p0 + p1
    w0 = pg * p0 / psum
    w1 = pg * p1 / psum
    onehot = ((lane == l0) | (lane == l1)).astype(BF16)
    r_i = lax.broadcasted_iota(jnp.int32, (tm, tm), 0)
    c_i = lax.broadcasted_iota(jnp.int32, (tm, tm), 1)
    tri = (c_i < r_i).astype(BF16)
    before = jnp.dot(tri, onehot, preferred_element_type=F32) + carry[...]
    rank0 = jnp.sum(jnp.where(lane == l0, before, 0.0), axis=1, keepdims=True)
    rank1 = jnp.sum(jnp.where(lane == l1, before, 0.0), axis=1, keepdims=True)
    carry[...] = carry[...] + jnp.sum(onehot.astype(F32), axis=0, keepdims=True)
    cnt_ref[...] = carry[...]
    cols = [(l0 - _R_LOGIT0).astype(F32), (l1 - _R_LOGIT0).astype(F32), w0, w1, rank0, rank1]
    route = jnp.zeros(logits.shape, F32)
    for c, val in enumerate(cols):
        route = jnp.where(lane == c, val, route)
    route_ref[...] = route


def _router(x, g, mods, layer, w_route, b_route, tm=512):
    n, d = x.shape
    w_hi = w_route.astype(BF16)
    w_lo = (w_route - w_hi.astype(F32)).astype(BF16)
    return pl.pallas_call(
        _router_kernel,
        out_shape=(jax.ShapeDtypeStruct((n, d), F32),
                   jax.ShapeDtypeStruct((n, ROUTE_LANES), F32),
                   jax.ShapeDtypeStruct((1, ROUTE_LANES), F32)),
        grid=(n // tm,),
        in_specs=[pl.BlockSpec((tm, d), lambda i: (i, 0)),
                  pl.BlockSpec((1, d), lambda i: (0, 0)),
                  mods.spec(layer, 3, tm),
                  mods.spec(layer, 4, tm),
                  pl.BlockSpec((d, ROUTE_LANES), lambda i: (0, 0)),
                  pl.BlockSpec((d, ROUTE_LANES), lambda i: (0, 0)),
                  pl.BlockSpec((1, ROUTE_LANES), lambda i: (0, 0))],
        out_specs=(pl.BlockSpec((tm, d), lambda i: (i, 0)),
                   pl.BlockSpec((tm, ROUTE_LANES), lambda i: (i, 0)),
                   pl.BlockSpec((1, ROUTE_LANES), lambda i: (0, 0))),
        scratch_shapes=[pltpu.VMEM((1, ROUTE_LANES), F32)],
        compiler_params=_cparams(("arbitrary",)),
        name="moe_router",
    )(x, g.reshape(1, d), mods.mods, mods.mods, w_hi, w_lo, b_route)


def _dispatch_kernel(pcnt_ref, pend_ref, dest_ref, h_ref, xs_ref, zbuf, sem, zsem):
    tm = h_ref.shape[0]
    zrows = zbuf.shape[0]

    @pl.when(pl.program_id(0) == 0)
    def _():
        zbuf[...] = jnp.zeros_like(zbuf)

        def clear(e, _):
            @pl.when(pcnt_ref[e] > 0)
            def _():
                start = pl.multiple_of(pend_ref[e] - zrows, zrows)
                cp = pltpu.make_async_copy(zbuf, xs_ref.at[pl.ds(start, zrows)], zsem)
                cp.start()
                cp.wait()
            return 0

        lax.fori_loop(0, MOE_EXPERTS, clear, 0)

    def issue(r, _):
        for k in range(2):
            d = dest_ref[0, 0, 2 * r + k]
            pltpu.make_async_copy(h_ref.at[pl.ds(r, 1)], xs_ref.at[pl.ds(d, 1)], sem).start(priority=k)
        return 0

    lax.fori_loop(0, tm, issue, 0, unroll=ROW_ISSUE_UNROLL)
    for _ in range(2):
        pltpu.make_async_copy(h_ref, xs_ref.at[pl.ds(0, tm)], sem).wait()


def _dispatch(h, dest, pcnt, pends, n_pad, tm=DISPATCH_TM):
    n, d = h.shape
    nblk = n // tm
    grid_spec = pltpu.PrefetchScalarGridSpec(
        num_scalar_prefetch=2,
        grid=(nblk,),
        in_specs=[pl.BlockSpec((1, 1, 2 * tm), lambda i, *_: (i, 0, 0), memory_space=pltpu.SMEM),
                  pl.BlockSpec((tm, d), lambda i, *_: (i, 0))],
        out_specs=pl.BlockSpec(memory_space=pl.ANY),
        scratch_shapes=[pltpu.VMEM((MOE_TM, d), F32), pltpu.SemaphoreType.DMA(()),
                        pltpu.SemaphoreType.DMA(())],
    )
    return pl.pallas_call(
        _dispatch_kernel,
        out_shape=jax.ShapeDtypeStruct((n_pad, d), F32),
        grid_spec=grid_spec,
        compiler_params=_cparams(("arbitrary",)),
        name="moe_dispatch",
    )(pcnt, pends, dest.reshape(nblk, 1, 2 * tm), h)


def _experts_kernel(be_ref, first_ref, nxt_ref, nu_ref, x_ref, wg_hbm, wu_hbm, wd_hbm, o_ref,
                    stg_g, stg_u, stg_d, wg_bf, wu_bf, wd_bf, sem, *, layer, cast_rows):
    i = pl.program_id(0)
    active = i < nu_ref[0]
    stages = ((wg_hbm, stg_g, wg_bf), (wu_hbm, stg_u, wu_bf), (wd_hbm, stg_d, wd_bf))

    def weight_copies(e):
        return [pltpu.make_async_copy(hbm.at[layer, e], stg, sem.at[k])
                for k, (hbm, stg, _) in enumerate(stages)]

    @pl.when(i == 0)
    def _():
        for cp in weight_copies(be_ref[0]):
            cp.start()

    @pl.when(active & (first_ref[i] == 1))
    def _():
        for cp in weight_copies(be_ref[i]):
            cp.wait()
        for _, stg, wbf in stages:
            def cast(r, _, stg=stg, wbf=wbf):
                rows = pl.ds(pl.multiple_of(r * cast_rows, cast_rows), cast_rows)
                wbf[rows, :] = stg[rows, :].astype(BF16)
                return 0

            lax.fori_loop(0, stg.shape[0] // cast_rows, cast, 0)

        @pl.when(nxt_ref[i] >= 0)
        def _():
            for cp in weight_copies(nxt_ref[i]):
                cp.start()

    @pl.when(active)
    def _():
        x = x_ref[...].astype(BF16)
        g = jnp.dot(x, wg_bf[...], preferred_element_type=F32)
        u = jnp.dot(x, wu_bf[...], preferred_element_type=F32)
        a = (_silu(g) * u).astype(BF16)
        o_ref[...] = jnp.dot(a, wd_bf[...], preferred_element_type=F32)

    @pl.when(jnp.logical_not(active))
    def _():
        o_ref[...] = jnp.zeros_like(o_ref)


def _experts(xs, block_expert, block_first, block_next, n_used, w_gate, w_up, w_down, layer, tm=MOE_TM):
    n_pad, d = xs.shape
    f = w_gate.shape[-1]
    nblk = n_pad // tm
    grid_spec = pltpu.PrefetchScalarGridSpec(
        num_scalar_prefetch=4,
        grid=(nblk,),
        in_specs=[pl.BlockSpec((tm, d), lambda i, be, fi, nx, nu: (jnp.minimum(i, nu[0] - 1), 0)),
                  pl.BlockSpec(memory_space=pl.ANY),
                  pl.BlockSpec(memory_space=pl.ANY),
                  pl.BlockSpec(memory_space=pl.ANY)],
        out_specs=pl.BlockSpec((tm, d), lambda i, *_: (i, 0)),
        scratch_shapes=[pltpu.VMEM((d, f), F32), pltpu.VMEM((d, f), F32), pltpu.VMEM((f, d), F32),
                        pltpu.VMEM((d, f), BF16), pltpu.VMEM((d, f), BF16), pltpu.VMEM((f, d), BF16),
                        pltpu.SemaphoreType.DMA((3,))],
    )
    return pl.pallas_call(
        functools.partial(_experts_kernel, layer=layer, cast_rows=256),
        out_shape=jax.ShapeDtypeStruct((n_pad, d), F32),
        grid_spec=grid_spec,
        compiler_params=_cparams(("arbitrary",), vmem=EXPERT_VMEM_LIMIT_BYTES),
        name="moe_experts",
    )(block_expert, block_first, block_next, n_used, xs, w_gate, w_up, w_down)


def _combine_kernel(dest_ref, dnext_ref, y_hbm, x_ref, route_ref, gate_ref, o_ref, buf, sem):
    i = pl.program_id(0)
    nblk = pl.num_programs(0)
    tm = x_ref.shape[0]

    def issue_block(idx_ref, slot):
        def issue(r, _):
            for k in range(2):
                d = idx_ref[0, 0, 2 * r + k]
                pltpu.make_async_copy(y_hbm.at[pl.ds(d, 1)], buf.at[slot, k, pl.ds(r, 1)],
                                      sem.at[slot]).start(priority=k)
            return 0

        lax.fori_loop(0, tm, issue, 0, unroll=ROW_ISSUE_UNROLL)

    @pl.when(i == 0)
    def _():
        issue_block(dest_ref, 0)

    @pl.when(i + 1 < nblk)
    def _():
        issue_block(dnext_ref, (i + 1) % 2)

    slot = i % 2
    for k in range(2):
        pltpu.make_async_copy(y_hbm.at[pl.ds(0, tm)], buf.at[slot, k], sem.at[slot]).wait()
    route = route_ref[...]
    w0 = route[:, _R_W:_R_W + 1]
    w1 = route[:, _R_W + 1:_R_W + 2]
    o_ref[...] = x_ref[...] + gate_ref[...] * (buf[slot, 0] * w0 + buf[slot, 1] * w1)


def _combine(x, y, dest, route, mods, layer, tm=ROW_TM):
    n, d = x.shape
    nblk = n // tm
    dest3 = dest.reshape(nblk, 1, 2 * tm)
    return pl.pallas_call(
        _combine_kernel,
        out_shape=jax.ShapeDtypeStruct((n, d), F32),
        grid=(nblk,),
        in_specs=[pl.BlockSpec((1, 1, 2 * tm), lambda i: (i, 0, 0), memory_space=pltpu.SMEM),
                  pl.BlockSpec((1, 1, 2 * tm), lambda i: (jnp.minimum(i + 1, nblk - 1), 0, 0),
                               memory_space=pltpu.SMEM),
                  pl.BlockSpec(memory_space=pl.ANY),
                  pl.BlockSpec((tm, d), lambda i: (i, 0)),
                  pl.BlockSpec((tm, ROUTE_LANES), lambda i: (i, 0)),
                  mods.spec(layer, 5, tm)],
        out_specs=pl.BlockSpec((tm, d), lambda i: (i, 0)),
        scratch_shapes=[pltpu.VMEM((2, 2, tm, d), F32), pltpu.SemaphoreType.DMA((2,))],
        compiler_params=_cparams(("arbitrary",)),
        name="moe_combine",
    )(dest3, dest3, y, x, route, mods.mods)


def _moe_layer(x, mods, layer, norm_g, w_group, b_group, w_expert, b_expert, w_gate, w_up, w_down):
    n, d = x.shape
    ne = MOE_EXPERTS
    pad = ROUTE_LANES - MOE_GROUPS - ne
    w_route = jnp.concatenate([w_group, w_expert, jnp.zeros((d, pad), F32)], axis=1)
    b_route = jnp.concatenate([b_group, b_expert, jnp.zeros((pad,), F32)]).reshape(1, ROUTE_LANES)
    h, route, counts = _router(x, norm_g, mods, layer, w_route, b_route)
    cnt = counts[0, _R_LOGIT0:_R_LOGIT0 + ne].astype(jnp.int32)
    pcnt = (cnt + MOE_TM - 1) // MOE_TM * MOE_TM
    pends = jnp.cumsum(pcnt)
    pstarts = pends - pcnt
    experts = jnp.arange(ne, dtype=jnp.int32)
    eid = route[:, _R_EID:_R_EID + 2].astype(jnp.int32)
    rank = route[:, _R_RANK:_R_RANK + 2].astype(jnp.int32)
    dest = (jnp.sum(jnp.where(eid[..., None] == experts, pstarts, 0), axis=-1) + rank).reshape(-1)
    n_blocks = (n * 2) // MOE_TM + ne
    n_pad = n_blocks * MOE_TM
    block_row = jnp.arange(n_blocks, dtype=jnp.int32) * MOE_TM
    block_expert = jnp.minimum(jnp.sum((pends[None, :] <= block_row[:, None]).astype(jnp.int32), axis=1), ne - 1)
    block_first = jnp.concatenate([jnp.ones((1,), jnp.int32),
                                   (block_expert[1:] != block_expert[:-1]).astype(jnp.int32)])
    later = (experts[None, :] > experts[:, None]) & (pcnt[None, :] > 0)
    next_expert = jnp.min(jnp.where(later, experts[None, :], ne), axis=1)
    next_expert = jnp.where(next_expert == ne, -1, next_expert)
    block_next = jnp.sum(jnp.where(block_expert[:, None] == experts, next_expert, 0), axis=1)
    n_used = (pends[-1:] // MOE_TM).astype(jnp.int32)
    xs = _dispatch(h, dest, pcnt, pends, n_pad)
    y = _experts(xs, block_expert, block_first, block_next, n_used, w_gate, w_up, w_down, layer)
    return _combine(x, y, dest, route, mods, layer)


class _Streams:
    def __init__(self, batch, seq, dec_batch, dec_seq):
        self.ctx = (0, batch, seq)
        self.lat = (batch * seq, dec_batch, dec_seq)


def _s5_layer(x, mods, layer, streams, norm_g, lam_re, lam_im, log_step, b_re, b_im, c_re, c_im, d_skip,
              w_a, b_a, w_b, b_b, s0_re, s0_im):
    n, d = x.shape
    g, p = lam_re.shape[1:]
    operands = _s5_params(lam_re, lam_im, log_step, b_re, b_im, c_re, c_im)
    h = _adanorm(x, norm_g, mods, layer, 0, 1)
    dvec = d_skip.reshape(1, d)
    row0, nseq, seq = streams.ctx
    zero = jnp.zeros((2, d // LANES, nseq, operands[3].shape[-1]), F32)
    z, sf = _s5_scan(h, operands, dvec, zero, row0, nseq, seq)
    row0, nseq, seq = streams.lat
    z, _ = _s5_scan(h, operands, dvec, _s5_state_to_lanes(s0_re, s0_im), row0, nseq, seq, z_prev=z)
    x = _mm_residual(z, [w_a.astype(BF16), w_b.astype(BF16)], [b_a, b_b], x, mods, layer, 2)
    new_re, new_im = _s5_state_from_lanes(sf, g, p)
    return x, new_re, new_im


def _mlstm_layer(x, mods, layer, streams, norm_g, w_in, b_gates, head_g, w_out, c0, n0, m0, c_slot):
    n, d = x.shape
    hh = ML_HEADS
    dv = d // hh
    dk = dv // 2
    qk = hh * dk
    n_main = 2 * qk + 2 * d
    col_scale = jnp.concatenate([jnp.ones((qk,), F32), jnp.full((qk,), dk ** -0.5, F32),
                                 jnp.ones((2 * d,), F32)])
    w_main = (w_in[:, :n_main] * col_scale).astype(BF16)
    w_gates = jnp.pad(w_in[:, n_main:], ((0, 0), (0, LANES - 4 * hh)))
    proj, gates = _mm_adanorm(x, norm_g, mods, layer, 0, 1, w_main, w_gates)
    k_t = jnp.transpose(proj[:, qk:2 * qk])
    g_c, w_c, e_c, d_r, w_r = _gate_prep(gates, b_gates, ML_T)
    fwd, bwd = slice(_G_FWD, _G_FWD + hh), slice(_G_BWD, _G_BWD + hh)
    cols = jnp.stack([g_c[:, fwd], w_c[:, fwd], e_c[:, fwd], g_c[:, bwd], w_c[:, bwd], e_c[:, bwd]], axis=-1)
    cols = jnp.pad(cols, ((0, 0), (0, 0), (0, 2)))
    rows = jnp.stack([d_r[fwd], w_r[fwd], d_r[bwd], w_r[bwd]], axis=1)

    def gate_views(row0, nseq, seq):
        gc = cols[row0:row0 + nseq * seq].reshape(nseq, seq, hh, 8)
        gr = rows[:, :, row0:row0 + nseq * seq].reshape(hh, 4, nseq, seq)
        return jnp.transpose(gc, (0, 2, 1, 3)), jnp.transpose(gr, (2, 0, 1, 3))

    def n_lanes(nv):
        return jnp.pad(jnp.transpose(nv, (0, 2, 1, 3))[..., None], ((0, 0),) * 4 + ((0, LANES - 1),))

    def m_lanes(mv):
        return jnp.broadcast_to(jnp.transpose(mv, (0, 2, 1))[..., None], mv.shape[:1] + (hh, 2, LANES))

    row0, nseq, seq = streams.ctx
    gcol, grow = gate_views(row0, nseq, seq)
    hn, c_f, n_f, m_f = _mlstm(
        proj, k_t, gcol, grow, head_g,
        jnp.zeros((nseq, 2, hh, dk, dv), F32), jnp.zeros((nseq, hh, 2, dk, LANES), F32),
        jnp.full((nseq, hh, 2, LANES), NEG_BIG, F32), row0, nseq, seq, c_slot=c_slot)
    row0, nseq, seq = streams.lat
    gcol, grow = gate_views(row0, nseq, seq)
    hn, _, _, _ = _mlstm(proj, k_t, gcol, grow, head_g, c0, n_lanes(n0), m_lanes(m0),
                         row0, nseq, seq, hn_prev=hn)
    x = _mm_residual(hn, [w_out.astype(BF16)], None, x, mods, layer, 2)
    return x, c_f, jnp.transpose(n_f[..., 0], (0, 2, 1, 3)), jnp.transpose(m_f[..., 0], (0, 2, 1))


def kernel(x_prompt, x_sample, state_s5_re, state_s5_im, state_mlstm_C, state_mlstm_n, state_mlstm_m, c, c_ctx, w_ada, b_ada, norm1_g, norm2_g, final_norm_g, s5_lambda_re, s5_lambda_im, s5_log_step, s5_b_re, s5_b_im, s5_c_re, s5_c_im, s5_d, s5_w_glu_a, s5_b_glu_a, s5_w_glu_b, s5_b_glu_b, ml_w_in, ml_b_gates, ml_head_norm_g, ml_w_out, moe_w_group, moe_b_group, moe_w_expert, moe_b_expert, moe_w_gate, moe_w_up, moe_w_down):
    batch, seq, d = x_prompt.shape
    dec_batch, dec_seq, _ = x_sample.shape
    depth = w_ada.shape[0]
    n_ctx = batch * seq
    streams = _Streams(batch, seq, dec_batch, dec_seq)

    cond = jnp.concatenate([c_ctx[None, :], c, jnp.zeros((MOD_ROWS - 1 - dec_batch, d), F32)], axis=0)
    mods = _modulation(cond, w_ada, b_ada).reshape(depth * MOD_ROWS * N_MOD, 1, d)
    mods = _Mods(mods, n_ctx, dec_seq)

    x = _embed(x_prompt, x_sample)
    new_s5_re, new_s5_im, new_n, new_m = [], [], [], []
    new_c = None
    n_ml_layers = depth // N_MIXERS
    for l in range(depth):
        j = l // N_MIXERS
        if l % N_MIXERS == 0:
            x, s_re, s_im = _s5_layer(
                x, mods, l, streams, norm1_g[l], s5_lambda_re[j], s5_lambda_im[j], s5_log_step[j],
                s5_b_re[j], s5_b_im[j], s5_c_re[j], s5_c_im[j], s5_d[j],
                s5_w_glu_a[j], s5_b_glu_a[j], s5_w_glu_b[j], s5_b_glu_b[j],
                state_s5_re[:, j], state_s5_im[:, j])
            new_s5_re.append(s_re)
            new_s5_im.append(s_im)
        else:
            x, new_c, n_f, m_f = _mlstm_layer(
                x, mods, l, streams, norm1_g[l], ml_w_in[j], ml_b_gates[j], ml_head_norm_g[j], ml_w_out[j],
                state_mlstm_C[:, j], state_mlstm_n[:, j], state_mlstm_m[:, j], (j, n_ml_layers, new_c))
            new_n.append(n_f)
            new_m.append(m_f)
        x = _moe_layer(x, mods, l, norm2_g[l], moe_w_group[l], moe_b_group[l], moe_w_expert[l],
                       moe_b_expert[l], moe_w_gate, moe_w_up, moe_w_down)
    y_prompt = _final_norm(x, final_norm_g, 0, n_ctx).reshape(batch, seq, d)
    y_sample = _final_norm(x, final_norm_g, n_ctx, dec_batch * dec_seq).reshape(dec_batch, dec_seq, d)
    return (y_prompt, y_sample, jnp.stack(new_s5_re, axis=1), jnp.stack(new_s5_im, axis=1),
            new_c, jnp.stack(new_n, axis=1), jnp.stack(new_m, axis=1))
```

```python
import functools
import math

import jax
import jax.numpy as jnp
from jax import lax
from jax.experimental import pallas as pl
from jax.experimental.pallas import tpu as pltpu

F32 = jnp.float32
BF16 = jnp.bfloat16

S5_GROUP_CH = 16
ML_HEADS = 8
MOE_GROUPS = 4
MOE_EPG = 8
MOE_EXPERTS = MOE_GROUPS * MOE_EPG
GRID_W = 64
POS_BASE = 10000.0
RMS_EPS = 1e-6
NEG_BIG = -1e30
N_MIXERS = 2
N_MOD = 6

LANES = 128
SUBLANES = 8
VMEM_LIMIT_BYTES = 56 * 1024 * 1024
EXPERT_VMEM_LIMIT_BYTES = 60 * 1024 * 1024

MOD_ROWS = 16
S5_LANE_GROUPS = LANES // S5_GROUP_CH
S5_TC = 64
ML_T = 256
ML_HP = 2
ROUTE_LANES = LANES
MOE_TM = 256
ROW_TM = 512
DISPATCH_TM = 1024
ROW_ISSUE_UNROLL = 8


def _cparams(sem, vmem=VMEM_LIMIT_BYTES):
    return pltpu.CompilerParams(dimension_semantics=sem, vmem_limit_bytes=vmem)


def _silu(x):
    return x * jax.nn.sigmoid(x)


def _gelu_tanh(x):
    c = math.sqrt(2.0 / math.pi)
    return 0.5 * x * (1.0 + jnp.tanh(c * (x + 0.044715 * (x * x * x))))


def _ada_norm_tile(x, g, shift, scale):
    r = lax.rsqrt(jnp.mean(x * x, axis=-1, keepdims=True) + RMS_EPS)
    return (x * r * g) * (1.0 + scale) + shift


def _mod_kernel(c_ref, w_ref, b_ref, o_ref):
    o_ref[...] = _dot_3pass(_silu(c_ref[...]), w_ref[...]) + b_ref[...]


def _modulation(cond, w_ada, b_ada, tn=1024):
    depth, d, n = w_ada.shape
    return pl.pallas_call(
        _mod_kernel,
        out_shape=jax.ShapeDtypeStruct((depth, MOD_ROWS, n), F32),
        grid=(depth, n // tn),
        in_specs=[
            pl.BlockSpec((MOD_ROWS, d), lambda l, j: (0, 0)),
            pl.BlockSpec((None, d, tn), lambda l, j: (l, 0, j)),
            pl.BlockSpec((None, 1, tn), lambda l, j: (l, 0, j)),
        ],
        out_specs=pl.BlockSpec((None, MOD_ROWS, tn), lambda l, j: (l, 0, j)),
        compiler_params=_cparams(("parallel", "parallel")),
        name="modulation",
    )(cond, w_ada, b_ada.reshape(depth, 1, n))


class _Mods:
    def __init__(self, mods, n_ctx, dec_seq):
        self.mods = mods
        self.n_ctx = n_ctx
        self.dec_seq = dec_seq
        self.d = mods.shape[-1]

    def spec(self, layer, k, tm, row_offset=0):
        base = layer * MOD_ROWS * N_MOD + k
        n_ctx, dec_seq = self.n_ctx, self.dec_seq

        def index(i, *_):
            row0 = i * tm + row_offset
            r = jnp.where(row0 < n_ctx, 0, 1 + (row0 - n_ctx) // dec_seq)
            return (base + r * N_MOD, 0, 0)

        return pl.BlockSpec((None, 1, self.d), index)


def _embed_kernel(xc_ref, xl_ref, p_ref, o_ref, *, ctx_blocks):
    i = pl.program_id(0)

    @pl.when(i < ctx_blocks)
    def _():
        o_ref[...] = xc_ref[...]

    @pl.when(i >= ctx_blocks)
    def _():
        o_ref[...] = xl_ref[...] + p_ref[...]


def _grid_pos_embed(n_tokens, d):
    rows = n_tokens // GRID_W
    r, col = jnp.meshgrid(jnp.arange(rows, dtype=F32), jnp.arange(GRID_W, dtype=F32), indexing="ij")
    quarter = d // 4
    omega = 1.0 / (POS_BASE ** (jnp.arange(quarter, dtype=F32) / quarter))

    def axis_embed(pos):
        ang = pos.reshape(-1, 1) * omega[None, :]
        return jnp.concatenate([jnp.sin(ang), jnp.cos(ang)], axis=-1)

    return jnp.concatenate([axis_embed(r), axis_embed(col)], axis=-1)


def _embed(x_prompt, x_sample, tm=512):
    batch, seq, d = x_prompt.shape
    dec_batch, dec_seq, _ = x_sample.shape
    n_ctx, n_lat = batch * seq, dec_batch * dec_seq
    ctx_blocks, seq_blocks = n_ctx // tm, dec_seq // tm
    pos = _grid_pos_embed(dec_seq, d)
    return pl.pallas_call(
        functools.partial(_embed_kernel, ctx_blocks=ctx_blocks),
        out_shape=jax.ShapeDtypeStruct((n_ctx + n_lat, d), F32),
        grid=((n_ctx + n_lat) // tm,),
        in_specs=[pl.BlockSpec((tm, d), lambda i: (jnp.minimum(i, ctx_blocks - 1), 0)),
                  pl.BlockSpec((tm, d), lambda i: (jnp.maximum(i - ctx_blocks, 0), 0)),
                  pl.BlockSpec((tm, d), lambda i: (jnp.maximum(i - ctx_blocks, 0) % seq_blocks, 0))],
        out_specs=pl.BlockSpec((tm, d), lambda i: (i, 0)),
        compiler_params=_cparams(("parallel",)),
        name="embed",
    )(x_prompt.reshape(n_ctx, d), x_sample.reshape(n_lat, d), pos)


def _adanorm_kernel(x_ref, g_ref, sh_ref, sc_ref, o_ref):
    o_ref[...] = _ada_norm_tile(x_ref[...], g_ref[...], sh_ref[...], sc_ref[...]).astype(o_ref.dtype)


def _adanorm(x, g, mods, layer, k_shift, k_scale, tm=512):
    n, d = x.shape
    return pl.pallas_call(
        _adanorm_kernel,
        out_shape=jax.ShapeDtypeStruct((n, d), F32),
        grid=(n // tm,),
        in_specs=[pl.BlockSpec((tm, d), lambda i: (i, 0)),
                  pl.BlockSpec((1, d), lambda i: (0, 0)),
                  mods.spec(layer, k_shift, tm),
                  mods.spec(layer, k_scale, tm)],
        out_specs=pl.BlockSpec((tm, d), lambda i: (i, 0)),
        compiler_params=_cparams(("parallel",)),
        name="adanorm",
    )(x, g.reshape(1, d), mods.mods, mods.mods)


def _final_norm_kernel(x_ref, g_ref, o_ref):
    x = x_ref[...]
    r = lax.rsqrt(jnp.mean(x * x, axis=-1, keepdims=True) + RMS_EPS)
    o_ref[...] = x * r * g_ref[...]


def _final_norm(x, g, row0, nrows, tm=512):
    d = x.shape[1]
    off = row0 // tm
    return pl.pallas_call(
        _final_norm_kernel,
        out_shape=jax.ShapeDtypeStruct((nrows, d), F32),
        grid=(nrows // tm,),
        in_specs=[pl.BlockSpec((tm, d), lambda i: (i + off, 0)),
                  pl.BlockSpec((1, d), lambda i: (0, 0))],
        out_specs=pl.BlockSpec((tm, d), lambda i: (i, 0)),
        compiler_params=_cparams(("parallel",)),
        name="final_norm",
    )(x, g.reshape(1, d))


def _cmul(ar, ai, br, bi):
    return ar * br - ai * bi, ar * bi + ai * br


def _s5_param_kernel(lre_ref, lim_ref, ls_ref, bre_ref, bim_ref, cre_ref, cim_ref,
                     b2_ref, c2_ref, d2_ref, a2_ref):
    h, s = bre_ref.shape
    gl = LANES // h
    p = s // gl
    lr = lre_ref[...]
    li = lim_ref[...]
    dt = jnp.exp(ls_ref[...])
    mag = jnp.exp(lr * dt)
    ar = mag * jnp.cos(li * dt)
    ai = mag * jnp.sin(li * dt)
    den = lr * lr + li * li
    zr = ((ar - 1.0) * lr + ai * li) / den
    zi = (ai * lr - (ar - 1.0) * li) / den
    a2 = _cmul(ar, ai, ar, ai)
    bb = _cmul(zr, zi, bre_ref[...], bim_ref[...])
    abb = _cmul(ar, ai, *bb)
    cc = (cre_ref[...], cim_ref[...])
    ca = _cmul(*cc, ar, ai)
    ca2 = _cmul(*cc, *a2)

    same_b = (lax.broadcasted_iota(jnp.int32, (LANES, s), 0) // h
              == lax.broadcasted_iota(jnp.int32, (LANES, s), 1) // p)
    same_c = (lax.broadcasted_iota(jnp.int32, (s, LANES), 0) // p
              == lax.broadcasted_iota(jnp.int32, (s, LANES), 1) // h)
    spread = (lax.broadcasted_iota(jnp.int32, (h, LANES), 1) % h
              == lax.broadcasted_iota(jnp.int32, (h, LANES), 0)).astype(BF16)

    def bblock(re, im):
        def one(x):
            return jnp.where(same_b, jnp.concatenate([x] * gl, axis=0), 0.0)
        return jnp.concatenate([one(re), one(im)], axis=1)

    def cblock(re, im):
        def one(x):
            t = sum(lax.dot_general(part, spread, (((0,), (0,)), ((), ())), preferred_element_type=F32)
                    for part in _split_bf16(x))
            return jnp.where(same_c, t, 0.0)
        return jnp.concatenate([one(re), one(-im)], axis=0)

    b_blk = bblock(*bb)
    ab_blk = bblock(*abb)
    c_blk = cblock(*cc)
    cb = _dot_3pass(b_blk, c_blk)
    cab = _dot_3pass(ab_blk, c_blk)
    b2_ref[...] = jnp.concatenate([ab_blk, b_blk], axis=0).astype(BF16)
    c2_ref[...] = jnp.concatenate([cblock(*ca), cblock(*ca2)], axis=1).astype(BF16)
    d2_ref[...] = jnp.concatenate([jnp.concatenate([cb, cab], axis=1),
                                   jnp.concatenate([jnp.zeros_like(cb), cb], axis=1)], axis=0).astype(BF16)
    a2_ref[...] = jnp.concatenate(a2, axis=1)


def _s5_params(lam_re, lam_im, log_step, b_re, b_im, c_re, c_im):
    _, g, p = lam_re.shape
    h = b_re.shape[-1]
    gl = S5_LANE_GROUPS
    nlc = g // gl
    s = gl * p

    def lanes(v):
        return v.reshape(2, nlc, 1, s)

    def rows(m):
        return jnp.transpose(m.reshape(2, h, nlc, s), (0, 2, 1, 3))

    step = jnp.broadcast_to(log_step[:, :, None], (2, g, p))
    vec = pl.BlockSpec((None, None, 1, s), lambda d, j: (d, j, 0, 0))
    mat = pl.BlockSpec((None, None, h, s), lambda d, j: (d, j, 0, 0))

    def out(r, c, dt):
        return (jax.ShapeDtypeStruct((2, nlc, r, c), dt),
                pl.BlockSpec((None, None, r, c), lambda d, j: (d, j, 0, 0)))

    outs = [out(2 * LANES, 2 * s, BF16), out(2 * s, 2 * LANES, BF16), out(2 * LANES, 2 * LANES, BF16),
            out(1, 2 * s, F32)]
    return pl.pallas_call(
        _s5_param_kernel,
        out_shape=tuple(o[0] for o in outs),
        grid=(2, nlc),
        in_specs=[vec, vec, vec, mat, mat, mat, mat],
        out_specs=tuple(o[1] for o in outs),
        compiler_params=_cparams(("parallel", "parallel")),
        name="s5_params",
    )(lanes(lam_re), lanes(lam_im), lanes(step),
      rows(jnp.transpose(b_re, (0, 3, 1, 2))), rows(jnp.transpose(b_im, (0, 3, 1, 2))),
      rows(jnp.transpose(c_re, (0, 2, 1, 3))), rows(jnp.transpose(c_im, (0, 2, 1, 3))))


def _s5_state_to_lanes(s_re, s_im):
    b, _, g, p = s_re.shape
    gl = S5_LANE_GROUPS
    nlc = g // gl

    def lay(s):
        return jnp.transpose(s.reshape(b, 2, nlc, gl * p), (1, 2, 0, 3))

    return jnp.concatenate([lay(s_re), lay(s_im)], axis=-1)


def _s5_state_from_lanes(s, g, p):
    _, nlc, b, s2 = s.shape
    half = s2 // 2

    def unlay(t):
        return jnp.transpose(t, (2, 0, 1, 3)).reshape(b, 2, g, p)

    return unlay(s[..., :half]), unlay(s[..., half:])


def _s5_scan_kernel(*refs, seq, tc, aliased):
    if aliased:
        h_ref, b2_ref, c2_ref, d2_ref, a_ref, d_ref, s0_ref, _, z_ref, sf_ref = refs[:10]
    else:
        h_ref, b2_ref, c2_ref, d2_ref, a_ref, d_ref, s0_ref, z_ref, sf_ref = refs[:9]
    scr = (refs[-10:-5], refs[-5:])
    nb = SUBLANES
    half = a_ref.shape[-1] // 2
    nc = seq // tc
    npair = tc // 2
    a_re = [jnp.broadcast_to(a_ref[dr][:, :half], (nb, half)) for dr in range(2)]
    a_im = [jnp.broadcast_to(a_ref[dr][:, half:], (nb, half)) for dr in range(2)]

    def chunk(c, carry):
        t0s = (c * tc, (nc - 1 - c) * tc)

        def pair_rows(dr, p):
            if dr == 0:
                first = t0s[0] + 2 * p
                return pl.ds(first, nb, stride=seq), pl.ds(first + 1, nb, stride=seq)
            first = t0s[1] + tc - 1 - 2 * p
            return pl.ds(first, nb, stride=seq), pl.ds(first - 1, nb, stride=seq)

        carry = list(carry)
        for dr in range(2):
            u_scr, bu_scr, _, _, _ = scr[dr]
            for p in range(npair):
                r1, r2 = pair_rows(dr, p)
                u_scr[p * nb:(p + 1) * nb, :LANES] = h_ref[r1, :]
                u_scr[p * nb:(p + 1) * nb, LANES:] = h_ref[r2, :]
            bu_scr[...] = jnp.dot(u_scr[...].astype(BF16), b2_ref[dr], preferred_element_type=F32)
        for dr in range(2):
            _, bu_scr, x_scr, _, _ = scr[dr]
            xr, xi = carry[2 * dr], carry[2 * dr + 1]
            for p in range(npair):
                x_scr[p * nb:(p + 1) * nb, :half] = xr
                x_scr[p * nb:(p + 1) * nb, half:] = xi
                bu = bu_scr[p * nb:(p + 1) * nb, :]
                xr, xi = (a_re[dr] * xr - a_im[dr] * xi + bu[:, :half],
                          a_re[dr] * xi + a_im[dr] * xr + bu[:, half:])
            carry[2 * dr], carry[2 * dr + 1] = xr, xi
        for dr in range(2):
            u_scr, _, x_scr, yo_scr, y_scr = scr[dr]
            yo_scr[...] = jnp.dot(x_scr[...].astype(BF16), c2_ref[dr], preferred_element_type=F32) \
                + jnp.dot(u_scr[...].astype(BF16), d2_ref[dr], preferred_element_type=F32)
            for p in range(npair):
                r1, r2 = pair_rows(dr, p)
                y_scr[r1, :] = yo_scr[p * nb:(p + 1) * nb, :LANES]
                y_scr[r2, :] = yo_scr[p * nb:(p + 1) * nb, LANES:]
        return tuple(carry)

    s0f = s0_ref[0]
    s0b = s0_ref[1]
    init = (s0f[:, :half], s0f[:, half:], s0b[:, :half], s0b[:, half:])
    xr_f, xi_f, xr_b, xi_b = lax.fori_loop(0, nc, chunk, init)
    sf_ref[0, :, :half] = xr_f
    sf_ref[0, :, half:] = xi_f
    sf_ref[1, :, :half] = xr_b
    sf_ref[1, :, half:] = xi_b
    y = h_ref[...] * d_ref[...] + scr[0][4][...] + scr[1][4][...]
    z_ref[...] = _gelu_tanh(y).astype(z_ref.dtype)


def _s5_scan(h, operands, dvec, s0, row0, nseq, seq, z_prev=None):
    b2, c2, d2, avec = operands
    n, d = h.shape
    nlc = d // LANES
    s2 = avec.shape[-1]
    nb = SUBLANES
    rows = nb * seq
    off = row0 // rows
    aliased = z_prev is not None
    in_specs = [
        pl.BlockSpec((rows, LANES), lambda i, j: (i + off, j)),
        pl.BlockSpec((2, None, 2 * LANES, s2), lambda i, j: (0, j, 0, 0)),
        pl.BlockSpec((2, None, s2, 2 * LANES), lambda i, j: (0, j, 0, 0)),
        pl.BlockSpec((2, None, 2 * LANES, 2 * LANES), lambda i, j: (0, j, 0, 0)),
        pl.BlockSpec((2, None, 1, s2), lambda i, j: (0, j, 0, 0)),
        pl.BlockSpec((1, LANES), lambda i, j: (0, j)),
        pl.BlockSpec((2, None, nb, s2), lambda i, j: (0, j, i, 0)),
    ]
    args = [h, b2, c2, d2, avec, dvec, s0]
    io_alias = {}
    if aliased:
        in_specs.append(pl.BlockSpec(memory_space=pl.ANY))
        args.append(z_prev)
        io_alias = {len(args) - 1: 0}
    npair_rows = S5_TC // 2 * nb
    return pl.pallas_call(
        functools.partial(_s5_scan_kernel, seq=seq, tc=S5_TC, aliased=aliased),
        out_shape=(jax.ShapeDtypeStruct((n, d), BF16), jax.ShapeDtypeStruct((2, nlc, nseq, s2), F32)),
        grid=(nseq // nb, nlc),
        in_specs=in_specs,
        out_specs=(pl.BlockSpec((rows, LANES), lambda i, j: (i + off, j)),
                   pl.BlockSpec((2, None, nb, s2), lambda i, j: (0, j, i, 0))),
        scratch_shapes=2 * [pltpu.VMEM((npair_rows, 2 * LANES), F32),
                            pltpu.VMEM((npair_rows, s2), F32),
                            pltpu.VMEM((npair_rows, s2), F32),
                            pltpu.VMEM((npair_rows, 2 * LANES), F32),
                            pltpu.VMEM((rows, LANES), F32)],
        input_output_aliases=io_alias,
        compiler_params=_cparams(("parallel", "parallel")),
        name="s5_scan",
    )(*args)


def _mm_res_kernel(*refs, n_w, has_bias):
    z_ref = refs[0]
    w_refs = refs[1:1 + n_w]
    pos = 1 + n_w
    b_refs = refs[pos:pos + n_w] if has_bias else ()
    pos += n_w if has_bias else 0
    x_ref, gate_ref, o_ref = refs[pos:pos + 3]
    z = z_ref[...]
    acc = [jnp.dot(z, w[...], preferred_element_type=F32) for w in w_refs]
    if has_bias:
        acc = [a + b[...] for a, b in zip(acc, b_refs)]
    y = acc[0] if n_w == 1 else acc[0] * jax.nn.sigmoid(acc[1])
    o_ref[...] = x_ref[...] + gate_ref[...] * y


def _mm_residual(z, ws, bs, x, mods, layer, k_gate, tm=1024, tn=1024):
    n, k = z.shape
    n_out = ws[0].shape[1]
    n_w = len(ws)
    has_bias = bs is not None
    in_specs = [pl.BlockSpec((tm, k), lambda i, j: (i, 0))]
    in_specs += [pl.BlockSpec((k, tn), lambda i, j: (0, j)) for _ in ws]
    args = [z, *ws]
    if has_bias:
        in_specs += [pl.BlockSpec((1, tn), lambda i, j: (0, j)) for _ in bs]
        args += [b.reshape(1, n_out) for b in bs]
    gate_spec = mods.spec(layer, k_gate, tm)
    gate_spec = pl.BlockSpec((None, 1, tn), lambda i, j, f=gate_spec.index_map: (f(i)[0], 0, j))
    in_specs += [pl.BlockSpec((tm, tn), lambda i, j: (i, j)), gate_spec]
    args += [x, mods.mods]
    return pl.pallas_call(
        functools.partial(_mm_res_kernel, n_w=n_w, has_bias=has_bias),
        out_shape=jax.ShapeDtypeStruct((n, n_out), F32),
        grid=(n // tm, n_out // tn),
        in_specs=in_specs,
        out_specs=pl.BlockSpec((tm, tn), lambda i, j: (i, j)),
        compiler_params=_cparams(("parallel", "parallel")),
        name=f"proj_residual_{n_w}w",
    )(*args)


def _split_bf16(a):
    hi = a.astype(BF16)
    return hi, (a - hi.astype(F32)).astype(BF16)


def _dot_3pass(a, b):
    a_hi, a_lo = _split_bf16(a)
    b_hi, b_lo = _split_bf16(b)
    return (jnp.dot(a_hi, b_hi, preferred_element_type=F32) + jnp.dot(a_hi, b_lo, preferred_element_type=F32)
            + jnp.dot(a_lo, b_hi, preferred_element_type=F32))


def _mm_norm_kernel(x_ref, g_ref, sh_ref, sc_ref, w_ref, ws_ref, o_ref, os_ref, h_scr):
    @pl.when(pl.program_id(1) == 0)
    def _():
        h = _ada_norm_tile(x_ref[...], g_ref[...], sh_ref[...], sc_ref[...])
        h_scr[...] = h.astype(h_scr.dtype)
        os_ref[...] = _dot_3pass(h, ws_ref[...])

    o_ref[...] = jnp.dot(h_scr[...], w_ref[...], preferred_element_type=F32).astype(o_ref.dtype)


def _mm_adanorm(x, g, mods, layer, k_shift, k_scale, w, w_side, tm=1024, tn=1024):
    n, d = x.shape
    n_out = w.shape[1]
    n_side = w_side.shape[1]
    return pl.pallas_call(
        _mm_norm_kernel,
        out_shape=(jax.ShapeDtypeStruct((n, n_out), BF16), jax.ShapeDtypeStruct((n, n_side), F32)),
        grid=(n // tm, n_out // tn),
        in_specs=[pl.BlockSpec((tm, d), lambda i, j: (i, 0)),
                  pl.BlockSpec((1, d), lambda i, j: (0, 0)),
                  mods.spec(layer, k_shift, tm),
                  mods.spec(layer, k_scale, tm),
                  pl.BlockSpec((d, tn), lambda i, j: (0, j)),
                  pl.BlockSpec((d, n_side), lambda i, j: (0, 0))],
        out_specs=(pl.BlockSpec((tm, tn), lambda i, j: (i, j)),
                   pl.BlockSpec((tm, n_side), lambda i, j: (i, 0))),
        scratch_shapes=[pltpu.VMEM((tm, d), BF16)],
        compiler_params=_cparams(("parallel", "arbitrary")),
        name="adanorm_proj",
    )(x, g.reshape(1, d), mods.mods, mods.mods, w, w_side)


_G_FWD = ML_HEADS
_G_BWD = 3 * ML_HEADS


def _gate_kernel(xc_ref, xr_ref, bc_ref, br_ref, g_ref, w_ref, e_ref, dr_ref, wr_ref):
    t = xc_ref.shape[0]
    r_i = lax.broadcasted_iota(jnp.int32, (t, t), 0)
    c_i = lax.broadcasted_iota(jnp.int32, (t, t), 1)
    lower = (c_i <= r_i).astype(F32)
    upper = (c_i >= r_i).astype(F32)
    hi = lax.Precision.HIGHEST
    xc = xc_ref[...] + bc_ref[...]
    fc = jax.nn.log_sigmoid(xc)
    lane = lax.broadcasted_iota(jnp.int32, xc.shape, 1)
    g_c = jnp.where(lane < 2 * ML_HEADS,
                    jnp.dot(lower, fc, preferred_element_type=F32, precision=hi),
                    jnp.dot(upper, fc, preferred_element_type=F32, precision=hi))
    e_c = jnp.broadcast_to(jnp.sum(fc, axis=0, keepdims=True), xc.shape)
    g_ref[...] = g_c
    e_ref[...] = e_c
    w_ref[...] = e_c - g_c + pltpu.roll(xc, ML_HEADS, 1)
    xr = xr_ref[...] + br_ref[...]
    fr = jax.nn.log_sigmoid(xr)
    row = lax.broadcasted_iota(jnp.int32, xr.shape, 0)
    g_r = jnp.where(row < 2 * ML_HEADS,
                    jnp.dot(fr, upper, preferred_element_type=F32, precision=hi),
                    jnp.dot(fr, lower, preferred_element_type=F32, precision=hi))
    i_r = pltpu.roll(xr, ML_HEADS, 0)
    dr_ref[...] = i_r - g_r
    wr_ref[...] = jnp.sum(fr, axis=1, keepdims=True) - g_r + i_r


def _gate_prep(gates, b_gates, tch):
    n = gates.shape[0]
    ng = 4 * ML_HEADS
    bias_c = jnp.pad(b_gates, (0, LANES - ng)).reshape(1, LANES)
    bias_r = b_gates.reshape(ng, 1)
    gates_r = jnp.transpose(gates[:, :ng])
    col = pl.BlockSpec((tch, LANES), lambda i: (i, 0))
    rowb = pl.BlockSpec((ng, tch), lambda i: (0, i))
    return pl.pallas_call(
        _gate_kernel,
        out_shape=(jax.ShapeDtypeStruct((n, LANES), F32),) * 3 + (jax.ShapeDtypeStruct((ng, n), F32),) * 2,
        grid=(n // tch,),
        in_specs=[col, rowb, pl.BlockSpec((1, LANES), lambda i: (0, 0)),
                  pl.BlockSpec((ng, 1), lambda i: (0, 0))],
        out_specs=(col, col, col, rowb, rowb),
        compiler_params=_cparams(("parallel",)),
        name="mlstm_gates",
    )(gates, gates_r, bias_c, bias_r)


def _mlstm_kernel(q_ref, kt_ref, v_ref, o_ref, gcol_ref, grow_ref, hg_ref,
                  c0_ref, n0_ref, m0_ref, hn_ref, cf_ref, nf_ref, mf_ref, *scratch, seq, tch, hp):
    nc = seq // tch
    dv = v_ref.shape[-1] // hp
    dk = q_ref.shape[-1] // hp
    tt = lax.broadcasted_iota(jnp.int32, (tch, tch), 0)
    ss = lax.broadcasted_iota(jnp.int32, (tch, tch), 1)
    ones_col = (lax.broadcasted_iota(jnp.int32, (tch, LANES), 1) == 0).astype(BF16)
    scr = {(hd, dr): scratch[3 * (2 * hd + dr):3 * (2 * hd + dr) + 3] for hd in range(hp) for dr in range(2)}
    for (hd, dr), (_, c_scr, vx) in scr.items():
        c_scr[:, :dv] = c0_ref[dr, hd]
        c_scr[:, dv:] = n0_ref[hd, dr]
        vx[:, dv:] = ones_col

    def chunk_dir(hd, dr, r0, m):
        hacc, c_scr, vx = scr[hd, dr]
        rows = pl.ds(pl.multiple_of(r0, tch), tch)
        q = q_ref[rows, hd * dk:(hd + 1) * dk]
        kt = kt_ref[hd * dk:(hd + 1) * dk, rows]
        vx[:, :dv] = v_ref[rows, hd * dv:(hd + 1) * dv]
        gc = gcol_ref[hd, rows, :]
        gr = grow_ref[hd, :, rows]
        g_col = gc[:, 3 * dr:3 * dr + 1]
        w_col = gc[:, 3 * dr + 1:3 * dr + 2]
        e_col = gc[:, 3 * dr + 2:3 * dr + 3]
        d_row = gr[2 * dr:2 * dr + 1, :]
        w_row = gr[2 * dr + 1:2 * dr + 2, :]
        mask = (ss <= tt) if dr == 0 else (ss >= tt)
        a_col = g_col + m
        dmat = jnp.where(mask, g_col + d_row, -jnp.inf)
        mt = jnp.maximum(a_col, jnp.max(dmat, axis=1, keepdims=True))
        qk = jnp.dot(q, kt, preferred_element_type=F32)
        s = (qk * jnp.exp(dmat - mt)).astype(BF16)
        inter = jnp.exp(a_col - mt)
        vext = vx[...]
        tot = jnp.dot(s, vext, preferred_element_type=F32) \
            + inter * jnp.dot(q, c_scr[...].astype(BF16), preferred_element_type=F32)
        den = tot[:, dv:dv + 1]
        hacc[rows, :] = tot[:, :dv] / jnp.maximum(jnp.abs(den), jnp.exp(-mt))
        g_end = jnp.max(e_col, axis=0, keepdims=True)
        m_new = jnp.maximum(g_end + m, jnp.max(w_col, axis=0, keepdims=True))
        decay = jnp.exp(g_end + m - m_new)
        kw = (kt.astype(F32) * jnp.exp(w_row - m_new)).astype(BF16)
        c_scr[...] = decay * c_scr[...] + jnp.dot(kw, vext, preferred_element_type=F32)
        return m_new

    keys = list(scr)

    def body(c, carry):
        return tuple(chunk_dir(hd, dr, (c if dr == 0 else nc - 1 - c) * tch, m)
                     for (hd, dr), m in zip(keys, carry))

    m_fin = lax.fori_loop(0, nc, body, tuple(m0_ref[hd, dr:dr + 1, 0:1] for hd, dr in keys))
    for (hd, dr), m in zip(keys, m_fin):
        c_scr = scr[hd, dr][1]
        cf_ref[dr, hd] = c_scr[:, :dv]
        nf_ref[hd, dr] = c_scr[:, dv:]
        mf_ref[hd, dr:dr + 1, :] = jnp.broadcast_to(m, (1, LANES))
    for hd in range(hp):
        cols = slice(hd * dv, (hd + 1) * dv)
        hs = scr[hd, 0][0][...] + scr[hd, 1][0][...]
        hn = hs * lax.rsqrt(jnp.mean(hs * hs, axis=-1, keepdims=True) + RMS_EPS)
        hn = hn * hg_ref[:, cols] * jax.nn.sigmoid(o_ref[:, cols].astype(F32))
        hn_ref[:, cols] = hn.astype(hn_ref.dtype)


def _mlstm(proj, k_t, gcol, grow, head_g, c0, n0, m0, row0, nseq, seq, hn_prev=None, c_slot=None, hp=ML_HP):
    n = proj.shape[0]
    h = ML_HEADS
    dk = c0.shape[-2]
    dv = c0.shape[-1]
    d = h * dv
    qk = h * dk
    off = row0 // seq
    tch = min(ML_T, seq)
    aliased = hn_prev is not None
    c_spec = pl.BlockSpec((None, 2, hp, dk, dv), lambda b, j: (b, 0, j, 0, 0))
    n_spec = pl.BlockSpec((None, hp, 2, dk, LANES), lambda b, j: (b, j, 0, 0, 0))
    m_spec = pl.BlockSpec((None, hp, 2, LANES), lambda b, j: (b, j, 0, 0))
    wk, wv = hp * dk, hp * dv
    in_specs = [
        pl.BlockSpec((seq, wk), lambda b, j: (b + off, j)),
        pl.BlockSpec((wk, seq), lambda b, j: (j, b + off)),
        pl.BlockSpec((seq, wv), lambda b, j: (b + off, 2 * qk // wv + j)),
        pl.BlockSpec((seq, wv), lambda b, j: (b + off, (2 * qk + d) // wv + j)),
        pl.BlockSpec((None, hp, seq, 8), lambda b, j: (b, j, 0, 0)),
        pl.BlockSpec((None, hp, 4, seq), lambda b, j: (b, j, 0, 0)),
        pl.BlockSpec((1, wv), lambda b, j: (0, j)),
        c_spec, n_spec, m_spec,
    ]
    args = [proj, k_t, proj, proj, gcol, grow, head_g.reshape(1, d), c0, n0, m0]
    n_in = len(args)
    io_alias = {}
    if aliased:
        in_specs.append(pl.BlockSpec(memory_space=pl.ANY))
        args.append(hn_prev)
        io_alias[len(args) - 1] = 0
    c_type = jax.ShapeDtypeStruct((nseq, 2, h, dk, dv), F32)
    c_out_spec = c_spec
    if c_slot is not None:
        slot, n_slots, c_prev = c_slot
        c_type = jax.ShapeDtypeStruct((nseq, n_slots, 2, h, dk, dv), F32)
        c_out_spec = pl.BlockSpec((None, None, 2, hp, dk, dv), lambda b, j: (b, slot, 0, j, 0, 0))
        if c_prev is not None:
            in_specs.append(pl.BlockSpec(memory_space=pl.ANY))
            args.append(c_prev)
            io_alias[len(args) - 1] = 1

    def kern(*refs):
        _mlstm_kernel(*refs[:n_in], *refs[len(args):], seq=seq, tch=tch, hp=hp)

    return pl.pallas_call(
        kern,
        out_shape=(jax.ShapeDtypeStruct((n, d), BF16), c_type,
                   jax.ShapeDtypeStruct((nseq, h, 2, dk, LANES), F32),
                   jax.ShapeDtypeStruct((nseq, h, 2, LANES), F32)),
        grid=(nseq, h // hp),
        in_specs=in_specs,
        out_specs=(pl.BlockSpec((seq, wv), lambda b, j: (b + off, j)), c_out_spec, n_spec, m_spec),
        scratch_shapes=2 * hp * [pltpu.VMEM((seq, dv), F32), pltpu.VMEM((dk, dv + LANES), F32),
                                 pltpu.VMEM((tch, dv + LANES), BF16)],
        input_output_aliases=io_alias,
        compiler_params=_cparams(("parallel", "parallel")),
        name="mlstm",
    )(*args)


_R_EID, _R_W, _R_RANK = 0, 2, 4
_R_LOGIT0 = MOE_GROUPS


def _router_kernel(x_ref, g_ref, sh_ref, sc_ref, whi_ref, wlo_ref, b_ref, h_ref, route_ref, cnt_ref, carry):
    i = pl.program_id(0)

    @pl.when(i == 0)
    def _():
        carry[...] = jnp.zeros_like(carry)

    h = _ada_norm_tile(x_ref[...], g_ref[...], sh_ref[...], sc_ref[...])
    h_ref[...] = h
    tm = h.shape[0]
    h_hi = h.astype(BF16)
    h_lo = (h - h_hi.astype(F32)).astype(BF16)
    logits = (jnp.dot(h_hi, whi_ref[...], preferred_element_type=F32)
              + jnp.dot(h_hi, wlo_ref[...], preferred_element_type=F32)
              + jnp.dot(h_lo, whi_ref[...], preferred_element_type=F32)) + b_ref[...]
    lane = lax.broadcasted_iota(jnp.int32, logits.shape, 1)
    big = jnp.int32(ROUTE_LANES)

    def first_lane(cond):
        return jnp.min(jnp.where(cond, lane, big), axis=1, keepdims=True)

    glog = jnp.where(lane < MOE_GROUPS, logits, -jnp.inf)
    ge = jnp.exp(glog - jnp.max(glog, axis=1, keepdims=True))
    pgrp = ge / jnp.sum(ge, axis=1, keepdims=True)
    pg = jnp.max(pgrp, axis=1, keepdims=True)
    grp = first_lane(pgrp == pg)
    e_lane = lane - _R_LOGIT0
    in_grp = (e_lane >= 0) & (e_lane < MOE_EXPERTS) & ((e_lane // MOE_EPG) == grp)
    elog = jnp.where(in_grp, logits, -jnp.inf)
    ee = jnp.exp(elog - jnp.max(elog, axis=1, keepdims=True))
    pe = jnp.where(in_grp, ee / jnp.sum(ee, axis=1, keepdims=True), -1.0)
    p0 = jnp.max(pe, axis=1, keepdims=True)
    l0 = first_lane(pe == p0)
    pe1 = jnp.where(lane == l0, -1.0, pe)
    p1 = jnp.max(pe1, axis=1, keepdims=True)
    l1 = first_lane(pe1 == p1)
    psum = p0 + p1
    w0 = pg * p0 / psum
    w1 = pg * p1 / psum
    onehot = ((lane == l0) | (lane == l1)).astype(BF16)
    r_i = lax.broadcasted_iota(jnp.int32, (tm, tm), 0)
    c_i = lax.broadcasted_iota(jnp.int32, (tm, tm), 1)
    tri = (c_i < r_i).astype(BF16)
    before = jnp.dot(tri, onehot, preferred_element_type=F32) + carry[...]
    rank0 = jnp.sum(jnp.where(lane == l0, before, 0.0), axis=1, keepdims=True)
    rank1 = jnp.sum(jnp.where(lane == l1, before, 0.0), axis=1, keepdims=True)
    carry[...] = carry[...] + jnp.sum(onehot.astype(F32), axis=0, keepdims=True)
    cnt_ref[...] = carry[...]
    cols = [(l0 - _R_LOGIT0).astype(F32), (l1 - _R_LOGIT0).astype(F32), w0, w1, rank0, rank1]
    route = jnp.zeros(logits.shape, F32)
    for c, val in enumerate(cols):
        route = jnp.where(lane == c, val, route)
    route_ref[...] = route


def _router(x, g, mods, layer, w_route, b_route, tm=512):
    n, d = x.shape
    w_hi = w_route.astype(BF16)
    w_lo = (w_route - w_hi.astype(F32)).astype(BF16)
    return pl.pallas_call(
        _router_kernel,
        out_shape=(jax.ShapeDtypeStruct((n, d), F32),
                   jax.ShapeDtypeStruct((n, ROUTE_LANES), F32),
                   jax.ShapeDtypeStruct((1, ROUTE_LANES), F32)),
        grid=(n // tm,),
        in_specs=[pl.BlockSpec((tm, d), lambda i: (i, 0)),
                  pl.BlockSpec((1, d), lambda i: (0, 0)),
                  mods.spec(layer, 3, tm),
                  mods.spec(layer, 4, tm),
                  pl.BlockSpec((d, ROUTE_LANES), lambda i: (0, 0)),
                  pl.BlockSpec((d, ROUTE_LANES), lambda i: (0, 0)),
                  pl.BlockSpec((1, ROUTE_LANES), lambda i: (0, 0))],
        out_specs=(pl.BlockSpec((tm, d), lambda i: (i, 0)),
                   pl.BlockSpec((tm, ROUTE_LANES), lambda i: (i, 0)),
                   pl.BlockSpec((1, ROUTE_LANES), lambda i: (0, 0))),
        scratch_shapes=[pltpu.VMEM((1, ROUTE_LANES), F32)],
        compiler_params=_cparams(("arbitrary",)),
        name="moe_router",
    )(x, g.reshape(1, d), mods.mods, mods.mods, w_hi, w_lo, b_route)


def _dispatch_kernel(pcnt_ref, pend_ref, dest_ref, h_ref, xs_ref, zbuf, sem, zsem):
    tm = h_ref.shape[0]
    zrows = zbuf.shape[0]

    @pl.when(pl.program_id(0) == 0)
    def _():
        zbuf[...] = jnp.zeros_like(zbuf)

        def clear(e, _):
            @pl.when(pcnt_ref[e] > 0)
            def _():
                start = pl.multiple_of(pend_ref[e] - zrows, zrows)
                cp = pltpu.make_async_copy(zbuf, xs_ref.at[pl.ds(start, zrows)], zsem)
                cp.start()
                cp.wait()
            return 0

        lax.fori_loop(0, MOE_EXPERTS, clear, 0)

    def issue(r, _):
        for k in range(2):
            d = dest_ref[0, 0, 2 * r + k]
            pltpu.make_async_copy(h_ref.at[pl.ds(r, 1)], xs_ref.at[pl.ds(d, 1)], sem).start(priority=k)
        return 0

    lax.fori_loop(0, tm, issue, 0, unroll=ROW_ISSUE_UNROLL)
    for _ in range(2):
        pltpu.make_async_copy(h_ref, xs_ref.at[pl.ds(0, tm)], sem).wait()


def _dispatch(h, dest, pcnt, pends, n_pad, tm=DISPATCH_TM):
    n, d = h.shape
    nblk = n // tm
    grid_spec = pltpu.PrefetchScalarGridSpec(
        num_scalar_prefetch=2,
        grid=(nblk,),
        in_specs=[pl.BlockSpec((1, 1, 2 * tm), lambda i, *_: (i, 0, 0), memory_space=pltpu.SMEM),
                  pl.BlockSpec((tm, d), lambda i, *_: (i, 0))],
        out_specs=pl.BlockSpec(memory_space=pl.ANY),
        scratch_shapes=[pltpu.VMEM((MOE_TM, d), F32), pltpu.SemaphoreType.DMA(()),
                        pltpu.SemaphoreType.DMA(())],
    )
    return pl.pallas_call(
        _dispatch_kernel,
        out_shape=jax.ShapeDtypeStruct((n_pad, d), F32),
        grid_spec=grid_spec,
        compiler_params=_cparams(("arbitrary",)),
        name="moe_dispatch",
    )(pcnt, pends, dest.reshape(nblk, 1, 2 * tm), h)


def _experts_kernel(be_ref, first_ref, nxt_ref, nu_ref, x_ref, wg_hbm, wu_hbm, wd_hbm, o_ref,
                    stg_g, stg_u, stg_d, wg_bf, wu_bf, wd_bf, sem, *, layer, cast_rows):
    i = pl.program_id(0)
    active = i < nu_ref[0]
    stages = ((wg_hbm, stg_g, wg_bf), (wu_hbm, stg_u, wu_bf), (wd_hbm, stg_d, wd_bf))

    def weight_copies(e):
        return [pltpu.make_async_copy(hbm.at[layer, e], stg, sem.at[k])
                for k, (hbm, stg, _) in enumerate(stages)]

    @pl.when(i == 0)
    def _():
        for cp in weight_copies(be_ref[0]):
            cp.start()

    @pl.when(active & (first_ref[i] == 1))
    def _():
        for cp in weight_copies(be_ref[i]):
            cp.wait()
        for _, stg, wbf in stages:
            def cast(r, _, stg=stg, wbf=wbf):
                rows = pl.ds(pl.multiple_of(r * cast_rows, cast_rows), cast_rows)
                wbf[rows, :] = stg[rows, :].astype(BF16)
                return 0

            lax.fori_loop(0, stg.shape[0] // cast_rows, cast, 0)

        @pl.when(nxt_ref[i] >= 0)
        def _():
            for cp in weight_copies(nxt_ref[i]):
                cp.start()

    @pl.when(active)
    def _():
        x = x_ref[...].astype(BF16)
        g = jnp.dot(x, wg_bf[...], preferred_element_type=F32)
        u = jnp.dot(x, wu_bf[...], preferred_element_type=F32)
        a = (_silu(g) * u).astype(BF16)
        o_ref[...] = jnp.dot(a, wd_bf[...], preferred_element_type=F32)

    @pl.when(jnp.logical_not(active))
    def _():
        o_ref[...] = jnp.zeros_like(o_ref)


def _experts(xs, block_expert, block_first, block_next, n_used, w_gate, w_up, w_down, layer, tm=MOE_TM):
    n_pad, d = xs.shape
    f = w_gate.shape[-1]
    nblk = n_pad // tm
    grid_spec = pltpu.PrefetchScalarGridSpec(
        num_scalar_prefetch=4,
        grid=(nblk,),
        in_specs=[pl.BlockSpec((tm, d), lambda i, be, fi, nx, nu: (jnp.minimum(i, nu[0] - 1), 0)),
                  pl.BlockSpec(memory_space=pl.ANY),
                  pl.BlockSpec(memory_space=pl.ANY),
                  pl.BlockSpec(memory_space=pl.ANY)],
        out_specs=pl.BlockSpec((tm, d), lambda i, *_: (i, 0)),
        scratch_shapes=[pltpu.VMEM((d, f), F32), pltpu.VMEM((d, f), F32), pltpu.VMEM((f, d), F32),
                        pltpu.VMEM((d, f), BF16), pltpu.VMEM((d, f), BF16), pltpu.VMEM((f, d), BF16),
                        pltpu.SemaphoreType.DMA((3,))],
    )
    return pl.pallas_call(
        functools.partial(_experts_kernel, layer=layer, cast_rows=256),
        out_shape=jax.ShapeDtypeStruct((n_pad, d), F32),
        grid_spec=grid_spec,
        compiler_params=_cparams(("arbitrary",), vmem=EXPERT_VMEM_LIMIT_BYTES),
        name="moe_experts",
    )(block_expert, block_first, block_next, n_used, xs, w_gate, w_up, w_down)


def _combine_kernel(dest_ref, dnext_ref, y_hbm, x_ref, route_ref, gate_ref, o_ref, buf, sem):
    i = pl.program_id(0)
    nblk = pl.num_programs(0)
    tm = x_ref.shape[0]

    def issue_block(idx_ref, slot):
        def issue(r, _):
            for k in range(2):
                d = idx_ref[0, 0, 2 * r + k]
                pltpu.make_async_copy(y_hbm.at[pl.ds(d, 1)], buf.at[slot, k, pl.ds(r, 1)],
                                      sem.at[slot]).start(priority=k)
            return 0

        lax.fori_loop(0, tm, issue, 0, unroll=ROW_ISSUE_UNROLL)

    @pl.when(i == 0)
    def _():
        issue_block(dest_ref, 0)

    @pl.when(i + 1 < nblk)
    def _():
        issue_block(dnext_ref, (i + 1) % 2)

    slot = i % 2
    for k in range(2):
        pltpu.make_async_copy(y_hbm.at[pl.ds(0, tm)], buf.at[slot, k], sem.at[slot]).wait()
    route = route_ref[...]
    w0 = route[:, _R_W:_R_W + 1]
    w1 = route[:, _R_W + 1:_R_W + 2]
    o_ref[...] = x_ref[...] + gate_ref[...] * (buf[slot, 0] * w0 + buf[slot, 1] * w1)


def _combine(x, y, dest, route, mods, layer, tm=ROW_TM):
    n, d = x.shape
    nblk = n // tm
    dest3 = dest.reshape(nblk, 1, 2 * tm)
    return pl.pallas_call(
        _combine_kernel,
        out_shape=jax.ShapeDtypeStruct((n, d), F32),
        grid=(nblk,),
        in_specs=[pl.BlockSpec((1, 1, 2 * tm), lambda i: (i, 0, 0), memory_space=pltpu.SMEM),
                  pl.BlockSpec((1, 1, 2 * tm), lambda i: (jnp.minimum(i + 1, nblk - 1), 0, 0),
                               memory_space=pltpu.SMEM),
                  pl.BlockSpec(memory_space=pl.ANY),
                  pl.BlockSpec((tm, d), lambda i: (i, 0)),
                  pl.BlockSpec((tm, ROUTE_LANES), lambda i: (i, 0)),
                  mods.spec(layer, 5, tm)],
        out_specs=pl.BlockSpec((tm, d), lambda i: (i, 0)),
        scratch_shapes=[pltpu.VMEM((2, 2, tm, d), F32), pltpu.SemaphoreType.DMA((2,))],
        compiler_params=_cparams(("arbitrary",)),
        name="moe_combine",
    )(dest3, dest3, y, x, route, mods.mods)


def _moe_layer(x, mods, layer, norm_g, w_group, b_group, w_expert, b_expert, w_gate, w_up, w_down):
    n, d = x.shape
    ne = MOE_EXPERTS
    pad = ROUTE_LANES - MOE_GROUPS - ne
    w_route = jnp.concatenate([w_group, w_expert, jnp.zeros((d, pad), F32)], axis=1)
    b_route = jnp.concatenate([b_group, b_expert, jnp.zeros((pad,), F32)]).reshape(1, ROUTE_LANES)
    h, route, counts = _router(x, norm_g, mods, layer, w_route, b_route)
    cnt = counts[0, _R_LOGIT0:_R_LOGIT0 + ne].astype(jnp.int32)
    pcnt = (cnt + MOE_TM - 1) // MOE_TM * MOE_TM
    pends = jnp.cumsum(pcnt)
    pstarts = pends - pcnt
    experts = jnp.arange(ne, dtype=jnp.int32)
    eid = route[:, _R_EID:_R_EID + 2].astype(jnp.int32)
    rank = route[:, _R_RANK:_R_RANK + 2].astype(jnp.int32)
    dest = (jnp.sum(jnp.where(eid[..., None] == experts, pstarts, 0), axis=-1) + rank).reshape(-1)
    n_blocks = (n * 2) // MOE_TM + ne
    n_pad = n_blocks * MOE_TM
    block_row = jnp.arange(n_blocks, dtype=jnp.int32) * MOE_TM
    block_expert = jnp.minimum(jnp.sum((pends[None, :] <= block_row[:, None]).astype(jnp.int32), axis=1), ne - 1)
    block_first = jnp.concatenate([jnp.ones((1,), jnp.int32),
                                   (block_expert[1:] != block_expert[:-1]).astype(jnp.int32)])
    later = (experts[None, :] > experts[:, None]) & (pcnt[None, :] > 0)
    next_expert = jnp.min(jnp.where(later, experts[None, :], ne), axis=1)
    next_expert = jnp.where(next_expert == ne, -1, next_expert)
    block_next = jnp.sum(jnp.where(block_expert[:, None] == experts, next_expert, 0), axis=1)
    n_used = (pends[-1:] // MOE_TM).astype(jnp.int32)
    xs = _dispatch(h, dest, pcnt, pends, n_pad)
    y = _experts(xs, block_expert, block_first, block_next, n_used, w_gate, w_up, w_down, layer)
    return _combine(x, y, dest, route, mods, layer)


class _Streams:
    def __init__(self, batch, seq, dec_batch, dec_seq):
        self.ctx = (0, batch, seq)
        self.lat = (batch * seq, dec_batch, dec_seq)


def _s5_layer(x, mods, layer, streams, norm_g, lam_re, lam_im, log_step, b_re, b_im, c_re, c_im, d_skip,
              w_a, b_a, w_b, b_b, s0_re, s0_im):
    n, d = x.shape
    g, p = lam_re.shape[1:]
    operands = _s5_params(lam_re, lam_im, log_step, b_re, b_im, c_re, c_im)
    h = _adanorm(x, norm_g, mods, layer, 0, 1)
    dvec = d_skip.reshape(1, d)
    row0, nseq, seq = streams.ctx
    zero = jnp.zeros((2, d // LANES, nseq, operands[3].shape[-1]), F32)
    z, sf = _s5_scan(h, operands, dvec, zero, row0, nseq, seq)
    row0, nseq, seq = streams.lat
    z, _ = _s5_scan(h, operands, dvec, _s5_state_to_lanes(s0_re, s0_im), row0, nseq, seq, z_prev=z)
    x = _mm_residual(z, [w_a.astype(BF16), w_b.astype(BF16)], [b_a, b_b], x, mods, layer, 2)
    new_re, new_im = _s5_state_from_lanes(sf, g, p)
    return x, new_re, new_im


def _mlstm_layer(x, mods, layer, streams, norm_g, w_in, b_gates, head_g, w_out, c0, n0, m0, c_slot):
    n, d = x.shape
    hh = ML_HEADS
    dv = d // hh
    dk = dv // 2
    qk = hh * dk
    n_main = 2 * qk + 2 * d
    col_scale = jnp.concatenate([jnp.ones((qk,), F32), jnp.full((qk,), dk ** -0.5, F32),
                                 jnp.ones((2 * d,), F32)])
    w_main = (w_in[:, :n_main] * col_scale).astype(BF16)
    w_gates = jnp.pad(w_in[:, n_main:], ((0, 0), (0, LANES - 4 * hh)))
    proj, gates = _mm_adanorm(x, norm_g, mods, layer, 0, 1, w_main, w_gates)
    k_t = jnp.transpose(proj[:, qk:2 * qk])
    g_c, w_c, e_c, d_r, w_r = _gate_prep(gates, b_gates, ML_T)
    fwd, bwd = slice(_G_FWD, _G_FWD + hh), slice(_G_BWD, _G_BWD + hh)
    cols = jnp.stack([g_c[:, fwd], w_c[:, fwd], e_c[:, fwd], g_c[:, bwd], w_c[:, bwd], e_c[:, bwd]], axis=-1)
    cols = jnp.pad(cols, ((0, 0), (0, 0), (0, 2)))
    rows = jnp.stack([d_r[fwd], w_r[fwd], d_r[bwd], w_r[bwd]], axis=1)

    def gate_views(row0, nseq, seq):
        gc = cols[row0:row0 + nseq * seq].reshape(nseq, seq, hh, 8)
        gr = rows[:, :, row0:row0 + nseq * seq].reshape(hh, 4, nseq, seq)
        return jnp.transpose(gc, (0, 2, 1, 3)), jnp.transpose(gr, (2, 0, 1, 3))

    def n_lanes(nv):
        return jnp.pad(jnp.transpose(nv, (0, 2, 1, 3))[..., None], ((0, 0),) * 4 + ((0, LANES - 1),))

    def m_lanes(mv):
        return jnp.broadcast_to(jnp.transpose(mv, (0, 2, 1))[..., None], mv.shape[:1] + (hh, 2, LANES))

    row0, nseq, seq = streams.ctx
    gcol, grow = gate_views(row0, nseq, seq)
    hn, c_f, n_f, m_f = _mlstm(
        proj, k_t, gcol, grow, head_g,
        jnp.zeros((nseq, 2, hh, dk, dv), F32), jnp.zeros((nseq, hh, 2, dk, LANES), F32),
        jnp.full((nseq, hh, 2, LANES), NEG_BIG, F32), row0, nseq, seq, c_slot=c_slot)
    row0, nseq, seq = streams.lat
    gcol, grow = gate_views(row0, nseq, seq)
    hn, _, _, _ = _mlstm(proj, k_t, gcol, grow, head_g, c0, n_lanes(n0), m_lanes(m0),
                         row0, nseq, seq, hn_prev=hn)
    x = _mm_residual(hn, [w_out.astype(BF16)], None, x, mods, layer, 2)
    return x, c_f, jnp.transpose(n_f[..., 0], (0, 2, 1, 3)), jnp.transpose(m_f[..., 0], (0, 2, 1))


def kernel(x_prompt, x_sample, state_s5_re, state_s5_im, state_mlstm_C, state_mlstm_n, state_mlstm_m, c, c_ctx, w_ada, b_ada, norm1_g, norm2_g, final_norm_g, s5_lambda_re, s5_lambda_im, s5_log_step, s5_b_re, s5_b_im, s5_c_re, s5_c_im, s5_d, s5_w_glu_a, s5_b_glu_a, s5_w_glu_b, s5_b_glu_b, ml_w_in, ml_b_gates, ml_head_norm_g, ml_w_out, moe_w_group, moe_b_group, moe_w_expert, moe_b_expert, moe_w_gate, moe_w_up, moe_w_down):
    batch, seq, d = x_prompt.shape
    dec_batch, dec_seq, _ = x_sample.shape
    depth = w_ada.shape[0]
    n_ctx = batch * seq
    streams = _Streams(batch, seq, dec_batch, dec_seq)

    cond = jnp.concatenate([c_ctx[None, :], c, jnp.zeros((MOD_ROWS - 1 - dec_batch, d), F32)], axis=0)
    mods = _modulation(cond, w_ada, b_ada).reshape(depth * MOD_ROWS * N_MOD, 1, d)
    mods = _Mods(mods, n_ctx, dec_seq)

    x = _embed(x_prompt, x_sample)
    new_s5_re, new_s5_im, new_n, new_m = [], [], [], []
    new_c = None
    n_ml_layers = depth // N_MIXERS
    for l in range(depth):
        j = l // N_MIXERS
        if l % N_MIXERS == 0:
            x, s_re, s_im = _s5_layer(
                x, mods, l, streams, norm1_g[l], s5_lambda_re[j], s5_lambda_im[j], s5_log_step[j],
                s5_b_re[j], s5_b_im[j], s5_c_re[j], s5_c_im[j], s5_d[j],
                s5_w_glu_a[j], s5_b_glu_a[j], s5_w_glu_b[j], s5_b_glu_b[j],
                state_s5_re[:, j], state_s5_im[:, j])
            new_s5_re.append(s_re)
            new_s5_im.append(s_im)
        else:
            x, new_c, n_f, m_f = _mlstm_layer(
                x, mods, l, streams, norm1_g[l], ml_w_in[j], ml_b_gates[j], ml_head_norm_g[j], ml_w_out[j],
                state_mlstm_C[:, j], state_mlstm_n[:, j], state_mlstm_m[:, j], (j, n_ml_layers, new_c))
            new_n.append(n_f)
            new_m.append(m_f)
        x = _moe_layer(x, mods, l, norm2_g[l], moe_w_group[l], moe_b_group[l], moe_w_expert[l],
                       moe_b_expert[l], moe_w_gate, moe_w_up, moe_w_down)
    y_prompt = _final_norm(x, final_norm_g, 0, n_ctx).reshape(batch, seq, d)
    y_sample = _final_norm(x, final_norm_g, n_ctx, dec_batch * dec_seq).reshape(dec_batch, dec_seq, d)
    return (y_prompt, y_sample, jnp.stack(new_s5_re, axis=1), jnp.stack(new_s5_im, axis=1),
            new_c, jnp.stack(new_n, axis=1), jnp.stack(new_m, axis=1))
```

```python
import functools
import math

import jax
import jax.numpy as jnp
from jax import lax
from jax.experimental import pallas as pl
from jax.experimental.pallas import tpu as pltpu

F32 = jnp.float32
BF16 = jnp.bfloat16

S5_GROUP_CH = 16
ML_HEADS = 8
MOE_GROUPS = 4
MOE_EPG = 8
MOE_EXPERTS = MOE_GROUPS * MOE_EPG
GRID_W = 64
POS_BASE = 10000.0
RMS_EPS = 1e-6
NEG_BIG = -1e30
N_MIXERS = 2
N_MOD = 6

LANES = 128
SUBLANES = 8
VMEM_LIMIT_BYTES = 56 * 1024 * 1024
EXPERT_VMEM_LIMIT_BYTES = 60 * 1024 * 1024

MOD_ROWS = 16
S5_LANE_GROUPS = LANES // S5_GROUP_CH
S5_TC = 64
ML_T = 256
ML_HP = 2
ML_HP_LONG = 4
ROUTE_LANES = LANES
MOE_TM = 256
ROW_TM = 512
DISPATCH_TM = 2048
ROW_ISSUE_UNROLL = 8


def _cparams(sem, vmem=VMEM_LIMIT_BYTES):
    return pltpu.CompilerParams(dimension_semantics=sem, vmem_limit_bytes=vmem)


def _silu(x):
    return x * jax.nn.sigmoid(x)


def _gelu_tanh(x):
    c = math.sqrt(2.0 / math.pi)
    return 0.5 * x * (1.0 + jnp.tanh(c * (x + 0.044715 * (x * x * x))))


def _ada_norm_tile(x, g, shift, scale):
    r = lax.rsqrt(jnp.mean(x * x, axis=-1, keepdims=True) + RMS_EPS)
    return (x * r * g) * (1.0 + scale) + shift


def _mod_kernel(c_ref, w_ref, b_ref, o_ref):
    o_ref[...] = _dot_3pass(_silu(c_ref[...]), w_ref[...]) + b_ref[...]


def _modulation(cond, w_ada, b_ada, tn=1024):
    depth, d, n = w_ada.shape
    return pl.pallas_call(
        _mod_kernel,
        out_shape=jax.ShapeDtypeStruct((depth, MOD_ROWS, n), F32),
        grid=(depth, n // tn),
        in_specs=[
            pl.BlockSpec((MOD_ROWS, d), lambda l, j: (0, 0)),
            pl.BlockSpec((None, d, tn), lambda l, j: (l, 0, j)),
            pl.BlockSpec((None, 1, tn), lambda l, j: (l, 0, j)),
        ],
        out_specs=pl.BlockSpec((None, MOD_ROWS, tn), lambda l, j: (l, 0, j)),
        compiler_params=_cparams(("parallel", "parallel")),
        name="modulation",
    )(cond, w_ada, b_ada.reshape(depth, 1, n))


class _Mods:
    def __init__(self, mods, n_ctx, dec_seq):
        self.mods = mods
        self.n_ctx = n_ctx
        self.dec_seq = dec_seq
        self.d = mods.shape[-1]

    def spec(self, layer, k, tm, row_offset=0):
        base = layer * MOD_ROWS * N_MOD + k
        n_ctx, dec_seq = self.n_ctx, self.dec_seq

        def index(i, *_):
            row0 = i * tm + row_offset
            r = jnp.where(row0 < n_ctx, 0, 1 + (row0 - n_ctx) // dec_seq)
            return (base + r * N_MOD, 0, 0)

        return pl.BlockSpec((None, 1, self.d), index)


def _embed_kernel(xc_ref, xl_ref, p_ref, o_ref, *, ctx_blocks):
    i = pl.program_id(0)

    @pl.when(i < ctx_blocks)
    def _():
        o_ref[...] = xc_ref[...]

    @pl.when(i >= ctx_blocks)
    def _():
        o_ref[...] = xl_ref[...] + p_ref[...]


def _grid_pos_embed(n_tokens, d):
    rows = n_tokens // GRID_W
    r, col = jnp.meshgrid(jnp.arange(rows, dtype=F32), jnp.arange(GRID_W, dtype=F32), indexing="ij")
    quarter = d // 4
    omega = 1.0 / (POS_BASE ** (jnp.arange(quarter, dtype=F32) / quarter))

    def axis_embed(pos):
        ang = pos.reshape(-1, 1) * omega[None, :]
        return jnp.concatenate([jnp.sin(ang), jnp.cos(ang)], axis=-1)

    return jnp.concatenate([axis_embed(r), axis_embed(col)], axis=-1)


def _embed(x_prompt, x_sample, tm=512):
    batch, seq, d = x_prompt.shape
    dec_batch, dec_seq, _ = x_sample.shape
    n_ctx, n_lat = batch * seq, dec_batch * dec_seq
    ctx_blocks, seq_blocks = n_ctx // tm, dec_seq // tm
    pos = _grid_pos_embed(dec_seq, d)
    return pl.pallas_call(
        functools.partial(_embed_kernel, ctx_blocks=ctx_blocks),
        out_shape=jax.ShapeDtypeStruct((n_ctx + n_lat, d), F32),
        grid=((n_ctx + n_lat) // tm,),
        in_specs=[pl.BlockSpec((tm, d), lambda i: (jnp.minimum(i, ctx_blocks - 1), 0)),
                  pl.BlockSpec((tm, d), lambda i: (jnp.maximum(i - ctx_blocks, 0), 0)),
                  pl.BlockSpec((tm, d), lambda i: (jnp.maximum(i - ctx_blocks, 0) % seq_blocks, 0))],
        out_specs=pl.BlockSpec((tm, d), lambda i: (i, 0)),
        compiler_params=_cparams(("parallel",)),
        name="embed",
    )(x_prompt.reshape(n_ctx, d), x_sample.reshape(n_lat, d), pos)


def _adanorm_kernel(x_ref, g_ref, sh_ref, sc_ref, o_ref):
    o_ref[...] = _ada_norm_tile(x_ref[...], g_ref[...], sh_ref[...], sc_ref[...]).astype(o_ref.dtype)


def _adanorm(x, g, mods, layer, k_shift, k_scale, tm=512):
    n, d = x.shape
    return pl.pallas_call(
        _adanorm_kernel,
        out_shape=jax.ShapeDtypeStruct((n, d), F32),
        grid=(n // tm,),
        in_specs=[pl.BlockSpec((tm, d), lambda i: (i, 0)),
                  pl.BlockSpec((1, d), lambda i: (0, 0)),
                  mods.spec(layer, k_shift, tm),
                  mods.spec(layer, k_scale, tm)],
        out_specs=pl.BlockSpec((tm, d), lambda i: (i, 0)),
        compiler_params=_cparams(("parallel",)),
        name="adanorm",
    )(x, g.reshape(1, d), mods.mods, mods.mods)


def _final_norm_kernel(x_ref, g_ref, o_ref):
    x = x_ref[...]
    r = lax.rsqrt(jnp.mean(x * x, axis=-1, keepdims=True) + RMS_EPS)
    o_ref[...] = x * r * g_ref[...]


def _final_norm(x, g, row0, nrows, tm=512):
    d = x.shape[1]
    off = row0 // tm
    return pl.pallas_call(
        _final_norm_kernel,
        out_shape=jax.ShapeDtypeStruct((nrows, d), F32),
        grid=(nrows // tm,),
        in_specs=[pl.BlockSpec((tm, d), lambda i: (i + off, 0)),
                  pl.BlockSpec((1, d), lambda i: (0, 0))],
        out_specs=pl.BlockSpec((tm, d), lambda i: (i, 0)),
        compiler_params=_cparams(("parallel",)),
        name="final_norm",
    )(x, g.reshape(1, d))


def _cmul(ar, ai, br, bi):
    return ar * br - ai * bi, ar * bi + ai * br


def _s5_param_kernel(lre_ref, lim_ref, ls_ref, bre_ref, bim_ref, cre_ref, cim_ref,
                     b2_ref, c2_ref, d2_ref, a2_ref):
    h, s = bre_ref.shape
    gl = LANES // h
    p = s // gl
    lr = lre_ref[...]
    li = lim_ref[...]
    dt = jnp.exp(ls_ref[...])
    mag = jnp.exp(lr * dt)
    ar = mag * jnp.cos(li * dt)
    ai = mag * jnp.sin(li * dt)
    den = lr * lr + li * li
    zr = ((ar - 1.0) * lr + ai * li) / den
    zi = (ai * lr - (ar - 1.0) * li) / den
    a2 = _cmul(ar, ai, ar, ai)
    bb = _cmul(zr, zi, bre_ref[...], bim_ref[...])
    abb = _cmul(ar, ai, *bb)
    cc = (cre_ref[...], cim_ref[...])
    ca = _cmul(*cc, ar, ai)
    ca2 = _cmul(*cc, *a2)

    same_b = (lax.broadcasted_iota(jnp.int32, (LANES, s), 0) // h
              == lax.broadcasted_iota(jnp.int32, (LANES, s), 1) // p)
    same_c = (lax.broadcasted_iota(jnp.int32, (s, LANES), 0) // p
              == lax.broadcasted_iota(jnp.int32, (s, LANES), 1) // h)
    spread = (lax.broadcasted_iota(jnp.int32, (h, LANES), 1) % h
              == lax.broadcasted_iota(jnp.int32, (h, LANES), 0)).astype(BF16)

    def bblock(re, im):
        def one(x):
            return jnp.where(same_b, jnp.concatenate([x] * gl, axis=0), 0.0)
        return jnp.concatenate([one(re), one(im)], axis=1)

    def cblock(re, im):
        def one(x):
            t = sum(lax.dot_general(part, spread, (((0,), (0,)), ((), ())), preferred_element_type=F32)
                    for part in _split_bf16(x))
            return jnp.where(same_c, t, 0.0)
        return jnp.concatenate([one(re), one(-im)], axis=0)

    b_blk = bblock(*bb)
    ab_blk = bblock(*abb)
    c_blk = cblock(*cc)
    cb = _dot_3pass(b_blk, c_blk)
    cab = _dot_3pass(ab_blk, c_blk)
    b2_ref[...] = jnp.concatenate([ab_blk, b_blk], axis=0).astype(BF16)
    c2_ref[...] = jnp.concatenate([cblock(*ca), cblock(*ca2)], axis=1).astype(BF16)
    d2_ref[...] = jnp.concatenate([jnp.concatenate([cb, cab], axis=1),
                                   jnp.concatenate([jnp.zeros_like(cb), cb], axis=1)], axis=0).astype(BF16)
    a2_ref[...] = jnp.concatenate(a2, axis=1)


def _s5_params(lam_re, lam_im, log_step, b_re, b_im, c_re, c_im):
    _, g, p = lam_re.shape
    h = b_re.shape[-1]
    gl = S5_LANE_GROUPS
    nlc = g // gl
    s = gl * p

    def lanes(v):
        return v.reshape(2, nlc, 1, s)

    def rows(m):
        return jnp.transpose(m.reshape(2, h, nlc, s), (0, 2, 1, 3))

    step = jnp.broadcast_to(log_step[:, :, None], (2, g, p))
    vec = pl.BlockSpec((None, None, 1, s), lambda d, j: (d, j, 0, 0))
    mat = pl.BlockSpec((None, None, h, s), lambda d, j: (d, j, 0, 0))

    def out(r, c, dt):
        return (jax.ShapeDtypeStruct((2, nlc, r, c), dt),
                pl.BlockSpec((None, None, r, c), lambda d, j: (d, j, 0, 0)))

    outs = [out(2 * LANES, 2 * s, BF16), out(2 * s, 2 * LANES, BF16), out(2 * LANES, 2 * LANES, BF16),
            out(1, 2 * s, F32)]
    return pl.pallas_call(
        _s5_param_kernel,
        out_shape=tuple(o[0] for o in outs),
        grid=(2, nlc),
        in_specs=[vec, vec, vec, mat, mat, mat, mat],
        out_specs=tuple(o[1] for o in outs),
        compiler_params=_cparams(("parallel", "parallel")),
        name="s5_params",
    )(lanes(lam_re), lanes(lam_im), lanes(step),
      rows(jnp.transpose(b_re, (0, 3, 1, 2))), rows(jnp.transpose(b_im, (0, 3, 1, 2))),
      rows(jnp.transpose(c_re, (0, 2, 1, 3))), rows(jnp.transpose(c_im, (0, 2, 1, 3))))


def _s5_state_to_lanes(s_re, s_im):
    b, _, g, p = s_re.shape
    gl = S5_LANE_GROUPS
    nlc = g // gl

    def lay(s):
        return jnp.transpose(s.reshape(b, 2, nlc, gl * p), (1, 2, 0, 3))

    return jnp.concatenate([lay(s_re), lay(s_im)], axis=-1)


def _s5_state_from_lanes(s, g, p):
    _, nlc, b, s2 = s.shape
    half = s2 // 2

    def unlay(t):
        return jnp.transpose(t, (2, 0, 1, 3)).reshape(b, 2, g, p)

    return unlay(s[..., :half]), unlay(s[..., half:])


def _s5_scan_kernel(*refs, seq, tc, aliased):
    if aliased:
        h_ref, b2_ref, c2_ref, d2_ref, a_ref, d_ref, s0_ref, _, z_ref, sf_ref = refs[:10]
    else:
        h_ref, b2_ref, c2_ref, d2_ref, a_ref, d_ref, s0_ref, z_ref, sf_ref = refs[:9]
    scr = (refs[-10:-5], refs[-5:])
    nb = SUBLANES
    half = a_ref.shape[-1] // 2
    nc = seq // tc
    npair = tc // 2
    a_re = [jnp.broadcast_to(a_ref[dr][:, :half], (nb, half)) for dr in range(2)]
    a_im = [jnp.broadcast_to(a_ref[dr][:, half:], (nb, half)) for dr in range(2)]

    def chunk(c, carry):
        t0s = (c * tc, (nc - 1 - c) * tc)

        def pair_rows(dr, p):
            if dr == 0:
                first = t0s[0] + 2 * p
                return pl.ds(first, nb, stride=seq), pl.ds(first + 1, nb, stride=seq)
            first = t0s[1] + tc - 1 - 2 * p
            return pl.ds(first, nb, stride=seq), pl.ds(first - 1, nb, stride=seq)

        carry = list(carry)
        for dr in range(2):
            u_scr, bu_scr, _, _, _ = scr[dr]
            for p in range(npair):
                r1, r2 = pair_rows(dr, p)
                u_scr[p * nb:(p + 1) * nb, :LANES] = h_ref[r1, :]
                u_scr[p * nb:(p + 1) * nb, LANES:] = h_ref[r2, :]
            bu_scr[...] = jnp.dot(u_scr[...].astype(BF16), b2_ref[dr], preferred_element_type=F32)
        for dr in range(2):
            _, bu_scr, x_scr, _, _ = scr[dr]
            xr, xi = carry[2 * dr], carry[2 * dr + 1]
            for p in range(npair):
                x_scr[p * nb:(p + 1) * nb, :half] = xr
                x_scr[p * nb:(p + 1) * nb, half:] = xi
                bu = bu_scr[p * nb:(p + 1) * nb, :]
                xr, xi = (a_re[dr] * xr - a_im[dr] * xi + bu[:, :half],
                          a_re[dr] * xi + a_im[dr] * xr + bu[:, half:])
            carry[2 * dr], carry[2 * dr + 1] = xr, xi
        for dr in range(2):
            u_scr, _, x_scr, yo_scr, y_scr = scr[dr]
            yo_scr[...] = jnp.dot(x_scr[...].astype(BF16), c2_ref[dr], preferred_element_type=F32) \
                + jnp.dot(u_scr[...].astype(BF16), d2_ref[dr], preferred_element_type=F32)
            for p in range(npair):
                r1, r2 = pair_rows(dr, p)
                y_scr[r1, :] = yo_scr[p * nb:(p + 1) * nb, :LANES]
                y_scr[r2, :] = yo_scr[p * nb:(p + 1) * nb, LANES:]
        return tuple(carry)

    s0f = s0_ref[0]
    s0b = s0_ref[1]
    init = (s0f[:, :half], s0f[:, half:], s0b[:, :half], s0b[:, half:])
    xr_f, xi_f, xr_b, xi_b = lax.fori_loop(0, nc, chunk, init)
    sf_ref[0, :, :half] = xr_f
    sf_ref[0, :, half:] = xi_f
    sf_ref[1, :, :half] = xr_b
    sf_ref[1, :, half:] = xi_b
    y = h_ref[...] * d_ref[...] + scr[0][4][...] + scr[1][4][...]
    z_ref[...] = _gelu_tanh(y).astype(z_ref.dtype)


def _s5_scan(h, operands, dvec, s0, row0, nseq, seq, z_prev=None):
    b2, c2, d2, avec = operands
    n, d = h.shape
    nlc = d // LANES
    s2 = avec.shape[-1]
    nb = SUBLANES
    rows = nb * seq
    off = row0 // rows
    aliased = z_prev is not None
    in_specs = [
        pl.BlockSpec((rows, LANES), lambda i, j: (i + off, j)),
        pl.BlockSpec((2, None, 2 * LANES, s2), lambda i, j: (0, j, 0, 0)),
        pl.BlockSpec((2, None, s2, 2 * LANES), lambda i, j: (0, j, 0, 0)),
        pl.BlockSpec((2, None, 2 * LANES, 2 * LANES), lambda i, j: (0, j, 0, 0)),
        pl.BlockSpec((2, None, 1, s2), lambda i, j: (0, j, 0, 0)),
        pl.BlockSpec((1, LANES), lambda i, j: (0, j)),
        pl.BlockSpec((2, None, nb, s2), lambda i, j: (0, j, i, 0)),
    ]
    args = [h, b2, c2, d2, avec, dvec, s0]
    io_alias = {}
    if aliased:
        in_specs.append(pl.BlockSpec(memory_space=pl.ANY))
        args.append(z_prev)
        io_alias = {len(args) - 1: 0}
    npair_rows = S5_TC // 2 * nb
    return pl.pallas_call(
        functools.partial(_s5_scan_kernel, seq=seq, tc=S5_TC, aliased=aliased),
        out_shape=(jax.ShapeDtypeStruct((n, d), BF16), jax.ShapeDtypeStruct((2, nlc, nseq, s2), F32)),
        grid=(nseq // nb, nlc),
        in_specs=in_specs,
        out_specs=(pl.BlockSpec((rows, LANES), lambda i, j: (i + off, j)),
                   pl.BlockSpec((2, None, nb, s2), lambda i, j: (0, j, i, 0))),
        scratch_shapes=2 * [pltpu.VMEM((npair_rows, 2 * LANES), F32),
                            pltpu.VMEM((npair_rows, s2), F32),
                            pltpu.VMEM((npair_rows, s2), F32),
                            pltpu.VMEM((npair_rows, 2 * LANES), F32),
                            pltpu.VMEM((rows, LANES), F32)],
        input_output_aliases=io_alias,
        compiler_params=_cparams(("parallel", "parallel")),
        name="s5_scan",
    )(*args)


def _mm_res_kernel(*refs, n_w, has_bias):
    z_ref = refs[0]
    w_refs = refs[1:1 + n_w]
    pos = 1 + n_w
    b_refs = refs[pos:pos + n_w] if has_bias else ()
    pos += n_w if has_bias else 0
    x_ref, gate_ref, o_ref = refs[pos:pos + 3]
    z = z_ref[...]
    acc = [jnp.dot(z, w[...], preferred_element_type=F32) for w in w_refs]
    if has_bias:
        acc = [a + b[...] for a, b in zip(acc, b_refs)]
    y = acc[0] if n_w == 1 else acc[0] * jax.nn.sigmoid(acc[1])
    o_ref[...] = x_ref[...] + gate_ref[...] * y


def _mm_residual(z, ws, bs, x, mods, layer, k_gate, tm=1024, tn=1024):
    n, k = z.shape
    n_out = ws[0].shape[1]
    n_w = len(ws)
    has_bias = bs is not None
    in_specs = [pl.BlockSpec((tm, k), lambda i, j: (i, 0))]
    in_specs += [pl.BlockSpec((k, tn), lambda i, j: (0, j)) for _ in ws]
    args = [z, *ws]
    if has_bias:
        in_specs += [pl.BlockSpec((1, tn), lambda i, j: (0, j)) for _ in bs]
        args += [b.reshape(1, n_out) for b in bs]
    gate_spec = mods.spec(layer, k_gate, tm)
    gate_spec = pl.BlockSpec((None, 1, tn), lambda i, j, f=gate_spec.index_map: (f(i)[0], 0, j))
    in_specs += [pl.BlockSpec((tm, tn), lambda i, j: (i, j)), gate_spec]
    args += [x, mods.mods]
    return pl.pallas_call(
        functools.partial(_mm_res_kernel, n_w=n_w, has_bias=has_bias),
        out_shape=jax.ShapeDtypeStruct((n, n_out), F32),
        grid=(n // tm, n_out // tn),
        in_specs=in_specs,
        out_specs=pl.BlockSpec((tm, tn), lambda i, j: (i, j)),
        compiler_params=_cparams(("parallel", "parallel")),
        name=f"proj_residual_{n_w}w",
    )(*args)


def _split_bf16(a):
    hi = a.astype(BF16)
    return hi, (a - hi.astype(F32)).astype(BF16)


def _dot_3pass(a, b):
    a_hi, a_lo = _split_bf16(a)
    b_hi, b_lo = _split_bf16(b)
    return (jnp.dot(a_hi, b_hi, preferred_element_type=F32) + jnp.dot(a_hi, b_lo, preferred_element_type=F32)
            + jnp.dot(a_lo, b_hi, preferred_element_type=F32))


def _mm_norm_kernel(x_ref, g_ref, sh_ref, sc_ref, w_ref, ws_ref, o_ref, os_ref, h_scr):
    @pl.when(pl.program_id(1) == 0)
    def _():
        h = _ada_norm_tile(x_ref[...], g_ref[...], sh_ref[...], sc_ref[...])
        h_scr[...] = h.astype(h_scr.dtype)
        os_ref[...] = _dot_3pass(h, ws_ref[...])

    o_ref[...] = jnp.dot(h_scr[...], w_ref[...], preferred_element_type=F32).astype(o_ref.dtype)


def _mm_adanorm(x, g, mods, layer, k_shift, k_scale, w, w_side, tm=1024, tn=1024):
    n, d = x.shape
    n_out = w.shape[1]
    n_side = w_side.shape[1]
    return pl.pallas_call(
        _mm_norm_kernel,
        out_shape=(jax.ShapeDtypeStruct((n, n_out), BF16), jax.ShapeDtypeStruct((n, n_side), F32)),
        grid=(n // tm, n_out // tn),
        in_specs=[pl.BlockSpec((tm, d), lambda i, j: (i, 0)),
                  pl.BlockSpec((1, d), lambda i, j: (0, 0)),
                  mods.spec(layer, k_shift, tm),
                  mods.spec(layer, k_scale, tm),
                  pl.BlockSpec((d, tn), lambda i, j: (0, j)),
                  pl.BlockSpec((d, n_side), lambda i, j: (0, 0))],
        out_specs=(pl.BlockSpec((tm, tn), lambda i, j: (i, j)),
                   pl.BlockSpec((tm, n_side), lambda i, j: (i, 0))),
        scratch_shapes=[pltpu.VMEM((tm, d), BF16)],
        compiler_params=_cparams(("parallel", "arbitrary")),
        name="adanorm_proj",
    )(x, g.reshape(1, d), mods.mods, mods.mods, w, w_side)


_G_FWD = ML_HEADS
_G_BWD = 3 * ML_HEADS


def _gate_kernel(xc_ref, xr_ref, bc_ref, br_ref, g_ref, w_ref, e_ref, dr_ref, wr_ref):
    t = xc_ref.shape[0]
    r_i = lax.broadcasted_iota(jnp.int32, (t, t), 0)
    c_i = lax.broadcasted_iota(jnp.int32, (t, t), 1)
    lower = (c_i <= r_i).astype(F32)
    upper = (c_i >= r_i).astype(F32)
    hi = lax.Precision.HIGHEST
    xc = xc_ref[...] + bc_ref[...]
    fc = jax.nn.log_sigmoid(xc)
    lane = lax.broadcasted_iota(jnp.int32, xc.shape, 1)
    g_c = jnp.where(lane < 2 * ML_HEADS,
                    jnp.dot(lower, fc, preferred_element_type=F32, precision=hi),
                    jnp.dot(upper, fc, preferred_element_type=F32, precision=hi))
    e_c = jnp.broadcast_to(jnp.sum(fc, axis=0, keepdims=True), xc.shape)
    g_ref[...] = g_c
    e_ref[...] = e_c
    w_ref[...] = e_c - g_c + pltpu.roll(xc, ML_HEADS, 1)
    xr = xr_ref[...] + br_ref[...]
    fr = jax.nn.log_sigmoid(xr)
    row = lax.broadcasted_iota(jnp.int32, xr.shape, 0)
    g_r = jnp.where(row < 2 * ML_HEADS,
                    jnp.dot(fr, upper, preferred_element_type=F32, precision=hi),
                    jnp.dot(fr, lower, preferred_element_type=F32, precision=hi))
    i_r = pltpu.roll(xr, ML_HEADS, 0)
    dr_ref[...] = i_r - g_r
    wr_ref[...] = jnp.sum(fr, axis=1, keepdims=True) - g_r + i_r


def _gate_prep(gates, b_gates, tch):
    n = gates.shape[0]
    ng = 4 * ML_HEADS
    bias_c = jnp.pad(b_gates, (0, LANES - ng)).reshape(1, LANES)
    bias_r = b_gates.reshape(ng, 1)
    gates_r = jnp.transpose(gates[:, :ng])
    col = pl.BlockSpec((tch, LANES), lambda i: (i, 0))
    rowb = pl.BlockSpec((ng, tch), lambda i: (0, i))
    return pl.pallas_call(
        _gate_kernel,
        out_shape=(jax.ShapeDtypeStruct((n, LANES), F32),) * 3 + (jax.ShapeDtypeStruct((ng, n), F32),) * 2,
        grid=(n // tch,),
        in_specs=[col, rowb, pl.BlockSpec((1, LANES), lambda i: (0, 0)),
                  pl.BlockSpec((ng, 1), lambda i: (0, 0))],
        out_specs=(col, col, col, rowb, rowb),
        compiler_params=_cparams(("parallel",)),
        name="mlstm_gates",
    )(gates, gates_r, bias_c, bias_r)


def _mlstm_kernel(q_ref, kt_ref, v_ref, o_ref, gcol_ref, grow_ref, hg_ref,
                  c0_ref, n0_ref, m0_ref, hn_ref, cf_ref, nf_ref, mf_ref, *scratch, seq, tch, hp):
    nc = seq // tch
    dv = v_ref.shape[-1] // hp
    dk = q_ref.shape[-1] // hp
    tt = lax.broadcasted_iota(jnp.int32, (tch, tch), 0)
    ss = lax.broadcasted_iota(jnp.int32, (tch, tch), 1)
    ones_col = (lax.broadcasted_iota(jnp.int32, (tch, LANES), 1) == 0).astype(BF16)
    scr = {(hd, dr): scratch[3 * (2 * hd + dr):3 * (2 * hd + dr) + 3] for hd in range(hp) for dr in range(2)}
    for (hd, dr), (_, c_scr, vx) in scr.items():
        c_scr[:, :dv] = c0_ref[dr, hd]
        c_scr[:, dv:] = n0_ref[hd, dr]
        vx[:, dv:] = ones_col

    def chunk_dir(hd, dr, r0, m):
        hacc, c_scr, vx = scr[hd, dr]
        rows = pl.ds(pl.multiple_of(r0, tch), tch)
        q = q_ref[rows, hd * dk:(hd + 1) * dk]
        kt = kt_ref[hd * dk:(hd + 1) * dk, rows]
        vx[:, :dv] = v_ref[rows, hd * dv:(hd + 1) * dv]
        gc = gcol_ref[hd, rows, :]
        gr = grow_ref[hd, :, rows]
        g_col = gc[:, 3 * dr:3 * dr + 1]
        w_col = gc[:, 3 * dr + 1:3 * dr + 2]
        e_col = gc[:, 3 * dr + 2:3 * dr + 3]
        d_row = gr[2 * dr:2 * dr + 1, :]
        w_row = gr[2 * dr + 1:2 * dr + 2, :]
        mask = (ss <= tt) if dr == 0 else (ss >= tt)
        a_col = g_col + m
        dmat = jnp.where(mask, g_col + d_row, -jnp.inf)
        mt = jnp.maximum(a_col, jnp.max(dmat, axis=1, keepdims=True))
        qk = jnp.dot(q, kt, preferred_element_type=F32)
        s = (qk * jnp.exp(dmat - mt)).astype(BF16)
        inter = jnp.exp(a_col - mt)
        vext = vx[...]
        tot = jnp.dot(s, vext, preferred_element_type=F32) \
            + inter * jnp.dot(q, c_scr[...].astype(BF16), preferred_element_type=F32)
        den = tot[:, dv:dv + 1]
        hacc[rows, :] = tot[:, :dv] / jnp.maximum(jnp.abs(den), jnp.exp(-mt))
        g_end = jnp.max(e_col, axis=0, keepdims=True)
        m_new = jnp.maximum(g_end + m, jnp.max(w_col, axis=0, keepdims=True))
        decay = jnp.exp(g_end + m - m_new)
        kw = (kt.astype(F32) * jnp.exp(w_row - m_new)).astype(BF16)
        c_scr[...] = decay * c_scr[...] + jnp.dot(kw, vext, preferred_element_type=F32)
        return m_new

    keys = list(scr)

    def body(c, carry):
        return tuple(chunk_dir(hd, dr, (c if dr == 0 else nc - 1 - c) * tch, m)
                     for (hd, dr), m in zip(keys, carry))

    m_fin = lax.fori_loop(0, nc, body, tuple(m0_ref[hd, dr:dr + 1, 0:1] for hd, dr in keys))
    for (hd, dr), m in zip(keys, m_fin):
        c_scr = scr[hd, dr][1]
        cf_ref[dr, hd] = c_scr[:, :dv]
        nf_ref[hd, dr] = c_scr[:, dv:]
        mf_ref[hd, dr:dr + 1, :] = jnp.broadcast_to(m, (1, LANES))
    for hd in range(hp):
        cols = slice(hd * dv, (hd + 1) * dv)
        hs = scr[hd, 0][0][...] + scr[hd, 1][0][...]
        hn = hs * lax.rsqrt(jnp.mean(hs * hs, axis=-1, keepdims=True) + RMS_EPS)
        hn = hn * hg_ref[:, cols] * jax.nn.sigmoid(o_ref[:, cols].astype(F32))
        hn_ref[:, cols] = hn.astype(hn_ref.dtype)


def _mlstm(proj, k_t, gcol, grow, head_g, c0, n0, m0, row0, nseq, seq, hn_prev=None, c_slot=None, hp=ML_HP):
    n = proj.shape[0]
    h = ML_HEADS
    dk = c0.shape[-2]
    dv = c0.shape[-1]
    d = h * dv
    qk = h * dk
    off = row0 // seq
    tch = min(ML_T, seq)
    aliased = hn_prev is not None
    c_spec = pl.BlockSpec((None, 2, hp, dk, dv), lambda b, j: (b, 0, j, 0, 0))
    n_spec = pl.BlockSpec((None, hp, 2, dk, LANES), lambda b, j: (b, j, 0, 0, 0))
    m_spec = pl.BlockSpec((None, hp, 2, LANES), lambda b, j: (b, j, 0, 0))
    wk, wv = hp * dk, hp * dv
    in_specs = [
        pl.BlockSpec((seq, wk), lambda b, j: (b + off, j)),
        pl.BlockSpec((wk, seq), lambda b, j: (j, b + off)),
        pl.BlockSpec((seq, wv), lambda b, j: (b + off, 2 * qk // wv + j)),
        pl.BlockSpec((seq, wv), lambda b, j: (b + off, (2 * qk + d) // wv + j)),
        pl.BlockSpec((None, hp, seq, 8), lambda b, j: (b, j, 0, 0)),
        pl.BlockSpec((None, hp, 4, seq), lambda b, j: (b, j, 0, 0)),
        pl.BlockSpec((1, wv), lambda b, j: (0, j)),
        c_spec, n_spec, m_spec,
    ]
    args = [proj, k_t, proj, proj, gcol, grow, head_g.reshape(1, d), c0, n0, m0]
    n_in = len(args)
    io_alias = {}
    if aliased:
        in_specs.append(pl.BlockSpec(memory_space=pl.ANY))
        args.append(hn_prev)
        io_alias[len(args) - 1] = 0
    c_type = jax.ShapeDtypeStruct((nseq, 2, h, dk, dv), F32)
    c_out_spec = c_spec
    if c_slot is not None:
        slot, n_slots, c_prev = c_slot
        c_type = jax.ShapeDtypeStruct((nseq, n_slots, 2, h, dk, dv), F32)
        c_out_spec = pl.BlockSpec((None, None, 2, hp, dk, dv), lambda b, j: (b, slot, 0, j, 0, 0))
        if c_prev is not None:
            in_specs.append(pl.BlockSpec(memory_space=pl.ANY))
            args.append(c_prev)
            io_alias[len(args) - 1] = 1

    def kern(*refs):
        _mlstm_kernel(*refs[:n_in], *refs[len(args):], seq=seq, tch=tch, hp=hp)

    return pl.pallas_call(
        kern,
        out_shape=(jax.ShapeDtypeStruct((n, d), BF16), c_type,
                   jax.ShapeDtypeStruct((nseq, h, 2, dk, LANES), F32),
                   jax.ShapeDtypeStruct((nseq, h, 2, LANES), F32)),
        grid=(nseq, h // hp),
        in_specs=in_specs,
        out_specs=(pl.BlockSpec((seq, wv), lambda b, j: (b + off, j)), c_out_spec, n_spec, m_spec),
        scratch_shapes=2 * hp * [pltpu.VMEM((seq, dv), F32), pltpu.VMEM((dk, dv + LANES), F32),
                                 pltpu.VMEM((tch, dv + LANES), BF16)],
        input_output_aliases=io_alias,
        compiler_params=_cparams(("parallel", "parallel")),
        name="mlstm",
    )(*args)


_R_EID, _R_W, _R_RANK = 0, 2, 4
_R_LOGIT0 = MOE_GROUPS


def _router_kernel(x_ref, g_ref, sh_ref, sc_ref, whi_ref, wlo_ref, b_ref, h_ref, route_ref, cnt_ref, carry):
    i = pl.program_id(0)

    @pl.when(i == 0)
    def _():
        carry[...] = jnp.zeros_like(carry)

    h = _ada_norm_tile(x_ref[...], g_ref[...], sh_ref[...], sc_ref[...])
    h_ref[...] = h
    tm = h.shape[0]
    h_hi = h.astype(BF16)
    h_lo = (h - h_hi.astype(F32)).astype(BF16)
    logits = (jnp.dot(h_hi, whi_ref[...], preferred_element_type=F32)
              + jnp.dot(h_hi, wlo_ref[...], preferred_element_type=F32)
              + jnp.dot(h_lo, whi_ref[...], preferred_element_type=F32)) + b_ref[...]
    lane = lax.broadcasted_iota(jnp.int32, logits.shape, 1)
    big = jnp.int32(ROUTE_LANES)

    def first_lane(cond):
        return jnp.min(jnp.where(cond, lane, big), axis=1, keepdims=True)

    glog = jnp.where(lane < MOE_GROUPS, logits, -jnp.inf)
    ge = jnp.exp(glog - jnp.max(glog, axis=1, keepdims=True))
    pgrp = ge / jnp.sum(ge, axis=1, keepdims=True)
    pg = jnp.max(pgrp, axis=1, keepdims=True)
    grp = first_lane(pgrp == pg)
    e_lane = lane - _R_LOGIT0
    in_grp = (e_lane >= 0) & (e_lane < MOE_EXPERTS) & ((e_lane // MOE_EPG) == grp)
    elog = jnp.where(in_grp, logits, -jnp.inf)
    ee = jnp.exp(elog - jnp.max(elog, axis=1, keepdims=True))
    pe = jnp.where(in_grp, ee / jnp.sum(ee, axis=1, keepdims=True), -1.0)
    p0 = jnp.max(pe, axis=1, keepdims=True)
    l0 = first_lane(pe == p0)
    pe1 = jnp.where(lane == l0, -1.0, pe)
    p1 = jnp.max(pe1, axis=1, keepdims=True)
    l1 = first_lane(pe1 == p1)
    psum = p0 + p1
    w0 = pg * p0 / psum
    w1 = pg * p1 / psum
    onehot = ((lane == l0) | (lane == l1)).astype(BF16)
    r_i = lax.broadcasted_iota(jnp.int32, (tm, tm), 0)
    c_i = lax.broadcasted_iota(jnp.int32, (tm, tm), 1)
    tri = (c_i < r_i).astype(BF16)
    before = jnp.dot(tri, onehot, preferred_element_type=F32) + carry[...]
    rank0 = jnp.sum(jnp.where(lane == l0, before, 0.0), axis=1, keepdims=True)
    rank1 = jnp.sum(jnp.where(lane == l1, before, 0.0), axis=1, keepdims=True)
    carry[...] = carry[...] + jnp.sum(onehot.astype(F32), axis=0, keepdims=True)
    cnt_ref[...] = carry[...]
    cols = [(l0 - _R_LOGIT0).astype(F32), (l1 - _R_LOGIT0).astype(F32), w0, w1, rank0, rank1]
    route = jnp.zeros(logits.shape, F32)
    for c, val in enumerate(cols):
        route = jnp.where(lane == c, val, route)
    route_ref[...] = route


def _router(x, g, mods, layer, w_route, b_route, tm=512):
    n, d = x.shape
    w_hi = w_route.astype(BF16)
    w_lo = (w_route - w_hi.astype(F32)).astype(BF16)
    return pl.pallas_call(
        _router_kernel,
        out_shape=(jax.ShapeDtypeStruct((n, d), F32),
                   jax.ShapeDtypeStruct((n, ROUTE_LANES), F32),
                   jax.ShapeDtypeStruct((1, ROUTE_LANES), F32)),
        grid=(n // tm,),
        in_specs=[pl.BlockSpec((tm, d), lambda i: (i, 0)),
                  pl.BlockSpec((1, d), lambda i: (0, 0)),
                  mods.spec(layer, 3, tm),
                  mods.spec(layer, 4, tm),
                  pl.BlockSpec((d, ROUTE_LANES), lambda i: (0, 0)),
                  pl.BlockSpec((d, ROUTE_LANES), lambda i: (0, 0)),
                  pl.BlockSpec((1, ROUTE_LANES), lambda i: (0, 0))],
        out_specs=(pl.BlockSpec((tm, d), lambda i: (i, 0)),
                   pl.BlockSpec((tm, ROUTE_LANES), lambda i: (i, 0)),
                   pl.BlockSpec((1, ROUTE_LANES), lambda i: (0, 0))),
        scratch_shapes=[pltpu.VMEM((1, ROUTE_LANES), F32)],
        compiler_params=_cparams(("arbitrary",)),
        name="moe_router",
    )(x, g.reshape(1, d), mods.mods, mods.mods, w_hi, w_lo, b_route)


def _dispatch_kernel(pcnt_ref, pend_ref, dest_ref, h_ref, xs_ref, zbuf, sem, zsem):
    tm = h_ref.shape[0]
    zrows = zbuf.shape[0]

    @pl.when(pl.program_id(0) == 0)
    def _():
        zbuf[...] = jnp.zeros_like(zbuf)

        def clear(e, _):
            @pl.when(pcnt_ref[e] > 0)
            def _():
                start = pl.multiple_of(pend_ref[e] - zrows, zrows)
                cp = pltpu.make_async_copy(zbuf, xs_ref.at[pl.ds(start, zrows)], zsem)
                cp.start()
                cp.wait()
            return 0

        lax.fori_loop(0, MOE_EXPERTS, clear, 0)

    def issue(r, _):
        for k in range(2):
            d = dest_ref[0, 0, 2 * r + k]
            pltpu.make_async_copy(h_ref.at[pl.ds(r, 1)], xs_ref.at[pl.ds(d, 1)], sem).start(priority=k)
        return 0

    lax.fori_loop(0, tm, issue, 0, unroll=ROW_ISSUE_UNROLL)
    for _ in range(2):
        pltpu.make_async_copy(h_ref, xs_ref.at[pl.ds(0, tm)], sem).wait()


def _dispatch(h, dest, pcnt, pends, n_pad, tm=DISPATCH_TM):
    n, d = h.shape
    nblk = n // tm
    grid_spec = pltpu.PrefetchScalarGridSpec(
        num_scalar_prefetch=2,
        grid=(nblk,),
        in_specs=[pl.BlockSpec((1, 1, 2 * tm), lambda i, *_: (i, 0, 0), memory_space=pltpu.SMEM),
                  pl.BlockSpec((tm, d), lambda i, *_: (i, 0))],
        out_specs=pl.BlockSpec(memory_space=pl.ANY),
        scratch_shapes=[pltpu.VMEM((MOE_TM, d), F32), pltpu.SemaphoreType.DMA(()),
                        pltpu.SemaphoreType.DMA(())],
    )
    return pl.pallas_call(
        _dispatch_kernel,
        out_shape=jax.ShapeDtypeStruct((n_pad, d), F32),
        grid_spec=grid_spec,
        compiler_params=_cparams(("arbitrary",)),
        name="moe_dispatch",
    )(pcnt, pends, dest.reshape(nblk, 1, 2 * tm), h)


def _experts_kernel(be_ref, first_ref, nxt_ref, nu_ref, x_ref, wg_hbm, wu_hbm, wd_hbm, o_ref,
                    stg_g, stg_u, stg_d, wg_bf, wu_bf, wd_bf, sem, *, layer, cast_rows):
    i = pl.program_id(0)
    active = i < nu_ref[0]
    stages = ((wg_hbm, stg_g, wg_bf), (wu_hbm, stg_u, wu_bf), (wd_hbm, stg_d, wd_bf))

    def weight_copies(e):
        return [pltpu.make_async_copy(hbm.at[layer, e], stg, sem.at[k])
                for k, (hbm, stg, _) in enumerate(stages)]

    @pl.when(i == 0)
    def _():
        for cp in weight_copies(be_ref[0]):
            cp.start()

    @pl.when(active & (first_ref[i] == 1))
    def _():
        for cp in weight_copies(be_ref[i]):
            cp.wait()
        for _, stg, wbf in stages:
            def cast(r, _, stg=stg, wbf=wbf):
                rows = pl.ds(pl.multiple_of(r * cast_rows, cast_rows), cast_rows)
                wbf[rows, :] = stg[rows, :].astype(BF16)
                return 0

            lax.fori_loop(0, stg.shape[0] // cast_rows, cast, 0)

        @pl.when(nxt_ref[i] >= 0)
        def _():
            for cp in weight_copies(nxt_ref[i]):
                cp.start()

    @pl.when(active)
    def _():
        x = x_ref[...].astype(BF16)
        g = jnp.dot(x, wg_bf[...], preferred_element_type=F32)
        u = jnp.dot(x, wu_bf[...], preferred_element_type=F32)
        a = (_silu(g) * u).astype(BF16)
        o_ref[...] = jnp.dot(a, wd_bf[...], preferred_element_type=F32)

    @pl.when(jnp.logical_not(active))
    def _():
        o_ref[...] = jnp.zeros_like(o_ref)


def _experts(xs, block_expert, block_first, block_next, n_used, w_gate, w_up, w_down, layer, tm=MOE_TM):
    n_pad, d = xs.shape
    f = w_gate.shape[-1]
    nblk = n_pad // tm
    grid_spec = pltpu.PrefetchScalarGridSpec(
        num_scalar_prefetch=4,
        grid=(nblk,),
        in_specs=[pl.BlockSpec((tm, d), lambda i, be, fi, nx, nu: (jnp.minimum(i, nu[0] - 1), 0)),
                  pl.BlockSpec(memory_space=pl.ANY),
                  pl.BlockSpec(memory_space=pl.ANY),
                  pl.BlockSpec(memory_space=pl.ANY)],
        out_specs=pl.BlockSpec((tm, d), lambda i, *_: (i, 0)),
        scratch_shapes=[pltpu.VMEM((d, f), F32), pltpu.VMEM((d, f), F32), pltpu.VMEM((f, d), F32),
                        pltpu.VMEM((d, f), BF16), pltpu.VMEM((d, f), BF16), pltpu.VMEM((f, d), BF16),
                        pltpu.SemaphoreType.DMA((3,))],
    )
    return pl.pallas_call(
        functools.partial(_experts_kernel, layer=layer, cast_rows=256),
        out_shape=jax.ShapeDtypeStruct((n_pad, d), F32),
        grid_spec=grid_spec,
        compiler_params=_cparams(("arbitrary",), vmem=EXPERT_VMEM_LIMIT_BYTES),
        name="moe_experts",
    )(block_expert, block_first, block_next, n_used, xs, w_gate, w_up, w_down)


def _combine_kernel(dest_ref, dnext_ref, y_hbm, x_ref, route_ref, gate_ref, o_ref, buf, sem):
    i = pl.program_id(0)
    nblk = pl.num_programs(0)
    tm = x_ref.shape[0]

    def issue_block(idx_ref, slot):
        def issue(r, _):
            for k in range(2):
                d = idx_ref[0, 0, 2 * r + k]
                pltpu.make_async_copy(y_hbm.at[pl.ds(d, 1)], buf.at[slot, k, pl.ds(r, 1)],
                                      sem.at[slot]).start(priority=k)
            return 0

        lax.fori_loop(0, tm, issue, 0, unroll=ROW_ISSUE_UNROLL)

    @pl.when(i == 0)
    def _():
        issue_block(dest_ref, 0)

    @pl.when(i + 1 < nblk)
    def _():
        issue_block(dnext_ref, (i + 1) % 2)

    slot = i % 2
    for k in range(2):
        pltpu.make_async_copy(y_hbm.at[pl.ds(0, tm)], buf.at[slot, k], sem.at[slot]).wait()
    route = route_ref[...]
    w0 = route[:, _R_W:_R_W + 1]
    w1 = route[:, _R_W + 1:_R_W + 2]
    o_ref[...] = x_ref[...] + gate_ref[...] * (buf[slot, 0] * w0 + buf[slot, 1] * w1)


def _combine(x, y, dest, route, mods, layer, tm=ROW_TM):
    n, d = x.shape
    nblk = n // tm
    dest3 = dest.reshape(nblk, 1, 2 * tm)
    return pl.pallas_call(
        _combine_kernel,
        out_shape=jax.ShapeDtypeStruct((n, d), F32),
        grid=(nblk,),
        in_specs=[pl.BlockSpec((1, 1, 2 * tm), lambda i: (i, 0, 0), memory_space=pltpu.SMEM),
                  pl.BlockSpec((1, 1, 2 * tm), lambda i: (jnp.minimum(i + 1, nblk - 1), 0, 0),
                               memory_space=pltpu.SMEM),
                  pl.BlockSpec(memory_space=pl.ANY),
                  pl.BlockSpec((tm, d), lambda i: (i, 0)),
                  pl.BlockSpec((tm, ROUTE_LANES), lambda i: (i, 0)),
                  mods.spec(layer, 5, tm)],
        out_specs=pl.BlockSpec((tm, d), lambda i: (i, 0)),
        scratch_shapes=[pltpu.VMEM((2, 2, tm, d), F32), pltpu.SemaphoreType.DMA((2,))],
        compiler_params=_cparams(("arbitrary",)),
        name="moe_combine",
    )(dest3, dest3, y, x, route, mods.mods)


def _moe_layer(x, mods, layer, norm_g, w_group, b_group, w_expert, b_expert, w_gate, w_up, w_down):
    n, d = x.shape
    ne = MOE_EXPERTS
    pad = ROUTE_LANES - MOE_GROUPS - ne
    w_route = jnp.concatenate([w_group, w_expert, jnp.zeros((d, pad), F32)], axis=1)
    b_route = jnp.concatenate([b_group, b_expert, jnp.zeros((pad,), F32)]).reshape(1, ROUTE_LANES)
    h, route, counts = _router(x, norm_g, mods, layer, w_route, b_route)
    cnt = counts[0, _R_LOGIT0:_R_LOGIT0 + ne].astype(jnp.int32)
    pcnt = (cnt + MOE_TM - 1) // MOE_TM * MOE_TM
    pends = jnp.cumsum(pcnt)
    pstarts = pends - pcnt
    experts = jnp.arange(ne, dtype=jnp.int32)
    eid = route[:, _R_EID:_R_EID + 2].astype(jnp.int32)
    rank = route[:, _R_RANK:_R_RANK + 2].astype(jnp.int32)
    dest = (jnp.sum(jnp.where(eid[..., None] == experts, pstarts, 0), axis=-1) + rank).reshape(-1)
    n_blocks = (n * 2) // MOE_TM + ne
    n_pad = n_blocks * MOE_TM
    block_row = jnp.arange(n_blocks, dtype=jnp.int32) * MOE_TM
    block_expert = jnp.minimum(jnp.sum((pends[None, :] <= block_row[:, None]).astype(jnp.int32), axis=1), ne - 1)
    block_first = jnp.concatenate([jnp.ones((1,), jnp.int32),
                                   (block_expert[1:] != block_expert[:-1]).astype(jnp.int32)])
    later = (experts[None, :] > experts[:, None]) & (pcnt[None, :] > 0)
    next_expert = jnp.min(jnp.where(later, experts[None, :], ne), axis=1)
    next_expert = jnp.where(next_expert == ne, -1, next_expert)
    block_next = jnp.sum(jnp.where(block_expert[:, None] == experts, next_expert, 0), axis=1)
    n_used = (pends[-1:] // MOE_TM).astype(jnp.int32)
    xs = _dispatch(h, dest, pcnt, pends, n_pad)
    y = _experts(xs, block_expert, block_first, block_next, n_used, w_gate, w_up, w_down, layer)
    return _combine(x, y, dest, route, mods, layer)


class _Streams:
    def __init__(self, batch, seq, dec_batch, dec_seq):
        self.ctx = (0, batch, seq)
        self.lat = (batch * seq, dec_batch, dec_seq)


def _s5_layer(x, mods, layer, streams, norm_g, lam_re, lam_im, log_step, b_re, b_im, c_re, c_im, d_skip,
              w_a, b_a, w_b, b_b, s0_re, s0_im):
    n, d = x.shape
    g, p = lam_re.shape[1:]
    operands = _s5_params(lam_re, lam_im, log_step, b_re, b_im, c_re, c_im)
    h = _adanorm(x, norm_g, mods, layer, 0, 1)
    dvec = d_skip.reshape(1, d)
    row0, nseq, seq = streams.ctx
    zero = jnp.zeros((2, d // LANES, nseq, operands[3].shape[-1]), F32)
    z, sf = _s5_scan(h, operands, dvec, zero, row0, nseq, seq)
    row0, nseq, seq = streams.lat
    z, _ = _s5_scan(h, operands, dvec, _s5_state_to_lanes(s0_re, s0_im), row0, nseq, seq, z_prev=z)
    x = _mm_residual(z, [w_a.astype(BF16), w_b.astype(BF16)], [b_a, b_b], x, mods, layer, 2)
    new_re, new_im = _s5_state_from_lanes(sf, g, p)
    return x, new_re, new_im


def _mlstm_layer(x, mods, layer, streams, norm_g, w_in, b_gates, head_g, w_out, c0, n0, m0, c_slot):
    n, d = x.shape
    hh = ML_HEADS
    dv = d // hh
    dk = dv // 2
    qk = hh * dk
    n_main = 2 * qk + 2 * d
    col_scale = jnp.concatenate([jnp.ones((qk,), F32), jnp.full((qk,), dk ** -0.5, F32),
                                 jnp.ones((2 * d,), F32)])
    w_main = (w_in[:, :n_main] * col_scale).astype(BF16)
    w_gates = jnp.pad(w_in[:, n_main:], ((0, 0), (0, LANES - 4 * hh)))
    proj, gates = _mm_adanorm(x, norm_g, mods, layer, 0, 1, w_main, w_gates)
    k_t = jnp.transpose(proj[:, qk:2 * qk])
    g_c, w_c, e_c, d_r, w_r = _gate_prep(gates, b_gates, ML_T)
    fwd, bwd = slice(_G_FWD, _G_FWD + hh), slice(_G_BWD, _G_BWD + hh)
    cols = jnp.stack([g_c[:, fwd], w_c[:, fwd], e_c[:, fwd], g_c[:, bwd], w_c[:, bwd], e_c[:, bwd]], axis=-1)
    cols = jnp.pad(cols, ((0, 0), (0, 0), (0, 2)))
    rows = jnp.stack([d_r[fwd], w_r[fwd], d_r[bwd], w_r[bwd]], axis=1)

    def gate_views(row0, nseq, seq):
        gc = cols[row0:row0 + nseq * seq].reshape(nseq, seq, hh, 8)
        gr = rows[:, :, row0:row0 + nseq * seq].reshape(hh, 4, nseq, seq)
        return jnp.transpose(gc, (0, 2, 1, 3)), jnp.transpose(gr, (2, 0, 1, 3))

    def n_lanes(nv):
        return jnp.pad(jnp.transpose(nv, (0, 2, 1, 3))[..., None], ((0, 0),) * 4 + ((0, LANES - 1),))

    def m_lanes(mv):
        return jnp.broadcast_to(jnp.transpose(mv, (0, 2, 1))[..., None], mv.shape[:1] + (hh, 2, LANES))

    row0, nseq, seq = streams.ctx
    gcol, grow = gate_views(row0, nseq, seq)
    hn, c_f, n_f, m_f = _mlstm(
        proj, k_t, gcol, grow, head_g,
        jnp.zeros((nseq, 2, hh, dk, dv), F32), jnp.zeros((nseq, hh, 2, dk, LANES), F32),
        jnp.full((nseq, hh, 2, LANES), NEG_BIG, F32), row0, nseq, seq, c_slot=c_slot)
    row0, nseq, seq = streams.lat
    gcol, grow = gate_views(row0, nseq, seq)
    hn, _, _, _ = _mlstm(proj, k_t, gcol, grow, head_g, c0, n_lanes(n0), m_lanes(m0),
                         row0, nseq, seq, hn_prev=hn, hp=ML_HP_LONG)
    x = _mm_residual(hn, [w_out.astype(BF16)], None, x, mods, layer, 2)
    return x, c_f, jnp.transpose(n_f[..., 0], (0, 2, 1, 3)), jnp.transpose(m_f[..., 0], (0, 2, 1))


def kernel(x_prompt, x_sample, state_s5_re, state_s5_im, state_mlstm_C, state_mlstm_n, state_mlstm_m, c, c_ctx, w_ada, b_ada, norm1_g, norm2_g, final_norm_g, s5_lambda_re, s5_lambda_im, s5_log_step, s5_b_re, s5_b_im, s5_c_re, s5_c_im, s5_d, s5_w_glu_a, s5_b_glu_a, s5_w_glu_b, s5_b_glu_b, ml_w_in, ml_b_gates, ml_head_norm_g, ml_w_out, moe_w_group, moe_b_group, moe_w_expert, moe_b_expert, moe_w_gate, moe_w_up, moe_w_down):
    batch, seq, d = x_prompt.shape
    dec_batch, dec_seq, _ = x_sample.shape
    depth = w_ada.shape[0]
    n_ctx = batch * seq
    streams = _Streams(batch, seq, dec_batch, dec_seq)

    cond = jnp.concatenate([c_ctx[None, :], c, jnp.zeros((MOD_ROWS - 1 - dec_batch, d), F32)], axis=0)
    mods = _modulation(cond, w_ada, b_ada).reshape(depth * MOD_ROWS * N_MOD, 1, d)
    mods = _Mods(mods, n_ctx, dec_seq)

    x = _embed(x_prompt, x_sample)
    new_s5_re, new_s5_im, new_n, new_m = [], [], [], []
    new_c = None
    n_ml_layers = depth // N_MIXERS
    for l in range(depth):
        j = l // N_MIXERS
        if l % N_MIXERS == 0:
            x, s_re, s_im = _s5_layer(
                x, mods, l, streams, norm1_g[l], s5_lambda_re[j], s5_lambda_im[j], s5_log_step[j],
                s5_b_re[j], s5_b_im[j], s5_c_re[j], s5_c_im[j], s5_d[j],
                s5_w_glu_a[j], s5_b_glu_a[j], s5_w_glu_b[j], s5_b_glu_b[j],
                state_s5_re[:, j], state_s5_im[:, j])
            new_s5_re.append(s_re)
            new_s5_im.append(s_im)
        else:
            x, new_c, n_f, m_f = _mlstm_layer(
                x, mods, l, streams, norm1_g[l], ml_w_in[j], ml_b_gates[j], ml_head_norm_g[j], ml_w_out[j],
                state_mlstm_C[:, j], state_mlstm_n[:, j], state_mlstm_m[:, j], (j, n_ml_layers, new_c))
            new_n.append(n_f)
            new_m.append(m_f)
        x = _moe_layer(x, mods, l, norm2_g[l], moe_w_group[l], moe_b_group[l], moe_w_expert[l],
                       moe_b_expert[l], moe_w_gate, moe_w_up, moe_w_down)
    y_prompt = _final_norm(x, final_norm_g, 0, n_ctx).reshape(batch, seq, d)
    y_sample = _final_norm(x, final_norm_g, n_ctx, dec_batch * dec_seq).reshape(dec_batch, dec_seq, d)
    return (y_prompt, y_sample, jnp.stack(new_s5_re, axis=1), jnp.stack(new_s5_im, axis=1),
            new_c, jnp.stack(new_n, axis=1), jnp.stack(new_m, axis=1))
```
